```python
import math
import jax, jax.numpy as jnp
from jax import lax
import numpy as np

D_MODEL = 2048
BATCH = 16
SEQ = 256
DEPTH = 2
DEC_BATCH = 2
DEC_SEQ = 1024
PAST_LEN = 256

GRID_W = 64
N_MIXERS = 2
N_A = (DEPTH + 1) // 2
N_B = DEPTH // 2
DN_DK = 128
DN_DV = 128
DN_HEADS_K = D_MODEL // DN_DK
DN_HEADS_V = 2 * DN_HEADS_K
DN_K_DIM = DN_HEADS_K * DN_DK
DN_V_DIM = DN_HEADS_V * DN_DV
DN_QKV_DIM = 2 * DN_K_DIM + DN_V_DIM
DN_PROJ = DN_QKV_DIM + DN_V_DIM + 4 * DN_HEADS_V
DN_CONV = 5
DN_CHUNK = 64
CM_DIM = 2 * D_MODEL
CM_CHUNK = 128
CM_GROUPS = 16
CM_GDIM = CM_DIM // CM_GROUPS
FF_DIM = 4 * D_MODEL
N_MOD = 6
EPS = 1e-6

kernel_name = 'hybrid_deltanet_chunkmlp_diffusion_step'


def rms_norm(x, w):
    xf = x.astype(jnp.float32)
    y = xf * lax.rsqrt(jnp.mean(xf * xf, axis=-1, keepdims=True) + EPS)
    return (y * w.astype(jnp.float32)).astype(x.dtype)


def layer_norm(x, w, b):
    xf = x.astype(jnp.float32)
    mu = jnp.mean(xf, axis=-1, keepdims=True)
    xc = xf - mu
    y = xc * lax.rsqrt(jnp.mean(xc * xc, axis=-1, keepdims=True) + EPS)
    return (y * w.astype(jnp.float32) + b.astype(jnp.float32)).astype(x.dtype)


def l2_norm(x):
    xf = x.astype(jnp.float32)
    return xf * lax.rsqrt(jnp.sum(xf * xf, axis=-1, keepdims=True) + EPS)


def grid_pos_embed(n_tokens, dtype):
    rows = n_tokens // GRID_W
    r = jnp.repeat(jnp.arange(rows), GRID_W).astype(jnp.float32)
    col = jnp.tile(jnp.arange(GRID_W), rows).astype(jnp.float32)
    quarter = D_MODEL // 4
    freq = 1.0 / (10000.0 ** (jnp.arange(quarter, dtype=jnp.float32) / quarter))
    ar = r[:, None] * freq[None, :]
    ac = col[:, None] * freq[None, :]
    pe = jnp.concatenate([jnp.sin(ar), jnp.cos(ar), jnp.sin(ac), jnp.cos(ac)], axis=-1)
    return pe.astype(dtype)


def adaln(cond, w_ada, b_ada):
    mod = (jax.nn.silu(cond) @ w_ada + b_ada)[:, None, :]
    return jnp.split(mod, N_MOD, axis=-1)


def modulate(h, shift, scale):
    return h * (1 + scale) + shift


def short_conv_centred(x, w):
    k_w = w.shape[0]
    pad = k_w // 2
    n = x.shape[1]
    xp = jnp.pad(x, ((0, 0), (pad, pad), (0, 0)))
    return sum(xp[:, j:j + n] * w[j] for j in range(k_w))


def gated_delta_chunked(q, k, v, log_g, beta, s0):
    b_sz, n_tok, n_h, dk = q.shape
    dv = v.shape[-1]
    c_len = DN_CHUNK
    n_ch = n_tok // c_len
    f32 = jnp.float32

    def chunks(t):
        t = t.astype(f32).reshape((b_sz, n_ch, c_len, n_h) + t.shape[3:])
        return jnp.moveaxis(t, (1, 3), (0, 2))

    qc, kc, vc = chunks(q), chunks(k), chunks(v)
    gc = jnp.cumsum(chunks(log_g), axis=-1)
    bc = chunks(beta)
    causal = jnp.tril(jnp.ones((c_len, c_len), dtype=bool))
    strict = jnp.tril(jnp.ones((c_len, c_len), dtype=bool), -1)
    decay = jnp.exp(jnp.where(causal, gc[..., :, None] - gc[..., None, :], -jnp.inf))
    kb = kc * bc[..., None]
    a_mat = jnp.where(strict, jnp.einsum('nbhik,nbhjk->nbhij', kb, kc) * decay, 0.0)
    t_sys = a_mat + jnp.eye(c_len, dtype=f32)
    rhs = jnp.concatenate([vc * bc[..., None], kb * jnp.exp(gc)[..., None]], axis=-1)
    sol = lax.linalg.triangular_solve(t_sys, rhs, left_side=True, lower=True, unit_diagonal=True)
    w_val, u_key = sol[..., :dv], sol[..., dv:]
    intra = jnp.einsum('nbhik,nbhjk->nbhij', qc, kc) * decay
    q_dec = qc * jnp.exp(gc)[..., None]
    g_last = gc[..., -1]
    k_dec = kc * jnp.exp(g_last[..., None] - gc)[..., None]

    def step(s, xs):
        w_i, u_i, intra_i, qd_i, kd_i, gl_i = xs
        v_new = w_i - jnp.einsum('bhck,bhkv->bhcv', u_i, s)
        o_i = jnp.einsum('bhck,bhkv->bhcv', qd_i, s) + jnp.einsum('bhij,bhjv->bhiv', intra_i, v_new)
        s = s * jnp.exp(gl_i)[..., None, None] + jnp.einsum('bhck,bhcv->bhkv', kd_i, v_new)
        return s, o_i

    s_fin, o = lax.scan(step, s0.astype(f32), (w_val, u_key, intra, q_dec, k_dec, g_last))
    o = jnp.moveaxis(o, (0, 2), (1, 3)).reshape(b_sz, n_tok, n_h, dv)
    return o, s_fin


def deltanet_mixer(h, s0_fwd, s0_bwd, w_in, conv_w, a_log, dt_bias, norm_w, w_out):
    b_sz, n_tok, _ = h.shape
    p = h @ w_in
    qkv = p[..., :DN_QKV_DIM]
    z = p[..., DN_QKV_DIM:DN_QKV_DIM + DN_V_DIM]
    ab = p[..., DN_QKV_DIM + DN_V_DIM:].reshape(b_sz, n_tok, 2, 2, DN_HEADS_V)
    qkv = jax.nn.silu(short_conv_centred(qkv, conv_w))
    q = qkv[..., :DN_K_DIM].reshape(b_sz, n_tok, DN_HEADS_K, DN_DK)
    k = qkv[..., DN_K_DIM:2 * DN_K_DIM].reshape(b_sz, n_tok, DN_HEADS_K, DN_DK)
    v = qkv[..., 2 * DN_K_DIM:].reshape(b_sz, n_tok, DN_HEADS_V, DN_DV)
    rep = DN_HEADS_V // DN_HEADS_K
    q = jnp.repeat(l2_norm(q) * (DN_DK ** -0.5), rep, axis=2)
    k = jnp.repeat(l2_norm(k), rep, axis=2)
    ab = ab.astype(jnp.float32)
    log_g = -jnp.exp(a_log.astype(jnp.float32)) * jax.nn.softplus(ab[..., 0, :] + dt_bias.astype(jnp.float32))
    beta = jax.nn.sigmoid(ab[..., 1, :])
    o_f, s_f = gated_delta_chunked(q, k, v, log_g[:, :, 0], beta[:, :, 0], s0_fwd)
    o_b, s_b = gated_delta_chunked(q[:, ::-1], k[:, ::-1], v[:, ::-1], log_g[:, ::-1, 1], beta[:, ::-1, 1], s0_bwd)
    o = o_f + o_b[:, ::-1]
    o = rms_norm(o, norm_w) * jax.nn.silu(z.reshape(b_sz, n_tok, DN_HEADS_V, DN_DV).astype(jnp.float32))
    out = o.reshape(b_sz, n_tok, DN_V_DIM).astype(h.dtype) @ w_out
    return out, s_f, s_b


def chunk_mlp_mixer(h, w_in, b_in, ln_w, ln_b, w_s, b_s, w_out):
    b_sz, n_tok, _ = h.shape
    zz = jax.nn.gelu(h @ w_in + b_in)
    u, v = jnp.split(zz, 2, axis=-1)
    v = layer_norm(v, ln_w, ln_b)
    v = v.reshape(b_sz, n_tok // CM_CHUNK, CM_CHUNK, CM_GROUPS, CM_GDIM)
    v = jnp.einsum('gpq,bnqgc->bnpgc', w_s, v) + b_s.T[None, None, :, :, None]
    return (u * v.reshape(b_sz, n_tok, CM_DIM)) @ w_out


def squared_relu_mlp(h, w1, w2):
    return jnp.square(jax.nn.relu(h @ w1)) @ w2


def setup_inputs(seed: int = 0) -> dict:
    key = jax.random.key(seed)
    ks = jax.random.split(key, 32)
    f32 = jnp.float32

    def nrm(k, shape, scale):
        return jax.random.normal(k, shape, f32) * scale

    dt = jnp.exp(jax.random.uniform(ks[13], (N_A, 2, DN_HEADS_V), f32, math.log(1e-3), math.log(1e-1)))
    return {
        'x_prompt': nrm(ks[0], (BATCH, SEQ, D_MODEL), 1.0),
        'x_sample': nrm(ks[1], (DEC_BATCH, DEC_SEQ, D_MODEL), 1.0),
        'state_dn_fwd': nrm(ks[2], (DEC_BATCH, N_A, DN_HEADS_V, DN_DK, DN_DV), 0.2),
        'state_dn_bwd': nrm(ks[3], (DEC_BATCH, N_A, DN_HEADS_V, DN_DK, DN_DV), 0.2),
        'c': nrm(ks[4], (DEC_BATCH, D_MODEL), 1.0),
        'c_ctx': nrm(ks[5], (D_MODEL,), 1.0),
        'norm_mix_w': 1.0 + nrm(ks[6], (DEPTH, D_MODEL), 0.01),
        'norm_mlp_w': 1.0 + nrm(ks[7], (DEPTH, D_MODEL), 0.01),
        'w_ada': nrm(ks[8], (DEPTH, D_MODEL, N_MOD * D_MODEL), 0.5 * D_MODEL ** -0.5),
        'b_ada': nrm(ks[9], (DEPTH, N_MOD * D_MODEL), 0.01),
        'dn_w_in': nrm(ks[10], (N_A, D_MODEL, DN_PROJ), D_MODEL ** -0.5),
        'dn_conv_w': nrm(ks[11], (N_A, DN_CONV, DN_QKV_DIM), DN_CONV ** -0.5),
        'dn_A_log': jnp.log(jax.random.uniform(ks[12], (N_A, 2, DN_HEADS_V), f32, 1.0, 16.0)),
        'dn_dt_bias': dt + jnp.log(-jnp.expm1(-dt)),
        'dn_norm_w': 1.0 + nrm(ks[14], (N_A, DN_DV), 0.01),
        'dn_w_out': nrm(ks[15], (N_A, DN_V_DIM, D_MODEL), DN_V_DIM ** -0.5),
        'cm_w_in': nrm(ks[16], (N_B, D_MODEL, 2 * CM_DIM), D_MODEL ** -0.5),
        'cm_b_in': nrm(ks[17], (N_B, 2 * CM_DIM), 0.01),
        'cm_ln_w': 1.0 + nrm(ks[18], (N_B, CM_DIM), 0.01),
        'cm_ln_b': nrm(ks[19], (N_B, CM_DIM), 0.01),
        'cm_w_s': nrm(ks[20], (N_B, CM_GROUPS, CM_CHUNK, CM_CHUNK), CM_CHUNK ** -0.5),
        'cm_b_s': 1.0 + nrm(ks[21], (N_B, CM_GROUPS, CM_CHUNK), 0.01),
        'cm_w_out': nrm(ks[22], (N_B, CM_DIM, D_MODEL), CM_DIM ** -0.5),
        'w_ff1': nrm(ks[23], (DEPTH, D_MODEL, FF_DIM), D_MODEL ** -0.5),
        'w_ff2': nrm(ks[24], (DEPTH, FF_DIM, D_MODEL), FF_DIM ** -0.5),
        'final_norm_w': 1.0 + nrm(ks[25], (D_MODEL,), 0.01),
    }


def reference(x_prompt, x_sample, state_dn_fwd, state_dn_bwd, c, c_ctx, norm_mix_w, norm_mlp_w,
              w_ada, b_ada, dn_w_in, dn_conv_w, dn_A_log, dn_dt_bias, dn_norm_w, dn_w_out,
              cm_w_in, cm_b_in, cm_ln_w, cm_ln_b, cm_w_s, cm_b_s, cm_w_out, w_ff1, w_ff2, final_norm_w):
    ctx = x_prompt
    lat = x_sample + grid_pos_embed(x_sample.shape[1], x_sample.dtype)[None]
    cond_ctx = c_ctx[None, :]
    zero_state = jnp.zeros((x_prompt.shape[0], DN_HEADS_V, DN_DK, DN_DV), jnp.float32)
    new_fwd, new_bwd = [], []
    for i in range(DEPTH):
        j = i // N_MIXERS
        sh_c, sc_c, g_c, sh2_c, sc2_c, g2_c = adaln(cond_ctx, w_ada[i], b_ada[i])
        sh_l, sc_l, g_l, sh2_l, sc2_l, g2_l = adaln(c, w_ada[i], b_ada[i])
        h_c = modulate(rms_norm(ctx, norm_mix_w[i]), sh_c, sc_c)
        h_l = modulate(rms_norm(lat, norm_mix_w[i]), sh_l, sc_l)
        if i % N_MIXERS == 0:
            dn = (dn_w_in[j], dn_conv_w[j], dn_A_log[j], dn_dt_bias[j], dn_norm_w[j], dn_w_out[j])
            m_c, s_f, s_b = deltanet_mixer(h_c, zero_state, zero_state, *dn)
            m_l, _, _ = deltanet_mixer(h_l, state_dn_fwd[:, j], state_dn_bwd[:, j], *dn)
            new_fwd.append(s_f.astype(x_prompt.dtype))
            new_bwd.append(s_b.astype(x_prompt.dtype))
        else:
            cm = (cm_w_in[j], cm_b_in[j], cm_ln_w[j], cm_ln_b[j], cm_w_s[j], cm_b_s[j], cm_w_out[j])
            m_c = chunk_mlp_mixer(h_c, *cm)
            m_l = chunk_mlp_mixer(h_l, *cm)
        ctx = ctx + g_c * m_c
        lat = lat + g_l * m_l
        ctx = ctx + g2_c * squared_relu_mlp(modulate(rms_norm(ctx, norm_mlp_w[i]), sh2_c, sc2_c), w_ff1[i], w_ff2[i])
        lat = lat + g2_l * squared_relu_mlp(modulate(rms_norm(lat, norm_mlp_w[i]), sh2_l, sc2_l), w_ff1[i], w_ff2[i])
    y_prompt = rms_norm(ctx, final_norm_w)
    y_sample = rms_norm(lat, final_norm_w)
    new_state_dn_fwd = jnp.stack(new_fwd, axis=1)
    new_state_dn_bwd = jnp.stack(new_bwd, axis=1)
    return (y_prompt, y_sample, new_state_dn_fwd, new_state_dn_bwd)
```

```python
import functools
import math

import jax
import jax.numpy as jnp
from jax import lax
from jax.experimental import pallas as pl
from jax.experimental.pallas import tpu as pltpu

F32 = jnp.float32
BF16 = jnp.bfloat16

D_MODEL = 2048
BATCH = 16
SEQ = 256
DEPTH = 2
DEC_BATCH = 2
DEC_SEQ = 1024
GRID_W = 64
N_MIXERS = 2
DN_DK = 128
DN_DV = 128
DN_HEADS_K = D_MODEL // DN_DK
DN_HEADS_V = 2 * DN_HEADS_K
DN_K_DIM = DN_HEADS_K * DN_DK
DN_V_DIM = DN_HEADS_V * DN_DV
DN_QKV_DIM = 2 * DN_K_DIM + DN_V_DIM
DN_GATE_COLS = 4 * DN_HEADS_V
DN_CONV = 5
DN_CHUNK = 64
CM_DIM = 2 * D_MODEL
CM_CHUNK = 128
CM_GROUPS = 16
CM_GDIM = CM_DIM // CM_GROUPS
FF_DIM = 4 * D_MODEL
N_MOD = 6
EPS = 1e-6

N_CTX_TOK = BATCH * SEQ
N_LAT_TOK = DEC_BATCH * DEC_SEQ
N_TOK = N_CTX_TOK + N_LAT_TOK
N_COND = 8
CONV_HALO = 8
VMEM_LIMIT_BYTES = 56 * 1024 * 1024


def _group_of_row(row0):
    return jnp.where(row0 < N_CTX_TOK, 0, 1 + (row0 - N_CTX_TOK) // DEC_SEQ)


def _params(*sem):
    return pltpu.CompilerParams(dimension_semantics=sem, vmem_limit_bytes=VMEM_LIMIT_BYTES)


def _rms(x, w):
    return x * lax.rsqrt(jnp.mean(x * x, axis=-1, keepdims=True) + EPS) * w


def _silu(x):
    return x * jax.nn.sigmoid(x)


def _gelu_tanh(x):
    return 0.5 * x * (1.0 + jnp.tanh(math.sqrt(2.0 / math.pi) * (x + 0.044715 * (x * x * x))))


def _adaln_kernel(c_ref, w_ref, b_ref, o_ref):
    x = _silu(c_ref[...]).astype(BF16)
    acc = jnp.dot(x, w_ref[...].astype(BF16), preferred_element_type=F32)
    o_ref[...] = acc + b_ref[...]


def adaln_all(cond, w_ada, b_ada, tn=1024):
    depth, d, n = w_ada.shape
    return pl.pallas_call(
        _adaln_kernel,
        grid=(depth, n // tn),
        in_specs=[
            pl.BlockSpec((N_COND, d), lambda l, j: (0, 0)),
            pl.BlockSpec((None, d, tn), lambda l, j: (l, 0, j)),
            pl.BlockSpec((None, 1, tn), lambda l, j: (l, 0, j)),
        ],
        out_specs=pl.BlockSpec((None, N_COND, tn), lambda l, j: (l, 0, j)),
        out_shape=jax.ShapeDtypeStruct((depth, N_COND, n), F32),
        compiler_params=_params("arbitrary", "arbitrary"),
        name="adaln",
    )(cond, w_ada, b_ada.reshape(depth, 1, n))


def _normmod_kernel(x_ref, w_ref, sh_ref, sc_ref, o_ref):
    y = _rms(x_ref[...], w_ref[...])
    o_ref[...] = (y * (1.0 + sc_ref[...]) + sh_ref[...]).astype(BF16)


def norm_modulate(x, norm_w, layer, shift, scale, tm=512):
    m, d = x.shape
    mod_spec = pl.BlockSpec((None, 1, d), lambda i: (_group_of_row(i * tm), 0, 0))
    return pl.pallas_call(
        _normmod_kernel,
        grid=(m // tm,),
        in_specs=[
            pl.BlockSpec((tm, d), lambda i: (i, 0)),
            pl.BlockSpec((None, 1, d), lambda i: (layer, 0, 0)),
            mod_spec,
            mod_spec,
        ],
        out_specs=pl.BlockSpec((tm, d), lambda i: (i, 0)),
        out_shape=jax.ShapeDtypeStruct((m, d), BF16),
        compiler_params=_params("arbitrary"),
        name="norm_modulate",
    )(x, norm_w.reshape(norm_w.shape[0], 1, d), shift, scale)


def _mm_kernel(x_ref, w_ref, *refs, n_extra, epilogue):
    extra = refs[:n_extra]
    o_ref = refs[n_extra]
    wb_ref = refs[n_extra + 1]

    @pl.when(pl.program_id(1) == 0)
    def _():
        wb_ref[...] = w_ref[...].astype(BF16)

    acc = jnp.dot(x_ref[...], wb_ref[...], preferred_element_type=F32)
    o_ref[...] = epilogue(acc, *[r[...] for r in extra]).astype(o_ref.dtype)


def matmul_ws(x, w, layer, n_out, *, tm, tn, epilogue, extras=(), out_dtype=F32, col_block0=0, name="matmul"):
    m, k = x.shape
    in_specs = [
        pl.BlockSpec((tm, k), lambda j, i: (i, 0)),
        pl.BlockSpec((None, k, tn), lambda j, i: (layer, 0, j + col_block0)),
    ] + [spec for _, spec in extras]
    return pl.pallas_call(
        functools.partial(_mm_kernel, n_extra=len(extras), epilogue=epilogue),
        grid=(n_out // tn, m // tm),
        in_specs=in_specs,
        out_specs=pl.BlockSpec((tm, tn), lambda j, i: (i, j)),
        out_shape=jax.ShapeDtypeStruct((m, n_out), out_dtype),
        scratch_shapes=[pltpu.VMEM((k, tn), BF16)],
        compiler_params=_params("arbitrary", "arbitrary"),
        name=name,
    )(x, w, *[a for a, _ in extras])


def _ep_plain(acc):
    return acc


def _ep_bias_gelu(acc, b):
    return _gelu_tanh(acc + b)


def _ep_resid_gate(acc, resid, gate):
    return resid + gate * acc


def matmul_resid_gate(x, w, layer, resid, gate, *, tm, tn, name):
    n_out = resid.shape[1]
    extras = (
        (resid, pl.BlockSpec((tm, tn), lambda j, i: (i, j))),
        (gate, pl.BlockSpec((None, 1, tn), lambda j, i: (_group_of_row(i * tm), 0, j))),
    )
    return matmul_ws(x, w, layer, n_out, tm=tm, tn=tn, epilogue=_ep_resid_gate, extras=extras, name=name)


def _conv_kernel(main_ref, prev_ref, next_ref, w_ref, o_ref, *, rb, normalize, n_q_tiles):
    i = pl.program_id(0)
    j = pl.program_id(1)
    row0 = i * rb
    in_lat = row0 >= N_CTX_TOK
    first = jnp.logical_or(jnp.logical_not(in_lat), (row0 - N_CTX_TOK) % DEC_SEQ == 0)
    last = jnp.logical_or(jnp.logical_not(in_lat), (row0 + rb - N_CTX_TOK) % DEC_SEQ == 0)
    prev = jnp.where(first, 0.0, prev_ref[...])
    nxt = jnp.where(last, 0.0, next_ref[...])
    ext = jnp.concatenate([prev, main_ref[...], nxt], axis=0)
    n_ext = rb + 2 * CONV_HALO
    pad = DN_CONV // 2
    acc = None
    for t in range(DN_CONV):
        shifted = ext if t == pad else pltpu.roll(ext, (pad - t) % n_ext, 0)
        term = shifted[CONV_HALO:CONV_HALO + rb] * w_ref[t:t + 1, :]
        acc = term if acc is None else acc + term
    y = _silu(acc)
    if normalize:
        scale = jnp.where(j < n_q_tiles, DN_DK ** -0.5, 1.0)
        tc = y.shape[1]
        for hh in range(tc // DN_DK):
            sl = y[:, hh * DN_DK:(hh + 1) * DN_DK]
            ss = jnp.sum(sl * sl, axis=-1, keepdims=True)
            o_ref[:, hh * DN_DK:(hh + 1) * DN_DK] = (sl * (lax.rsqrt(ss + EPS) * scale)).astype(o_ref.dtype)
    else:
        o_ref[...] = y.astype(o_ref.dtype)


def conv_silu(p, conv_w, layer, col0, n_cols, *, normalize, out_dtype, rb=SEQ, tc=512):
    assert rb == SEQ and DEC_SEQ % rb == 0 and col0 % tc == 0 and n_cols % tc == 0
    m = p.shape[0]
    cb0 = col0 // tc
    hb = rb // CONV_HALO
    n_hblk = m // CONV_HALO
    return pl.pallas_call(
        functools.partial(_conv_kernel, rb=rb, normalize=normalize, n_q_tiles=DN_K_DIM // tc),
        grid=(m // rb, n_cols // tc),
        in_specs=[
            pl.BlockSpec((rb, tc), lambda i, j: (i, cb0 + j)),
            pl.BlockSpec((CONV_HALO, tc), lambda i, j: (jnp.maximum(i * hb - 1, 0), cb0 + j)),
            pl.BlockSpec((CONV_HALO, tc), lambda i, j: (jnp.minimum((i + 1) * hb, n_hblk - 1), cb0 + j)),
            pl.BlockSpec((None, DN_CONV, tc), lambda i, j: (layer, 0, cb0 + j)),
        ],
        out_specs=pl.BlockSpec((rb, tc), lambda i, j: (i, j)),
        out_shape=jax.ShapeDtypeStruct((m, n_cols), out_dtype),
        compiler_params=_params("arbitrary", "arbitrary"),
        name="dn_conv_norm" if normalize else "dn_conv",
    )(p, p, p, conv_w)


def _split3(x):
    hi = x.astype(BF16)
    r1 = x - hi.astype(F32)
    mid = r1.astype(BF16)
    lo = (r1 - mid.astype(F32)).astype(BF16)
    return hi, mid, lo


def _gates_kernel(ab_ref, alog_ref, dtb_ref, o_ref):
    ab = ab_ref[...]
    log_g = -jnp.exp(alog_ref[...]) * jax.nn.softplus(ab + dtb_ref[...])
    beta = jax.nn.sigmoid(ab)
    c = ab.shape[0]
    ii = lax.broadcasted_iota(jnp.int32, (c, c), 0)
    jj = lax.broadcasted_iota(jnp.int32, (c, c), 1)
    tril = (ii >= jj).astype(BF16)
    triu = (ii <= jj).astype(BF16)
    parts = _split3(log_g)
    cum_f = sum(jnp.dot(tril, part, preferred_element_type=F32) for part in parts)
    cum_b = sum(jnp.dot(triu, part, preferred_element_type=F32) for part in parts)
    lane = lax.broadcasted_iota(jnp.int32, ab.shape, 1)
    kind = lane // DN_HEADS_V
    o_ref[...] = jnp.where(kind == 0, cum_f, jnp.where(kind == 2, cum_b, beta))


def dn_gates(ab, a_log_row, dt_bias_row):
    m, n = ab.shape
    return pl.pallas_call(
        _gates_kernel,
        grid=(m // DN_CHUNK,),
        in_specs=[
            pl.BlockSpec((DN_CHUNK, n), lambda i: (i, 0)),
            pl.BlockSpec((1, n), lambda i: (0, 0)),
            pl.BlockSpec((1, n), lambda i: (0, 0)),
        ],
        out_specs=pl.BlockSpec((DN_CHUNK, n), lambda i: (i, 0)),
        out_shape=jax.ShapeDtypeStruct((m, n), F32),
        compiler_params=_params("arbitrary"),
        name="dn_gates",
    )(ab, a_log_row, dt_bias_row)


def _mm_bf16(a, b):
    return jnp.dot(a.astype(BF16), b.astype(BF16), preferred_element_type=F32)


def _inv_unit_triangular(a):
    c = a.shape[0]
    ii = lax.broadcasted_iota(jnp.int32, (c, c), 0)
    jj = lax.broadcasted_iota(jnp.int32, (c, c), 1)
    x = jnp.where(ii == jj, 1.0, 0.0) - jnp.where((ii >> 1) == (jj >> 1), a, 0.0)
    for level in range(1, int(math.log2(c))):
        joins = jnp.logical_and((ii >> (level + 1)) == (jj >> (level + 1)), (ii >> level) != (jj >> level))
        n = jnp.where(joins, a, 0.0)
        x = x - _mm_bf16(_mm_bf16(x, n), x)
    return x


def _dn_core_kernel(*refs, n_chunks, has_init, has_state_out):
    q_ref, k_ref, v_ref, z_ref, gc_ref, gr_ref, nw_ref = refs[:7]
    pos = 7
    if has_init:
        s0f_ref, s0b_ref, _og_prev = refs[pos:pos + 3]
        pos += 3
    og_ref = refs[pos]
    pos += 1
    if has_state_out:
        sf_ref, sb_ref = refs[pos:pos + 2]
        pos += 2
    oacc_ref, s_ref = refs[pos:pos + 2]

    c = DN_CHUNK
    oacc_ref[...] = jnp.zeros_like(oacc_ref)
    if has_init:
        s_ref[0] = s0f_ref[0]
        s_ref[1] = s0f_ref[1]
        s_ref[2] = s0b_ref[0]
        s_ref[3] = s0b_ref[1]
    else:
        s_ref[...] = jnp.zeros_like(s_ref)

    ii = lax.broadcasted_iota(jnp.int32, (c, c), 0)
    jj = lax.broadcasted_iota(jnp.int32, (c, c), 1)

    def chunk_step(chunk, d):
        rows = pl.ds(pl.multiple_of(chunk * c, c), c)
        qc = q_ref[rows, :]
        kc = k_ref[rows, :]
        qk = jnp.concatenate([qc, kc], axis=0)
        gram = lax.dot_general(qk, kc, (((1,), (1,)), ((), ())), preferred_element_type=F32)
        qk_t = gram[:c]
        kk_t = gram[c:]
        s_pair = jnp.concatenate([s_ref[2 * d], s_ref[2 * d + 1]], axis=1).astype(BF16)
        qk_s = jnp.dot(qk, s_pair, preferred_element_type=F32)
        incl = (ii >= jj) if d == 0 else (ii <= jj)
        strict = (ii > jj) if d == 0 else (ii < jj)
        gcv = gc_ref[rows, :]
        grv = gr_ref[chunk]
        for hh in range(2):
            cg = 4 * d + hh
            cb = 4 * d + 2 + hh
            gcol = gcv[:, cg:cg + 1]
            bcol = gcv[:, cb:cb + 1]
            grow = grv[cg:cg + 1, :]
            decay = jnp.exp(jnp.where(incl, gcol - grow, -1e30))
            a_mat = jnp.where(strict, bcol * kk_t * decay, 0.0)
            p_mat = qk_t * decay
            t_inv = _inv_unit_triangular(a_mat)
            eg = jnp.exp(gcol)
            lanes = slice(hh * DN_DV, (hh + 1) * DN_DV)
            q_s = qk_s[:c, lanes]
            k_s = qk_s[c:, lanes]
            rhs = bcol * (v_ref[rows, lanes] - eg * k_s)
            v_new = _mm_bf16(t_inv, rhs)
            o = eg * q_s + _mm_bf16(p_mat, v_new)
            g_last = gcol[c - 1:c] if d == 0 else gcol[0:1]
            k_dec = kc.astype(F32) * jnp.exp(g_last - gcol)
            s_new = s_ref[2 * d + hh] * jnp.exp(g_last) + _mm_bf16(k_dec.T, v_new)
            s_ref[2 * d + hh] = s_new
            oacc_ref[rows, lanes] += o

    def body(n, carry):
        chunk_step(n, 0)
        chunk_step(n_chunks - 1 - n, 1)
        return carry

    lax.fori_loop(0, n_chunks, body, 0)

    for hh in range(2):
        lanes = slice(hh * DN_DV, (hh + 1) * DN_DV)
        y = _rms(oacc_ref[:, lanes], nw_ref[...])
        og_ref[:, lanes] = (y * _silu(z_ref[:, lanes])).astype(og_ref.dtype)
    if has_state_out:
        sf_ref[0] = s_ref[0]
        sf_ref[1] = s_ref[1]
        sb_ref[0] = s_ref[2]
        sb_ref[1] = s_ref[3]


def dn_core(qk, v, p, gates_col, gates_row, norm_w, layer, *, seq_len, n_seq, row0,
            init_states=None, og_prev=None):
    m = qk.shape[0]
    n_chunks = seq_len // DN_CHUNK
    rb0 = row0 // seq_len
    pair = 2 * DN_DV
    z_cb0 = DN_QKV_DIM // pair
    has_init = init_states is not None
    in_specs = [
        pl.BlockSpec((seq_len, DN_DK), lambda s, h: (rb0 + s, h)),
        pl.BlockSpec((seq_len, DN_DK), lambda s, h: (rb0 + s, DN_HEADS_K + h)),
        pl.BlockSpec((seq_len, pair), lambda s, h: (rb0 + s, h)),
        pl.BlockSpec((seq_len, pair), lambda s, h: (rb0 + s, z_cb0 + h)),
        pl.BlockSpec((None, seq_len, 8), lambda s, h: (h, rb0 + s, 0)),
        pl.BlockSpec((None, n_chunks, 8, DN_CHUNK), lambda s, h: (h, rb0 + s, 0, 0)),
        pl.BlockSpec((None, 1, DN_DV), lambda s, h: (layer, 0, 0)),
    ]
    args = [qk, qk, v, p, gates_col, gates_row, norm_w]
    og_shape = jax.ShapeDtypeStruct((m, DN_V_DIM), BF16)
    og_spec = pl.BlockSpec((seq_len, pair), lambda s, h: (rb0 + s, h))
    aliases = {}
    if has_init:
        st_spec = pl.BlockSpec((None, None, 2, DN_DK, DN_DV), lambda s, h: (s, layer, h, 0, 0))
        in_specs += [st_spec, st_spec, pl.BlockSpec(memory_space=pl.ANY)]
        args += [init_states[0], init_states[1], og_prev]
        aliases = {len(args) - 1: 0}
        out_specs = og_spec
        out_shape = og_shape
    else:
        so_spec = pl.BlockSpec((None, None, 2, DN_DK, DN_DV), lambda s, h: (s, 0, h, 0, 0))
        so_shape = jax.ShapeDtypeStruct((n_seq, 1, DN_HEADS_V, DN_DK, DN_DV), F32)
        out_specs = [og_spec, so_spec, so_spec]
        out_shape = [og_shape, so_shape, so_shape]
    return pl.pallas_call(
        functools.partial(_dn_core_kernel, n_chunks=n_chunks, has_init=has_init, has_state_out=not has_init),
        grid=(n_seq, DN_HEADS_K),
        in_specs=in_specs,
        out_specs=out_specs,
        out_shape=out_shape,
        scratch_shapes=[pltpu.VMEM((seq_len, pair), F32), pltpu.VMEM((4, DN_DK, DN_DV), F32)],
        input_output_aliases=aliases,
        compiler_params=_params("arbitrary", "arbitrary"),
        name="dn_core_lat" if has_init else "dn_core_ctx",
    )(*args)


def _cm_gate_kernel(u_ref, v_ref, lw_ref, lb_ref, ws_ref, bs_ref, o_ref):
    v = v_ref[...]
    mu = jnp.mean(v, axis=-1, keepdims=True)
    vc = v - mu
    vn = vc * lax.rsqrt(jnp.mean(vc * vc, axis=-1, keepdims=True) + EPS) * lw_ref[...] + lb_ref[...]
    vn = vn.astype(BF16)
    for g in range(CM_GROUPS):
        lanes = slice(g * CM_GDIM, (g + 1) * CM_GDIM)
        sp = jnp.dot(ws_ref[g].astype(BF16), vn[:, lanes], preferred_element_type=F32) + bs_ref[g]
        o_ref[:, lanes] = (u_ref[:, lanes] * sp).astype(o_ref.dtype)


def cm_gate(zz, ln_w, ln_b, w_s, b_s_col, layer):
    m = zz.shape[0]
    row = pl.BlockSpec((None, 1, CM_DIM), lambda i: (layer, 0, 0))
    return pl.pallas_call(
        _cm_gate_kernel,
        grid=(m // CM_CHUNK,),
        in_specs=[
            pl.BlockSpec((CM_CHUNK, CM_DIM), lambda i: (i, 0)),
            pl.BlockSpec((CM_CHUNK, CM_DIM), lambda i: (i, 1)),
            row,
            row,
            pl.BlockSpec((None, CM_GROUPS, CM_CHUNK, CM_CHUNK), lambda i: (layer, 0, 0, 0)),
            pl.BlockSpec((None, CM_GROUPS, CM_CHUNK, 1), lambda i: (layer, 0, 0, 0)),
        ],
        out_specs=pl.BlockSpec((CM_CHUNK, CM_DIM), lambda i: (i, 0)),
        out_shape=jax.ShapeDtypeStruct((m, CM_DIM), BF16),
        compiler_params=_params("arbitrary"),
        name="cm_gate",
    )(zz, zz, ln_w, ln_b, w_s, b_s_col)


def _ffn_kernel(x_ref, nw_ref, sh_ref, sc_ref, g_ref, w1_ref, w2_ref, fw_ref, o_ref, h_ref, *, final_norm):
    kk = pl.program_id(1)

    @pl.when(kk == 0)
    def _():
        y = _rms(x_ref[...], nw_ref[...])
        h_ref[...] = (y * (1.0 + sc_ref[...]) + sh_ref[...]).astype(BF16)
        o_ref[...] = jnp.zeros_like(o_ref)

    a = jnp.dot(h_ref[...], w1_ref[...].astype(BF16), preferred_element_type=F32)
    a = jnp.square(jnp.maximum(a, 0.0)).astype(BF16)
    o_ref[...] += jnp.dot(a, w2_ref[...].astype(BF16), preferred_element_type=F32)

    @pl.when(kk == pl.num_programs(1) - 1)
    def _():
        r = x_ref[...] + g_ref[...] * o_ref[...]
        if final_norm:
            r = _rms(r, fw_ref[...])
        o_ref[...] = r


def ffn(x, norm_w, layer, shift, scale, gate, w1, w2, final_w, *, final_norm, tm=512, tk=512):
    m, d = x.shape
    ff = w1.shape[2]
    mod_spec = pl.BlockSpec((None, 1, d), lambda i, k: (_group_of_row(i * tm), 0, 0))
    return pl.pallas_call(
        functools.partial(_ffn_kernel, final_norm=final_norm),
        grid=(m // tm, ff // tk),
        in_specs=[
            pl.BlockSpec((tm, d), lambda i, k: (i, 0)),
            pl.BlockSpec((None, 1, d), lambda i, k: (layer, 0, 0)),
            mod_spec,
            mod_spec,
            mod_spec,
            pl.BlockSpec((None, d, tk), lambda i, k: (layer, 0, k)),
            pl.BlockSpec((None, tk, d), lambda i, k: (layer, k, 0)),
            pl.BlockSpec((1, d), lambda i, k: (0, 0)),
        ],
        out_specs=pl.BlockSpec((tm, d), lambda i, k: (i, 0)),
        out_shape=jax.ShapeDtypeStruct((m, d), F32),
        scratch_shapes=[pltpu.VMEM((tm, d), BF16)],
        compiler_params=_params("arbitrary", "arbitrary"),
        name="ffn",
    )(x, norm_w.reshape(norm_w.shape[0], 1, d), shift, scale, gate, w1, w2, final_w.reshape(1, d))


def _grid_pos_embed(n_tokens):
    rows = n_tokens // GRID_W
    r = jnp.repeat(jnp.arange(rows), GRID_W).astype(F32)
    col = jnp.tile(jnp.arange(GRID_W), rows).astype(F32)
    quarter = D_MODEL // 4
    freq = 1.0 / (10000.0 ** (jnp.arange(quarter, dtype=F32) / quarter))
    ar = r[:, None] * freq[None, :]
    ac = col[:, None] * freq[None, :]
    return jnp.concatenate([jnp.sin(ar), jnp.cos(ar), jnp.sin(ac), jnp.cos(ac)], axis=-1)


def _deltanet_layer(x, h, j, gate, state_f, state_b, dn_w_in, dn_conv_w, dn_a_log, dn_dt_bias, dn_norm_w, dn_w_out):
    n_main = DN_QKV_DIM + DN_V_DIM
    p = matmul_ws(h, dn_w_in, j, n_main, tm=1024, tn=1024, epilogue=_ep_plain, name="dn_in")
    ab = matmul_ws(h, dn_w_in, j, DN_GATE_COLS, tm=1024, tn=DN_GATE_COLS, epilogue=_ep_plain,
                   col_block0=n_main // DN_GATE_COLS, name="dn_in_gates")
    qk = conv_silu(p, dn_conv_w, j, 0, 2 * DN_K_DIM, normalize=True, out_dtype=BF16)
    v = conv_silu(p, dn_conv_w, j, 2 * DN_K_DIM, DN_V_DIM, normalize=False, out_dtype=F32)

    zeros = jnp.zeros((DN_HEADS_V,), F32)
    a_log_row = jnp.concatenate([dn_a_log[j, 0], zeros, dn_a_log[j, 1], zeros])[None, :]
    dt_row = jnp.concatenate([dn_dt_bias[j, 0], zeros, dn_dt_bias[j, 1], zeros])[None, :]
    gates = dn_gates(ab, a_log_row, dt_row)
    m = gates.shape[0]
    gates_col = gates.reshape(m, 4, DN_HEADS_K, 2).transpose(2, 0, 1, 3).reshape(DN_HEADS_K, m, 8)
    gates_row = gates_col.reshape(DN_HEADS_K, m // DN_CHUNK, DN_CHUNK, 8).transpose(0, 1, 3, 2)
    norm_w = dn_norm_w.reshape(dn_norm_w.shape[0], 1, DN_DV)

    og, s_f, s_b = dn_core(qk, v, p, gates_col, gates_row, norm_w, j, seq_len=SEQ, n_seq=BATCH, row0=0)
    og = dn_core(qk, v, p, gates_col, gates_row, norm_w, j, seq_len=DEC_SEQ, n_seq=DEC_BATCH, row0=N_CTX_TOK,
                 init_states=(state_f, state_b), og_prev=og)
    x = matmul_resid_gate(og, dn_w_out, j, x, gate, tm=1024, tn=512, name="dn_out")
    return x, s_f, s_b


def _chunk_mlp_layer(x, h, j, gate, cm_w_in, cm_b_in, cm_ln_w, cm_ln_b, cm_w_s, cm_b_s, cm_w_out):
    n_in = 2 * CM_DIM
    tn = 1024
    bias = (cm_b_in.reshape(cm_b_in.shape[0], 1, n_in), pl.BlockSpec((None, 1, tn), lambda jj, i: (j, 0, jj)))
    zz = matmul_ws(h, cm_w_in, j, n_in, tm=1024, tn=tn, epilogue=_ep_bias_gelu, extras=(bias,), name="cm_in")
    n_b = cm_ln_w.shape[0]
    uv = cm_gate(zz, cm_ln_w.reshape(n_b, 1, CM_DIM), cm_ln_b.reshape(n_b, 1, CM_DIM), cm_w_s,
                 cm_b_s[..., None], j)
    return matmul_resid_gate(uv, cm_w_out, j, x, gate, tm=1024, tn=512, name="cm_out")


def kernel(x_prompt, x_sample, state_dn_fwd, state_dn_bwd, c, c_ctx, norm_mix_w, norm_mlp_w, w_ada, b_ada, dn_w_in, dn_conv_w, dn_A_log, dn_dt_bias, dn_norm_w, dn_w_out, cm_w_in, cm_b_in, cm_ln_w, cm_ln_b, cm_w_s, cm_b_s, cm_w_out, w_ff1, w_ff2, final_norm_w):
    lat = x_sample + _grid_pos_embed(DEC_SEQ)[None]
    x = jnp.concatenate([x_prompt.reshape(N_CTX_TOK, D_MODEL), lat.reshape(N_LAT_TOK, D_MODEL)], axis=0)
    cond = jnp.concatenate([c_ctx[None, :], c, jnp.zeros((N_COND - 1 - DEC_BATCH, D_MODEL), F32)], axis=0)
    mod = adaln_all(cond, w_ada, b_ada)
    mod = mod.reshape(DEPTH, N_COND, N_MOD, D_MODEL).transpose(0, 2, 1, 3)[:, :, :, None, :]

    new_fwd, new_bwd = [], []
    for i in range(DEPTH):
        j = i // N_MIXERS
        shift, scale, gate, shift2, scale2, gate2 = [mod[i, t] for t in range(N_MOD)]
        h = norm_modulate(x, norm_mix_w, i, shift, scale)
        if i % N_MIXERS == 0:
            x, s_f, s_b = _deltanet_layer(x, h, j, gate, state_dn_fwd, state_dn_bwd, dn_w_in, dn_conv_w,
                                          dn_A_log, dn_dt_bias, dn_norm_w, dn_w_out)
            new_fwd.append(s_f)
            new_bwd.append(s_b)
        else:
            x = _chunk_mlp_layer(x, h, j, gate, cm_w_in, cm_b_in, cm_ln_w, cm_ln_b, cm_w_s, cm_b_s, cm_w_out)
        x = ffn(x, norm_mlp_w, i, shift2, scale2, gate2, w_ff1, w_ff2, final_norm_w, final_norm=(i == DEPTH - 1))

    y_prompt = x[:N_CTX_TOK].reshape(BATCH, SEQ, D_MODEL)
    y_sample = x[N_CTX_TOK:].reshape(DEC_BATCH, DEC_SEQ, D_MODEL)
    return (y_prompt, y_sample, jnp.concatenate(new_fwd, axis=1), jnp.concatenate(new_bwd, axis=1))
```

```python
import functools
import math

import jax
import jax.numpy as jnp
from jax import lax
from jax.experimental import pallas as pl
from jax.experimental.pallas import tpu as pltpu

F32 = jnp.float32
BF16 = jnp.bfloat16

D_MODEL = 2048
BATCH = 16
SEQ = 256
DEPTH = 2
DEC_BATCH = 2
DEC_SEQ = 1024
GRID_W = 64
N_MIXERS = 2
DN_DK = 128
DN_DV = 128
DN_HEADS_K = D_MODEL // DN_DK
DN_HEADS_V = 2 * DN_HEADS_K
DN_K_DIM = DN_HEADS_K * DN_DK
DN_V_DIM = DN_HEADS_V * DN_DV
DN_QKV_DIM = 2 * DN_K_DIM + DN_V_DIM
DN_GATE_COLS = 4 * DN_HEADS_V
DN_CONV = 5
DN_CHUNK = 64
DN_CHUNK_GROUP = 4
CM_DIM = 2 * D_MODEL
CM_CHUNK = 128
CM_GROUPS = 16
CM_GDIM = CM_DIM // CM_GROUPS
FF_DIM = 4 * D_MODEL
N_MOD = 6
EPS = 1e-6

N_CTX_TOK = BATCH * SEQ
N_LAT_TOK = DEC_BATCH * DEC_SEQ
N_TOK = N_CTX_TOK + N_LAT_TOK
N_COND = 8
CONV_HALO = 8
VMEM_LIMIT_BYTES = 56 * 1024 * 1024


def _group_of_row(row0):
    return jnp.where(row0 < N_CTX_TOK, 0, 1 + (row0 - N_CTX_TOK) // DEC_SEQ)


def _params(*sem):
    return pltpu.CompilerParams(dimension_semantics=sem, vmem_limit_bytes=VMEM_LIMIT_BYTES)


def _rms(x, w):
    return x * lax.rsqrt(jnp.mean(x * x, axis=-1, keepdims=True) + EPS) * w


def _silu(x):
    return x * jax.nn.sigmoid(x)


def _gelu_tanh(x):
    return 0.5 * x * (1.0 + jnp.tanh(math.sqrt(2.0 / math.pi) * (x + 0.044715 * (x * x * x))))


def _adaln_kernel(c_ref, w_ref, b_ref, o_ref):
    x = _silu(c_ref[...]).astype(BF16)
    acc = jnp.dot(x, w_ref[...].astype(BF16), preferred_element_type=F32)
    o_ref[...] = acc + b_ref[...]


def adaln_all(cond, w_ada, b_ada, tn=1024):
    depth, d, n = w_ada.shape
    return pl.pallas_call(
        _adaln_kernel,
        grid=(depth, n // tn),
        in_specs=[
            pl.BlockSpec((N_COND, d), lambda l, j: (0, 0)),
            pl.BlockSpec((None, d, tn), lambda l, j: (l, 0, j)),
            pl.BlockSpec((None, 1, tn), lambda l, j: (l, 0, j)),
        ],
        out_specs=pl.BlockSpec((None, N_COND, tn), lambda l, j: (l, 0, j)),
        out_shape=jax.ShapeDtypeStruct((depth, N_COND, n), F32),
        compiler_params=_params("arbitrary", "arbitrary"),
        name="adaln",
    )(cond, w_ada, b_ada.reshape(depth, 1, n))


def _normmod_kernel(x_ref, w_ref, sh_ref, sc_ref, o_ref):
    y = _rms(x_ref[...], w_ref[...])
    o_ref[...] = (y * (1.0 + sc_ref[...]) + sh_ref[...]).astype(BF16)


def norm_modulate(x, norm_w, layer, shift, scale, tm=512):
    m, d = x.shape
    mod_spec = pl.BlockSpec((None, 1, d), lambda i: (_group_of_row(i * tm), 0, 0))
    return pl.pallas_call(
        _normmod_kernel,
        grid=(m // tm,),
        in_specs=[
            pl.BlockSpec((tm, d), lambda i: (i, 0)),
            pl.BlockSpec((None, 1, d), lambda i: (layer, 0, 0)),
            mod_spec,
            mod_spec,
        ],
        out_specs=pl.BlockSpec((tm, d), lambda i: (i, 0)),
        out_shape=jax.ShapeDtypeStruct((m, d), BF16),
        compiler_params=_params("arbitrary"),
        name="norm_modulate",
    )(x, norm_w.reshape(norm_w.shape[0], 1, d), shift, scale)


def _mm_kernel(x_ref, w_ref, *refs, n_extra, epilogue):
    extra = refs[:n_extra]
    o_ref = refs[n_extra]
    wb_ref = refs[n_extra + 1]

    @pl.when(pl.program_id(1) == 0)
    def _():
        wb_ref[...] = w_ref[...].astype(BF16)

    acc = jnp.dot(x_ref[...], wb_ref[...], preferred_element_type=F32)
    o_ref[...] = epilogue(acc, *[r[...] for r in extra]).astype(o_ref.dtype)


def matmul_ws(x, w, layer, n_out, *, tm, tn, epilogue, extras=(), out_dtype=F32, col_block0=0, name="matmul"):
    m, k = x.shape
    in_specs = [
        pl.BlockSpec((tm, k), lambda j, i: (i, 0)),
        pl.BlockSpec((None, k, tn), lambda j, i: (layer, 0, j + col_block0)),
    ] + [spec for _, spec in extras]
    return pl.pallas_call(
        functools.partial(_mm_kernel, n_extra=len(extras), epilogue=epilogue),
        grid=(n_out // tn, m // tm),
        in_specs=in_specs,
        out_specs=pl.BlockSpec((tm, tn), lambda j, i: (i, j)),
        out_shape=jax.ShapeDtypeStruct((m, n_out), out_dtype),
        scratch_shapes=[pltpu.VMEM((k, tn), BF16)],
        compiler_params=_params("arbitrary", "arbitrary"),
        name=name,
    )(x, w, *[a for a, _ in extras])


def _ep_plain(acc):
    return acc


def _ep_bias_gelu(acc, b):
    return _gelu_tanh(acc + b)


def _ep_resid_gate(acc, resid, gate):
    return resid + gate * acc


def matmul_resid_gate(x, w, layer, resid, gate, *, tm, tn, name):
    n_out = resid.shape[1]
    extras = (
        (resid, pl.BlockSpec((tm, tn), lambda j, i: (i, j))),
        (gate, pl.BlockSpec((None, 1, tn), lambda j, i: (_group_of_row(i * tm), 0, j))),
    )
    return matmul_ws(x, w, layer, n_out, tm=tm, tn=tn, epilogue=_ep_resid_gate, extras=extras, name=name)


def _conv_kernel(main_ref, prev_ref, next_ref, w_ref, o_ref, *, rb, normalize, n_q_tiles):
    i = pl.program_id(0)
    j = pl.program_id(1)
    row0 = i * rb
    in_lat = row0 >= N_CTX_TOK
    first = jnp.logical_or(jnp.logical_not(in_lat), (row0 - N_CTX_TOK) % DEC_SEQ == 0)
    last = jnp.logical_or(jnp.logical_not(in_lat), (row0 + rb - N_CTX_TOK) % DEC_SEQ == 0)
    prev = jnp.where(first, 0.0, prev_ref[...])
    nxt = jnp.where(last, 0.0, next_ref[...])
    ext = jnp.concatenate([prev, main_ref[...], nxt], axis=0)
    n_ext = rb + 2 * CONV_HALO
    pad = DN_CONV // 2
    acc = None
    for t in range(DN_CONV):
        shifted = ext if t == pad else pltpu.roll(ext, (pad - t) % n_ext, 0)
        term = shifted[CONV_HALO:CONV_HALO + rb] * w_ref[t:t + 1, :]
        acc = term if acc is None else acc + term
    y = _silu(acc)
    if normalize:
        scale = jnp.where(j < n_q_tiles, DN_DK ** -0.5, 1.0)
        tc = y.shape[1]
        for hh in range(tc // DN_DK):
            sl = y[:, hh * DN_DK:(hh + 1) * DN_DK]
            ss = jnp.sum(sl * sl, axis=-1, keepdims=True)
            o_ref[:, hh * DN_DK:(hh + 1) * DN_DK] = (sl * (lax.rsqrt(ss + EPS) * scale)).astype(o_ref.dtype)
    else:
        o_ref[...] = y.astype(o_ref.dtype)


def conv_silu(p, conv_w, layer, col0, n_cols, *, normalize, out_dtype, rb=SEQ, tc=512):
    assert rb == SEQ and DEC_SEQ % rb == 0 and col0 % tc == 0 and n_cols % tc == 0
    m = p.shape[0]
    cb0 = col0 // tc
    hb = rb // CONV_HALO
    n_hblk = m // CONV_HALO
    return pl.pallas_call(
        functools.partial(_conv_kernel, rb=rb, normalize=normalize, n_q_tiles=DN_K_DIM // tc),
        grid=(m // rb, n_cols // tc),
        in_specs=[
            pl.BlockSpec((rb, tc), lambda i, j: (i, cb0 + j)),
            pl.BlockSpec((CONV_HALO, tc), lambda i, j: (jnp.maximum(i * hb - 1, 0), cb0 + j)),
            pl.BlockSpec((CONV_HALO, tc), lambda i, j: (jnp.minimum((i + 1) * hb, n_hblk - 1), cb0 + j)),
            pl.BlockSpec((None, DN_CONV, tc), lambda i, j: (layer, 0, cb0 + j)),
        ],
        out_specs=pl.BlockSpec((rb, tc), lambda i, j: (i, j)),
        out_shape=jax.ShapeDtypeStruct((m, n_cols), out_dtype),
        compiler_params=_params("arbitrary", "arbitrary"),
        name="dn_conv_norm" if normalize else "dn_conv",
    )(p, p, p, conv_w)


def _split3(x):
    hi = x.astype(BF16)
    r1 = x - hi.astype(F32)
    mid = r1.astype(BF16)
    lo = (r1 - mid.astype(F32)).astype(BF16)
    return hi, mid, lo


def _gates_kernel(ab_ref, alog_ref, dtb_ref, o_ref):
    ab = ab_ref[...]
    log_g = -jnp.exp(alog_ref[...]) * jax.nn.softplus(ab + dtb_ref[...])
    beta = jax.nn.sigmoid(ab)
    c = ab.shape[0]
    ii = lax.broadcasted_iota(jnp.int32, (c, c), 0)
    jj = lax.broadcasted_iota(jnp.int32, (c, c), 1)
    tril = (ii >= jj).astype(BF16)
    triu = (ii <= jj).astype(BF16)
    parts = _split3(log_g)
    cum_f = sum(jnp.dot(tril, part, preferred_element_type=F32) for part in parts)
    cum_b = sum(jnp.dot(triu, part, preferred_element_type=F32) for part in parts)
    lane = lax.broadcasted_iota(jnp.int32, ab.shape, 1)
    kind = lane // DN_HEADS_V
    o_ref[...] = jnp.where(kind == 0, cum_f, jnp.where(kind == 2, cum_b, beta))


def dn_gates(ab, a_log_row, dt_bias_row):
    m, n = ab.shape
    return pl.pallas_call(
        _gates_kernel,
        grid=(m // DN_CHUNK,),
        in_specs=[
            pl.BlockSpec((DN_CHUNK, n), lambda i: (i, 0)),
            pl.BlockSpec((1, n), lambda i: (0, 0)),
            pl.BlockSpec((1, n), lambda i: (0, 0)),
        ],
        out_specs=pl.BlockSpec((DN_CHUNK, n), lambda i: (i, 0)),
        out_shape=jax.ShapeDtypeStruct((m, n), F32),
        compiler_params=_params("arbitrary"),
        name="dn_gates",
    )(ab, a_log_row, dt_bias_row)


def _dot(a, b):
    return jnp.dot(a, b, preferred_element_type=F32)


def _inv_unit_triangular_many(a_list):
    n = a_list[0].shape[0]
    ii = lax.broadcasted_iota(jnp.int32, (n, n), 0)
    jj = lax.broadcasted_iota(jnp.int32, (n, n), 1)
    eye = jnp.where(ii == jj, 1.0, 0.0)
    xs = [eye - jnp.where((ii >> 1) == (jj >> 1), a, 0.0) for a in a_list]
    for level in range(1, int(math.log2(DN_CHUNK))):
        joins = jnp.logical_and((ii >> (level + 1)) == (jj >> (level + 1)), (ii >> level) != (jj >> level))
        ns = [jnp.where(joins, a, 0.0).astype(BF16) for a in a_list]
        xbs = [x.astype(BF16) for x in xs]
        ys = [_dot(xb, nn).astype(BF16) for xb, nn in zip(xbs, ns)]
        xs = [x - _dot(y, xb) for x, y, xb in zip(xs, ys, xbs)]
    return xs


def _dn_core_kernel(*refs, n_chunks, group, has_init, has_state_out):
    q_ref, k_ref, v_ref, z_ref, gc_ref, gr_ref, nw_ref = refs[:7]
    pos = 7
    if has_init:
        s0f_ref, s0b_ref, _og_prev = refs[pos:pos + 3]
        pos += 3
    og_ref = refs[pos]
    pos += 1
    if has_state_out:
        sf_ref, sb_ref = refs[pos:pos + 2]
        pos += 2
    wm_ref, cm_ref, egl_ref, p_ref, u_ref, wv_ref, sh_ref, s_ref = refs[pos:pos + 8]

    c = DN_CHUNK
    pc = 2 * c
    dv = DN_DV
    if has_init:
        s_ref[0] = jnp.concatenate([s0f_ref[0], s0f_ref[1]], axis=1)
        s_ref[1] = jnp.concatenate([s0b_ref[0], s0b_ref[1]], axis=1)
    else:
        s_ref[...] = jnp.zeros_like(s_ref)

    ii = lax.broadcasted_iota(jnp.int32, (pc, pc), 0)
    jj = lax.broadcasted_iota(jnp.int32, (pc, pc), 1)
    chunk_shift = int(math.log2(c))
    same_head = (ii >> chunk_shift) == (jj >> chunk_shift)
    top = ii < c
    top_col = lax.broadcasted_iota(jnp.int32, (pc, 1), 0) < c
    nt_dims = (((1,), (1,)), ((), ()))

    def rows_of(chunk):
        start = chunk * c
        return pl.ds(start if isinstance(start, int) else pl.multiple_of(start, c), c)

    def gate_cols(chunk, d):
        colv = gc_ref[chunk]
        return colv[:, 2 * d:2 * d + 1], colv[:, 2 * d + 1:2 * d + 2]

    def phase_a(base):
        chunks = [base + i for i in range(group)]
        chains = [(i, d) for i in range(group) for d in (0, 1)]
        k2s, grams = [], []
        for ch in chunks:
            kc = k_ref[rows_of(ch), :]
            qc = q_ref[rows_of(ch), :]
            k2 = jnp.concatenate([kc, kc], axis=0)
            k2s.append(k2)
            lhs = jnp.concatenate([kc, kc, qc, qc], axis=0)
            grams.append(lax.dot_general(lhs, k2, nt_dims, preferred_element_type=F32))
        a_mats, p_mats, gcols, bcols = [], [], [], []
        for i, d in chains:
            gcol, bcol = gate_cols(chunks[i], d)
            grow = gr_ref[chunks[i]][2 * d:2 * d + 1, :]
            incl = jnp.logical_and(same_head, (ii >= jj) if d == 0 else (ii <= jj))
            strict = jnp.logical_and(same_head, (ii > jj) if d == 0 else (ii < jj))
            decay = jnp.exp(jnp.where(incl, gcol - grow, -1e30))
            a_mats.append(jnp.where(strict, bcol * grams[i][:pc] * decay, 0.0))
            p_mats.append(grams[i][pc:] * decay)
            gcols.append(gcol)
            bcols.append(bcol)
        t_invs = _inv_unit_triangular_many(a_mats)
        rhss, kfs = [], []
        for (i, d), gcol, bcol in zip(chains, gcols, bcols):
            kf = k2s[i].astype(F32)
            rows = rows_of(chunks[i])
            vp = jnp.concatenate([v_ref[rows, :dv], v_ref[rows, dv:]], axis=0)
            rhss.append(jnp.concatenate([(bcol * jnp.exp(gcol)) * kf, bcol * vp], axis=1).astype(BF16))
            kfs.append(kf)
        uws = [_dot(t.astype(BF16), r) for t, r in zip(t_invs, rhss)]
        kdts, x2s, egls = [], [], []
        for (i, d), gcol, kf, uw in zip(chains, gcols, kfs, uws):
            r0 = c - 1 if d == 0 else 0
            gl0 = gcol[r0:r0 + 1]
            gl1 = gcol[c + r0:c + r0 + 1]
            gl = jnp.where(top_col, gl0, gl1)
            kdts.append((kf * jnp.exp(gl - gcol)).T.astype(BF16))
            u, wv = uw[:, :dv], uw[:, dv:]
            x2s.append(jnp.concatenate([jnp.where(top, u, 0.0), jnp.where(top, 0.0, u),
                                        jnp.where(top, wv, 0.0), jnp.where(top, 0.0, wv)], axis=1).astype(BF16))
            egls.append(jnp.concatenate([jnp.broadcast_to(jnp.exp(gl0), (1, dv)),
                                         jnp.broadcast_to(jnp.exp(gl1), (1, dv))], axis=1))
        wcs = [_dot(kdt, x2) for kdt, x2 in zip(kdts, x2s)]
        for (i, d), p_mat, uw, wc, egl in zip(chains, p_mats, uws, wcs, egls):
            e = 2 * chunks[i] + d
            wm_ref[e] = wc[:, :2 * dv].astype(BF16)
            cm_ref[e] = wc[:, 2 * dv:]
            egl_ref[e] = egl
            p_ref[e] = p_mat.astype(BF16)
            u_ref[e] = uw[:, :dv].astype(BF16)
            wv_ref[e] = uw[:, dv:]

    def phase_b(n, carry):
        es = [2 * n, 2 * (n_chunks - 1 - n) + 1]
        ss = [s_ref[d] for d in (0, 1)]
        sbs = [s.astype(BF16) for s in ss]
        for d in (0, 1):
            sh_ref[es[d]] = sbs[d]
        wms = [wm_ref[e] for e in es]
        wss = [jnp.concatenate([_dot(wm[:, :dv], sb[:, :dv]), _dot(wm[:, dv:], sb[:, dv:])], axis=1)
               for wm, sb in zip(wms, sbs)]
        for d in (0, 1):
            s_ref[d] = egl_ref[es[d]] * ss[d] - wss[d] + cm_ref[es[d]]
        return carry

    def phase_c(base):
        chunks = [base + i for i in range(group)]
        chains = [(i, d) for i in range(group) for d in (0, 1)]
        rs = []
        for i, d in chains:
            e = 2 * chunks[i] + d
            lhs = jnp.concatenate([q_ref[rows_of(chunks[i]), :], u_ref[e]], axis=0)
            rs.append(_dot(lhs, sh_ref[e]))
        v_news, q_ss = [], []
        for (i, d), r in zip(chains, rs):
            e = 2 * chunks[i] + d
            q_ss.append(jnp.concatenate([r[:c, :dv], r[:c, dv:]], axis=0))
            u_s = jnp.concatenate([r[c:2 * c, :dv], r[2 * c:, dv:]], axis=0)
            v_news.append((wv_ref[e] - u_s).astype(BF16))
        pvs = [_dot(p_ref[2 * chunks[i] + d], v_new) for (i, d), v_new in zip(chains, v_news)]
        outs = []
        for (i, d), q_s, pv in zip(chains, q_ss, pvs):
            gcol, _ = gate_cols(chunks[i], d)
            outs.append(jnp.exp(gcol) * q_s + pv)
        for i in range(group):
            o = outs[2 * i] + outs[2 * i + 1]
            rows = rows_of(chunks[i])
            for hh in range(2):
                lanes = slice(hh * dv, (hh + 1) * dv)
                y = _rms(o[hh * c:(hh + 1) * c], nw_ref[...])
                og_ref[rows, lanes] = (y * _silu(z_ref[rows, lanes])).astype(og_ref.dtype)

    def over_groups(phase):
        n_groups = n_chunks // group
        if n_groups == 1:
            phase(0)
        else:
            def body(g, carry):
                phase(g * group)
                return carry
            lax.fori_loop(0, n_groups, body, 0)

    over_groups(phase_a)
    lax.fori_loop(0, n_chunks, phase_b, 0)
    over_groups(phase_c)

    if has_state_out:
        sf_ref[0] = s_ref[0][:, :dv]
        sf_ref[1] = s_ref[0][:, dv:]
        sb_ref[0] = s_ref[1][:, :dv]
        sb_ref[1] = s_ref[1][:, dv:]


def dn_core(qk, v, p, gates_col, gates_row, norm_w, layer, *, seq_len, n_seq, row0,
            init_states=None, og_prev=None):
    m = qk.shape[0]
    n_chunks = seq_len // DN_CHUNK
    rb0 = row0 // seq_len
    pair = 2 * DN_DV
    pc = 2 * DN_CHUNK
    n_e = 2 * n_chunks
    assert n_chunks % DN_CHUNK_GROUP == 0
    z_cb0 = DN_QKV_DIM // pair
    has_init = init_states is not None
    in_specs = [
        pl.BlockSpec((seq_len, DN_DK), lambda s, h: (rb0 + s, h)),
        pl.BlockSpec((seq_len, DN_DK), lambda s, h: (rb0 + s, DN_HEADS_K + h)),
        pl.BlockSpec((seq_len, pair), lambda s, h: (rb0 + s, h)),
        pl.BlockSpec((seq_len, pair), lambda s, h: (rb0 + s, z_cb0 + h)),
        pl.BlockSpec((None, n_chunks, 2 * DN_CHUNK, 4), lambda s, h: (h, rb0 + s, 0, 0)),
        pl.BlockSpec((None, n_chunks, 4, 2 * DN_CHUNK), lambda s, h: (h, rb0 + s, 0, 0)),
        pl.BlockSpec((None, 1, DN_DV), lambda s, h: (layer, 0, 0)),
    ]
    args = [qk, qk, v, p, gates_col, gates_row, norm_w]
    og_shape = jax.ShapeDtypeStruct((m, DN_V_DIM), BF16)
    og_spec = pl.BlockSpec((seq_len, pair), lambda s, h: (rb0 + s, h))
    aliases = {}
    if has_init:
        st_spec = pl.BlockSpec((None, None, 2, DN_DK, DN_DV), lambda s, h: (s, layer, h, 0, 0))
        in_specs += [st_spec, st_spec, pl.BlockSpec(memory_space=pl.ANY)]
        args += [init_states[0], init_states[1], og_prev]
        aliases = {len(args) - 1: 0}
        out_specs = og_spec
        out_shape = og_shape
    else:
        so_spec = pl.BlockSpec((None, None, 2, DN_DK, DN_DV), lambda s, h: (s, 0, h, 0, 0))
        so_shape = jax.ShapeDtypeStruct((n_seq, 1, DN_HEADS_V, DN_DK, DN_DV), F32)
        out_specs = [og_spec, so_spec, so_spec]
        out_shape = [og_shape, so_shape, so_shape]
    return pl.pallas_call(
        functools.partial(_dn_core_kernel, n_chunks=n_chunks, group=DN_CHUNK_GROUP, has_init=has_init,
                          has_state_out=not has_init),
        grid=(n_seq, DN_HEADS_K),
        in_specs=in_specs,
        out_specs=out_specs,
        out_shape=out_shape,
        scratch_shapes=[
            pltpu.VMEM((n_e, DN_DK, pair), BF16),
            pltpu.VMEM((n_e, DN_DK, pair), F32),
            pltpu.VMEM((n_e, 1, pair), F32),
            pltpu.VMEM((n_e, pc, pc), BF16),
            pltpu.VMEM((n_e, pc, DN_DK), BF16),
            pltpu.VMEM((n_e, pc, DN_DV), F32),
            pltpu.VMEM((n_e, DN_DK, pair), BF16),
            pltpu.VMEM((2, DN_DK, pair), F32),
        ],
        input_output_aliases=aliases,
        compiler_params=_params("arbitrary", "arbitrary"),
        name="dn_core_lat" if has_init else "dn_core_ctx",
    )(*args)


def _cm_gate_kernel(u_ref, v_ref, lw_ref, lb_ref, ws_ref, bs_ref, o_ref):
    v = v_ref[...]
    mu = jnp.mean(v, axis=-1, keepdims=True)
    vc = v - mu
    vn = vc * lax.rsqrt(jnp.mean(vc * vc, axis=-1, keepdims=True) + EPS) * lw_ref[...] + lb_ref[...]
    vn = vn.astype(BF16)
    for g in range(CM_GROUPS):
        lanes = slice(g * CM_GDIM, (g + 1) * CM_GDIM)
        sp = jnp.dot(ws_ref[g].astype(BF16), vn[:, lanes], preferred_element_type=F32) + bs_ref[g]
        o_ref[:, lanes] = (u_ref[:, lanes] * sp).astype(o_ref.dtype)


def cm_gate(zz, ln_w, ln_b, w_s, b_s_col, layer):
    m = zz.shape[0]
    row = pl.BlockSpec((None, 1, CM_DIM), lambda i: (layer, 0, 0))
    return pl.pallas_call(
        _cm_gate_kernel,
        grid=(m // CM_CHUNK,),
        in_specs=[
            pl.BlockSpec((CM_CHUNK, CM_DIM), lambda i: (i, 0)),
            pl.BlockSpec((CM_CHUNK, CM_DIM), lambda i: (i, 1)),
            row,
            row,
            pl.BlockSpec((None, CM_GROUPS, CM_CHUNK, CM_CHUNK), lambda i: (layer, 0, 0, 0)),
            pl.BlockSpec((None, CM_GROUPS, CM_CHUNK, 1), lambda i: (layer, 0, 0, 0)),
        ],
        out_specs=pl.BlockSpec((CM_CHUNK, CM_DIM), lambda i: (i, 0)),
        out_shape=jax.ShapeDtypeStruct((m, CM_DIM), BF16),
        compiler_params=_params("arbitrary"),
        name="cm_gate",
    )(zz, zz, ln_w, ln_b, w_s, b_s_col)


def _ffn_kernel(x_ref, nw_ref, sh_ref, sc_ref, g_ref, w1_ref, w2_ref, fw_ref, o_ref, h_ref, *, final_norm):
    kk = pl.program_id(1)

    @pl.when(kk == 0)
    def _():
        y = _rms(x_ref[...], nw_ref[...])
        h_ref[...] = (y * (1.0 + sc_ref[...]) + sh_ref[...]).astype(BF16)
        o_ref[...] = jnp.zeros_like(o_ref)

    a = jnp.dot(h_ref[...], w1_ref[...].astype(BF16), preferred_element_type=F32)
    a = jnp.square(jnp.maximum(a, 0.0)).astype(BF16)
    o_ref[...] += jnp.dot(a, w2_ref[...].astype(BF16), preferred_element_type=F32)

    @pl.when(kk == pl.num_programs(1) - 1)
    def _():
        r = x_ref[...] + g_ref[...] * o_ref[...]
        if final_norm:
            r = _rms(r, fw_ref[...])
        o_ref[...] = r


def ffn(x, norm_w, layer, shift, scale, gate, w1, w2, final_w, *, final_norm, tm=512, tk=512):
    m, d = x.shape
    ff = w1.shape[2]
    mod_spec = pl.BlockSpec((None, 1, d), lambda i, k: (_group_of_row(i * tm), 0, 0))
    return pl.pallas_call(
        functools.partial(_ffn_kernel, final_norm=final_norm),
        grid=(m // tm, ff // tk),
        in_specs=[
            pl.BlockSpec((tm, d), lambda i, k: (i, 0)),
            pl.BlockSpec((None, 1, d), lambda i, k: (layer, 0, 0)),
            mod_spec,
            mod_spec,
            mod_spec,
            pl.BlockSpec((None, d, tk), lambda i, k: (layer, 0, k)),
            pl.BlockSpec((None, tk, d), lambda i, k: (layer, k, 0)),
            pl.BlockSpec((1, d), lambda i, k: (0, 0)),
        ],
        out_specs=pl.BlockSpec((tm, d), lambda i, k: (i, 0)),
        out_shape=jax.ShapeDtypeStruct((m, d), F32),
        scratch_shapes=[pltpu.VMEM((tm, d), BF16)],
        compiler_params=_params("arbitrary", "arbitrary"),
        name="ffn",
    )(x, norm_w.reshape(norm_w.shape[0], 1, d), shift, scale, gate, w1, w2, final_w.reshape(1, d))


def _grid_pos_embed(n_tokens):
    rows = n_tokens // GRID_W
    r = jnp.repeat(jnp.arange(rows), GRID_W).astype(F32)
    col = jnp.tile(jnp.arange(GRID_W), rows).astype(F32)
    quarter = D_MODEL // 4
    freq = 1.0 / (10000.0 ** (jnp.arange(quarter, dtype=F32) / quarter))
    ar = r[:, None] * freq[None, :]
    ac = col[:, None] * freq[None, :]
    return jnp.concatenate([jnp.sin(ar), jnp.cos(ar), jnp.sin(ac), jnp.cos(ac)], axis=-1)


def _deltanet_layer(x, h, j, gate, state_f, state_b, dn_w_in, dn_conv_w, dn_a_log, dn_dt_bias, dn_norm_w, dn_w_out):
    n_main = DN_QKV_DIM + DN_V_DIM
    p = matmul_ws(h, dn_w_in, j, n_main, tm=1024, tn=1024, epilogue=_ep_plain, name="dn_in")
    ab = matmul_ws(h, dn_w_in, j, DN_GATE_COLS, tm=1024, tn=DN_GATE_COLS, epilogue=_ep_plain,
                   col_block0=n_main // DN_GATE_COLS, name="dn_in_gates")
    qk = conv_silu(p, dn_conv_w, j, 0, 2 * DN_K_DIM, normalize=True, out_dtype=BF16)
    v = conv_silu(p, dn_conv_w, j, 2 * DN_K_DIM, DN_V_DIM, normalize=False, out_dtype=F32)

    zeros = jnp.zeros((DN_HEADS_V,), F32)
    a_log_row = jnp.concatenate([dn_a_log[j, 0], zeros, dn_a_log[j, 1], zeros])[None, :]
    dt_row = jnp.concatenate([dn_dt_bias[j, 0], zeros, dn_dt_bias[j, 1], zeros])[None, :]
    gates = dn_gates(ab, a_log_row, dt_row)
    m = gates.shape[0]
    g5 = gates.reshape(m // DN_CHUNK, DN_CHUNK, 4, DN_HEADS_K, 2)
    gates_col = g5.transpose(3, 0, 4, 1, 2).reshape(DN_HEADS_K, m // DN_CHUNK, 2 * DN_CHUNK, 4)
    gates_row = g5.transpose(3, 0, 2, 4, 1).reshape(DN_HEADS_K, m // DN_CHUNK, 4, 2 * DN_CHUNK)
    norm_w = dn_norm_w.reshape(dn_norm_w.shape[0], 1, DN_DV)

    og, s_f, s_b = dn_core(qk, v, p, gates_col, gates_row, norm_w, j, seq_len=SEQ, n_seq=BATCH, row0=0)
    og = dn_core(qk, v, p, gates_col, gates_row, norm_w, j, seq_len=DEC_SEQ, n_seq=DEC_BATCH, row0=N_CTX_TOK,
                 init_states=(state_f, state_b), og_prev=og)
    x = matmul_resid_gate(og, dn_w_out, j, x, gate, tm=1024, tn=512, name="dn_out")
    return x, s_f, s_b


def _chunk_mlp_layer(x, h, j, gate, cm_w_in, cm_b_in, cm_ln_w, cm_ln_b, cm_w_s, cm_b_s, cm_w_out):
    n_in = 2 * CM_DIM
    tn = 1024
    bias = (cm_b_in.reshape(cm_b_in.shape[0], 1, n_in), pl.BlockSpec((None, 1, tn), lambda jj, i: (j, 0, jj)))
    zz = matmul_ws(h, cm_w_in, j, n_in, tm=1024, tn=tn, epilogue=_ep_bias_gelu, extras=(bias,), name="cm_in")
    n_b = cm_ln_w.shape[0]
    uv = cm_gate(zz, cm_ln_w.reshape(n_b, 1, CM_DIM), cm_ln_b.reshape(n_b, 1, CM_DIM), cm_w_s,
                 cm_b_s[..., None], j)
    return matmul_resid_gate(uv, cm_w_out, j, x, gate, tm=1024, tn=512, name="cm_out")


def kernel(x_prompt, x_sample, state_dn_fwd, state_dn_bwd, c, c_ctx, norm_mix_w, norm_mlp_w, w_ada, b_ada, dn_w_in, dn_conv_w, dn_A_log, dn_dt_bias, dn_norm_w, dn_w_out, cm_w_in, cm_b_in, cm_ln_w, cm_ln_b, cm_w_s, cm_b_s, cm_w_out, w_ff1, w_ff2, final_norm_w):
    lat = x_sample + _grid_pos_embed(DEC_SEQ)[None]
    x = jnp.concatenate([x_prompt.reshape(N_CTX_TOK, D_MODEL), lat.reshape(N_LAT_TOK, D_MODEL)], axis=0)
    cond = jnp.concatenate([c_ctx[None, :], c, jnp.zeros((N_COND - 1 - DEC_BATCH, D_MODEL), F32)], axis=0)
    mod = adaln_all(cond, w_ada, b_ada)
    mod = mod.reshape(DEPTH, N_COND, N_MOD, D_MODEL).transpose(0, 2, 1, 3)[:, :, :, None, :]

    new_fwd, new_bwd = [], []
    for i in range(DEPTH):
        j = i // N_MIXERS
        shift, scale, gate, shift2, scale2, gate2 = [mod[i, t] for t in range(N_MOD)]
        h = norm_modulate(x, norm_mix_w, i, shift, scale)
        if i % N_MIXERS == 0:
            x, s_f, s_b = _deltanet_layer(x, h, j, gate, state_dn_fwd, state_dn_bwd, dn_w_in, dn_conv_w,
                                          dn_A_log, dn_dt_bias, dn_norm_w, dn_w_out)
            new_fwd.append(s_f)
            new_bwd.append(s_b)
        else:
            x = _chunk_mlp_layer(x, h, j, gate, cm_w_in, cm_b_in, cm_ln_w, cm_ln_b, cm_w_s, cm_b_s, cm_w_out)
        x = ffn(x, norm_mlp_w, i, shift2, scale2, gate2, w_ff1, w_ff2, final_norm_w, final_norm=(i == DEPTH - 1))

    y_prompt = x[:N_CTX_TOK].reshape(BATCH, SEQ, D_MODEL)
    y_sample = x[N_CTX_TOK:].reshape(DEC_BATCH, DEC_SEQ, D_MODEL)
    return (y_prompt, y_sample, jnp.concatenate(new_fwd, axis=1), jnp.concatenate(new_bwd, axis=1))
```

```python
import functools
import math

import jax
import jax.numpy as jnp
from jax import lax
from jax.experimental import pallas as pl
from jax.experimental.pallas import tpu as pltpu

F32 = jnp.float32
BF16 = jnp.bfloat16

D_MODEL = 2048
BATCH = 16
SEQ = 256
DEPTH = 2
DEC_BATCH = 2
DEC_SEQ = 1024
GRID_W = 64
N_MIXERS = 2
DN_DK = 128
DN_DV = 128
DN_HEADS_K = D_MODEL // DN_DK
DN_HEADS_V = 2 * DN_HEADS_K
DN_K_DIM = DN_HEADS_K * DN_DK
DN_V_DIM = DN_HEADS_V * DN_DV
DN_QKV_DIM = 2 * DN_K_DIM + DN_V_DIM
DN_GATE_COLS = 4 * DN_HEADS_V
DN_CONV = 5
DN_CHUNK = 64
CM_DIM = 2 * D_MODEL
CM_CHUNK = 128
CM_GROUPS = 16
CM_GDIM = CM_DIM // CM_GROUPS
FF_DIM = 4 * D_MODEL
N_MOD = 6
EPS = 1e-6

N_CTX_TOK = BATCH * SEQ
N_LAT_TOK = DEC_BATCH * DEC_SEQ
N_TOK = N_CTX_TOK + N_LAT_TOK
N_COND = 8
CONV_HALO = 8
VMEM_LIMIT_BYTES = 56 * 1024 * 1024


def _group_of_row(row0):
    return jnp.where(row0 < N_CTX_TOK, 0, 1 + (row0 - N_CTX_TOK) // DEC_SEQ)


def _params(*sem):
    return pltpu.CompilerParams(dimension_semantics=sem, vmem_limit_bytes=VMEM_LIMIT_BYTES)


def _rms(x, w):
    return x * lax.rsqrt(jnp.mean(x * x, axis=-1, keepdims=True) + EPS) * w


def _silu(x):
    return x * jax.nn.sigmoid(x)


def _gelu_tanh(x):
    return 0.5 * x * (1.0 + jnp.tanh(math.sqrt(2.0 / math.pi) * (x + 0.044715 * (x * x * x))))


def _adaln_kernel(c_ref, w_ref, b_ref, o_ref):
    x = _silu(c_ref[...]).astype(BF16)
    acc = jnp.dot(x, w_ref[...].astype(BF16), preferred_element_type=F32)
    o_ref[...] = acc + b_ref[...]


def adaln_all(cond, w_ada, b_ada, tn=1024):
    depth, d, n = w_ada.shape
    return pl.pallas_call(
        _adaln_kernel,
        grid=(depth, n // tn),
        in_specs=[
            pl.BlockSpec((N_COND, d), lambda l, j: (0, 0)),
            pl.BlockSpec((None, d, tn), lambda l, j: (l, 0, j)),
            pl.BlockSpec((None, 1, tn), lambda l, j: (l, 0, j)),
        ],
        out_specs=pl.BlockSpec((None, N_COND, tn), lambda l, j: (l, 0, j)),
        out_shape=jax.ShapeDtypeStruct((depth, N_COND, n), F32),
        compiler_params=_params("arbitrary", "arbitrary"),
        name="adaln",
    )(cond, w_ada, b_ada.reshape(depth, 1, n))


def _normmod_kernel(x_ref, w_ref, sh_ref, sc_ref, o_ref):
    y = _rms(x_ref[...], w_ref[...])
    o_ref[...] = (y * (1.0 + sc_ref[...]) + sh_ref[...]).astype(BF16)


def norm_modulate(x, norm_w, layer, shift, scale, tm=512):
    m, d = x.shape
    mod_spec = pl.BlockSpec((None, 1, d), lambda i: (_group_of_row(i * tm), 0, 0))
    return pl.pallas_call(
        _normmod_kernel,
        grid=(m // tm,),
        in_specs=[
            pl.BlockSpec((tm, d), lambda i: (i, 0)),
            pl.BlockSpec((None, 1, d), lambda i: (layer, 0, 0)),
            mod_spec,
            mod_spec,
        ],
        out_specs=pl.BlockSpec((tm, d), lambda i: (i, 0)),
        out_shape=jax.ShapeDtypeStruct((m, d), BF16),
        compiler_params=_params("arbitrary"),
        name="norm_modulate",
    )(x, norm_w.reshape(norm_w.shape[0], 1, d), shift, scale)


def _mm_kernel(x_ref, w_ref, *refs, n_extra, epilogue):
    extra = refs[:n_extra]
    o_ref = refs[n_extra]
    wb_ref = refs[n_extra + 1]

    @pl.when(pl.program_id(1) == 0)
    def _():
        wb_ref[...] = w_ref[...].astype(BF16)

    acc = jnp.dot(x_ref[...], wb_ref[...], preferred_element_type=F32)
    o_ref[...] = epilogue(acc, *[r[...] for r in extra]).astype(o_ref.dtype)


def matmul_ws(x, w, layer, n_out, *, tm, tn, epilogue, extras=(), out_dtype=F32, col_block0=0, name="matmul"):
    m, k = x.shape
    in_specs = [
        pl.BlockSpec((tm, k), lambda j, i: (i, 0)),
        pl.BlockSpec((None, k, tn), lambda j, i: (layer, 0, j + col_block0)),
    ] + [spec for _, spec in extras]
    return pl.pallas_call(
        functools.partial(_mm_kernel, n_extra=len(extras), epilogue=epilogue),
        grid=(n_out // tn, m // tm),
        in_specs=in_specs,
        out_specs=pl.BlockSpec((tm, tn), lambda j, i: (i, j)),
        out_shape=jax.ShapeDtypeStruct((m, n_out), out_dtype),
        scratch_shapes=[pltpu.VMEM((k, tn), BF16)],
        compiler_params=_params("arbitrary", "arbitrary"),
        name=name,
    )(x, w, *[a for a, _ in extras])


def _ep_plain(acc):
    return acc


def _ep_bias_gelu(acc, b):
    return _gelu_tanh(acc + b)


def _ep_resid_gate(acc, resid, gate):
    return resid + gate * acc


def matmul_resid_gate(x, w, layer, resid, gate, *, tm, tn, name):
    n_out = resid.shape[1]
    extras = (
        (resid, pl.BlockSpec((tm, tn), lambda j, i: (i, j))),
        (gate, pl.BlockSpec((None, 1, tn), lambda j, i: (_group_of_row(i * tm), 0, j))),
    )
    return matmul_ws(x, w, layer, n_out, tm=tm, tn=tn, epilogue=_ep_resid_gate, extras=extras, name=name)


def _conv_kernel(main_ref, prev_ref, next_ref, w_ref, o_ref, *, rb, normalize, n_q_tiles):
    i = pl.program_id(0)
    j = pl.program_id(1)
    row0 = i * rb
    in_lat = row0 >= N_CTX_TOK
    first = jnp.logical_or(jnp.logical_not(in_lat), (row0 - N_CTX_TOK) % DEC_SEQ == 0)
    last = jnp.logical_or(jnp.logical_not(in_lat), (row0 + rb - N_CTX_TOK) % DEC_SEQ == 0)
    prev = jnp.where(first, 0.0, prev_ref[...])
    nxt = jnp.where(last, 0.0, next_ref[...])
    ext = jnp.concatenate([prev, main_ref[...], nxt], axis=0)
    n_ext = rb + 2 * CONV_HALO
    pad = DN_CONV // 2
    acc = None
    for t in range(DN_CONV):
        shifted = ext if t == pad else pltpu.roll(ext, (pad - t) % n_ext, 0)
        term = shifted[CONV_HALO:CONV_HALO + rb] * w_ref[t:t + 1, :]
        acc = term if acc is None else acc + term
    y = _silu(acc)
    if normalize:
        scale = jnp.where(j < n_q_tiles, DN_DK ** -0.5, 1.0)
        tc = y.shape[1]
        for hh in range(tc // DN_DK):
            sl = y[:, hh * DN_DK:(hh + 1) * DN_DK]
            ss = jnp.sum(sl * sl, axis=-1, keepdims=True)
            o_ref[:, hh * DN_DK:(hh + 1) * DN_DK] = (sl * (lax.rsqrt(ss + EPS) * scale)).astype(o_ref.dtype)
    else:
        o_ref[...] = y.astype(o_ref.dtype)


def conv_silu(p, conv_w, layer, col0, n_cols, *, normalize, out_dtype, rb=SEQ, tc=512):
    assert rb == SEQ and DEC_SEQ % rb == 0 and col0 % tc == 0 and n_cols % tc == 0
    m = p.shape[0]
    cb0 = col0 // tc
    hb = rb // CONV_HALO
    n_hblk = m // CONV_HALO
    return pl.pallas_call(
        functools.partial(_conv_kernel, rb=rb, normalize=normalize, n_q_tiles=DN_K_DIM // tc),
        grid=(m // rb, n_cols // tc),
        in_specs=[
            pl.BlockSpec((rb, tc), lambda i, j: (i, cb0 + j)),
            pl.BlockSpec((CONV_HALO, tc), lambda i, j: (jnp.maximum(i * hb - 1, 0), cb0 + j)),
            pl.BlockSpec((CONV_HALO, tc), lambda i, j: (jnp.minimum((i + 1) * hb, n_hblk - 1), cb0 + j)),
            pl.BlockSpec((None, DN_CONV, tc), lambda i, j: (layer, 0, cb0 + j)),
        ],
        out_specs=pl.BlockSpec((rb, tc), lambda i, j: (i, j)),
        out_shape=jax.ShapeDtypeStruct((m, n_cols), out_dtype),
        compiler_params=_params("arbitrary", "arbitrary"),
        name="dn_conv_norm" if normalize else "dn_conv",
    )(p, p, p, conv_w)


def _split3(x):
    hi = x.astype(BF16)
    r1 = x - hi.astype(F32)
    mid = r1.astype(BF16)
    lo = (r1 - mid.astype(F32)).astype(BF16)
    return hi, mid, lo


def _gates_kernel(ab_ref, alog_ref, dtb_ref, o_ref):
    ab = ab_ref[...]
    log_g = -jnp.exp(alog_ref[...]) * jax.nn.softplus(ab + dtb_ref[...])
    beta = jax.nn.sigmoid(ab)
    c = ab.shape[0]
    ii = lax.broadcasted_iota(jnp.int32, (c, c), 0)
    jj = lax.broadcasted_iota(jnp.int32, (c, c), 1)
    tril = (ii >= jj).astype(BF16)
    triu = (ii <= jj).astype(BF16)
    parts = _split3(log_g)
    cum_f = sum(jnp.dot(tril, part, preferred_element_type=F32) for part in parts)
    cum_b = sum(jnp.dot(triu, part, preferred_element_type=F32) for part in parts)
    lane = lax.broadcasted_iota(jnp.int32, ab.shape, 1)
    kind = lane // DN_HEADS_V
    o_ref[...] = jnp.where(kind == 0, cum_f, jnp.where(kind == 2, cum_b, beta))


def dn_gates(ab, a_log_row, dt_bias_row):
    m, n = ab.shape
    return pl.pallas_call(
        _gates_kernel,
        grid=(m // DN_CHUNK,),
        in_specs=[
            pl.BlockSpec((DN_CHUNK, n), lambda i: (i, 0)),
            pl.BlockSpec((1, n), lambda i: (0, 0)),
            pl.BlockSpec((1, n), lambda i: (0, 0)),
        ],
        out_specs=pl.BlockSpec((DN_CHUNK, n), lambda i: (i, 0)),
        out_shape=jax.ShapeDtypeStruct((m, n), F32),
        compiler_params=_params("arbitrary"),
        name="dn_gates",
    )(ab, a_log_row, dt_bias_row)


def _dot(a, b):
    return jnp.dot(a, b, preferred_element_type=F32)


def _inv_unit_triangular_many(a_list):
    n = a_list[0].shape[0]
    ii = lax.broadcasted_iota(jnp.int32, (n, n), 0)
    jj = lax.broadcasted_iota(jnp.int32, (n, n), 1)
    eye = jnp.where(ii == jj, 1.0, 0.0)
    xs = [eye - jnp.where((ii >> 1) == (jj >> 1), a, 0.0) for a in a_list]
    for level in range(1, int(math.log2(DN_CHUNK))):
        joins = jnp.logical_and((ii >> (level + 1)) == (jj >> (level + 1)), (ii >> level) != (jj >> level))
        ns = [jnp.where(joins, a, 0.0).astype(BF16) for a in a_list]
        xbs = [x.astype(BF16) for x in xs]
        ys = [_dot(xb, nn).astype(BF16) for xb, nn in zip(xbs, ns)]
        xs = [x - _dot(y, xb) for x, y, xb in zip(xs, ys, xbs)]
    return xs


def _dn_core_kernel(*refs, n_chunks, group, heads, has_init, has_state_out):
    q_ref, k_ref, v_ref, z_ref, gc_ref, gr_ref, nw_ref = refs[:7]
    pos = 7
    if has_init:
        s0f_ref, s0b_ref, _og_prev = refs[pos:pos + 3]
        pos += 3
    og_ref = refs[pos]
    pos += 1
    if has_state_out:
        sf_ref, sb_ref = refs[pos:pos + 2]
        pos += 2
    wm_ref, cm_ref, egl_ref, p_ref, u_ref, wv_ref, sh_ref, s_ref = refs[pos:pos + 8]

    c = DN_CHUNK
    pc = 2 * c
    dv = DN_DV
    if has_init:
        for hl in range(heads):
            s_ref[2 * hl] = jnp.concatenate([s0f_ref[2 * hl], s0f_ref[2 * hl + 1]], axis=1)
            s_ref[2 * hl + 1] = jnp.concatenate([s0b_ref[2 * hl], s0b_ref[2 * hl + 1]], axis=1)
    else:
        s_ref[...] = jnp.zeros_like(s_ref)

    ii = lax.broadcasted_iota(jnp.int32, (pc, pc), 0)
    jj = lax.broadcasted_iota(jnp.int32, (pc, pc), 1)
    chunk_shift = int(math.log2(c))
    same_head = (ii >> chunk_shift) == (jj >> chunk_shift)
    top = ii < c
    top_col = lax.broadcasted_iota(jnp.int32, (pc, 1), 0) < c
    nt_dims = (((1,), (1,)), ((), ()))

    def rows_of(chunk):
        start = chunk * c
        return pl.ds(start if isinstance(start, int) else pl.multiple_of(start, c), c)

    def gate_cols(hl, chunk, d):
        colv = gc_ref[hl, chunk]
        return colv[:, 2 * d:2 * d + 1], colv[:, 2 * d + 1:2 * d + 2]

    def entry(hl, chunk, d):
        return (hl * n_chunks + chunk) * 2 + d

    def qk_lanes(hl):
        return slice(hl * DN_DK, (hl + 1) * DN_DK)

    def v_lanes(hl, hh):
        return slice((2 * hl + hh) * dv, (2 * hl + hh + 1) * dv)

    def phase_a(base):
        units = [(hl, base + i) for hl in range(heads) for i in range(group)]
        chains = [(n, d) for n in range(len(units)) for d in (0, 1)]
        k2s, grams = [], []
        for hl, ch in units:
            kc = k_ref[rows_of(ch), qk_lanes(hl)]
            qc = q_ref[rows_of(ch), qk_lanes(hl)]
            k2 = jnp.concatenate([kc, kc], axis=0)
            k2s.append(k2)
            lhs = jnp.concatenate([kc, kc, qc, qc], axis=0)
            grams.append(lax.dot_general(lhs, k2, nt_dims, preferred_element_type=F32))
        a_mats, p_mats, gcols, bcols = [], [], [], []
        for n, d in chains:
            hl, ch = units[n]
            gcol, bcol = gate_cols(hl, ch, d)
            grow = gr_ref[hl, ch][2 * d:2 * d + 1, :]
            incl = jnp.logical_and(same_head, (ii >= jj) if d == 0 else (ii <= jj))
            strict = jnp.logical_and(same_head, (ii > jj) if d == 0 else (ii < jj))
            decay = jnp.exp(jnp.where(incl, gcol - grow, -1e30))
            a_mats.append(jnp.where(strict, bcol * grams[n][:pc] * decay, 0.0))
            p_mats.append(grams[n][pc:] * decay)
            gcols.append(gcol)
            bcols.append(bcol)
        t_invs = _inv_unit_triangular_many(a_mats)
        rhss, kfs = [], []
        for (n, d), gcol, bcol in zip(chains, gcols, bcols):
            hl, ch = units[n]
            kf = k2s[n].astype(F32)
            rows = rows_of(ch)
            vp = jnp.concatenate([v_ref[rows, v_lanes(hl, 0)], v_ref[rows, v_lanes(hl, 1)]], axis=0)
            rhss.append(jnp.concatenate([(bcol * jnp.exp(gcol)) * kf, bcol * vp], axis=1).astype(BF16))
            kfs.append(kf)
        uws = [_dot(t.astype(BF16), r) for t, r in zip(t_invs, rhss)]
        kdts, x2s, egls = [], [], []
        for (n, d), gcol, kf, uw in zip(chains, gcols, kfs, uws):
            r0 = c - 1 if d == 0 else 0
            gl0 = gcol[r0:r0 + 1]
            gl1 = gcol[c + r0:c + r0 + 1]
            gl = jnp.where(top_col, gl0, gl1)
            kdts.append((kf * jnp.exp(gl - gcol)).T.astype(BF16))
            u, wv = uw[:, :dv], uw[:, dv:]
            x2s.append(jnp.concatenate([jnp.where(top, u, 0.0), jnp.where(top, 0.0, u),
                                        jnp.where(top, wv, 0.0), jnp.where(top, 0.0, wv)], axis=1).astype(BF16))
            egls.append(jnp.concatenate([jnp.broadcast_to(jnp.exp(gl0), (1, dv)),
                                         jnp.broadcast_to(jnp.exp(gl1), (1, dv))], axis=1))
        wcs = [_dot(kdt, x2) for kdt, x2 in zip(kdts, x2s)]
        for (n, d), p_mat, uw, wc, egl in zip(chains, p_mats, uws, wcs, egls):
            e = entry(*units[n], d)
            wm_ref[e] = wc[:, :2 * dv].astype(BF16)
            cm_ref[e] = wc[:, 2 * dv:]
            egl_ref[e] = egl
            p_ref[e] = p_mat.astype(BF16)
            u_ref[e] = uw[:, :dv].astype(BF16)
            wv_ref[e] = uw[:, dv:]

    def phase_b(n, carry):
        lines = [(hl, d) for hl in range(heads) for d in (0, 1)]
        es = [entry(hl, n if d == 0 else n_chunks - 1 - n, d) for hl, d in lines]
        ss = [s_ref[2 * hl + d] for hl, d in lines]
        sbs = [s.astype(BF16) for s in ss]
        for e, sb in zip(es, sbs):
            sh_ref[e] = sb
        wms = [wm_ref[e] for e in es]
        wss = [jnp.concatenate([_dot(wm[:, :dv], sb[:, :dv]), _dot(wm[:, dv:], sb[:, dv:])], axis=1)
               for wm, sb in zip(wms, sbs)]
        for (hl, d), e, s, ws in zip(lines, es, ss, wss):
            s_ref[2 * hl + d] = egl_ref[e] * s - ws + cm_ref[e]
        return carry

    def phase_c(base):
        units = [(hl, base + i) for hl in range(heads) for i in range(group)]
        chains = [(n, d) for n in range(len(units)) for d in (0, 1)]
        es = [entry(*units[n], d) for n, d in chains]
        rs = []
        for (n, d), e in zip(chains, es):
            hl, ch = units[n]
            lhs = jnp.concatenate([q_ref[rows_of(ch), qk_lanes(hl)], u_ref[e]], axis=0)
            rs.append(_dot(lhs, sh_ref[e]))
        v_news, q_ss = [], []
        for e, r in zip(es, rs):
            q_ss.append(jnp.concatenate([r[:c, :dv], r[:c, dv:]], axis=0))
            u_s = jnp.concatenate([r[c:2 * c, :dv], r[2 * c:, dv:]], axis=0)
            v_news.append((wv_ref[e] - u_s).astype(BF16))
        pvs = [_dot(p_ref[e], v_new) for e, v_new in zip(es, v_news)]
        outs = []
        for (n, d), q_s, pv in zip(chains, q_ss, pvs):
            gcol, _ = gate_cols(*units[n], d)
            outs.append(jnp.exp(gcol) * q_s + pv)
        for n, (hl, ch) in enumerate(units):
            o = outs[2 * n] + outs[2 * n + 1]
            rows = rows_of(ch)
            for hh in range(2):
                y = _rms(o[hh * c:(hh + 1) * c], nw_ref[...])
                og_ref[rows, v_lanes(hl, hh)] = (y * _silu(z_ref[rows, v_lanes(hl, hh)])).astype(og_ref.dtype)

    def over_groups(phase):
        n_groups = n_chunks // group
        if n_groups == 1:
            phase(0)
        else:
            def body(g, carry):
                phase(g * group)
                return carry
            lax.fori_loop(0, n_groups, body, 0)

    over_groups(phase_a)
    lax.fori_loop(0, n_chunks, phase_b, 0)
    over_groups(phase_c)

    if has_state_out:
        for hl in range(heads):
            for hh in range(2):
                sf_ref[2 * hl + hh] = s_ref[2 * hl][:, hh * dv:(hh + 1) * dv]
                sb_ref[2 * hl + hh] = s_ref[2 * hl + 1][:, hh * dv:(hh + 1) * dv]


def dn_core(qk, v, p, gates_col, gates_row, norm_w, layer, *, seq_len, n_seq, row0, heads, group,
            init_states=None, og_prev=None):
    m = qk.shape[0]
    n_chunks = seq_len // DN_CHUNK
    rb0 = row0 // seq_len
    qkw = heads * DN_DK
    pair = 2 * DN_DV
    vw = heads * pair
    pc = 2 * DN_CHUNK
    n_e = 2 * n_chunks * heads
    assert n_chunks % group == 0 and DN_HEADS_K % heads == 0
    z_cb0 = DN_QKV_DIM // vw
    has_init = init_states is not None
    in_specs = [
        pl.BlockSpec((seq_len, qkw), lambda s, h: (rb0 + s, h)),
        pl.BlockSpec((seq_len, qkw), lambda s, h: (rb0 + s, DN_HEADS_K // heads + h)),
        pl.BlockSpec((seq_len, vw), lambda s, h: (rb0 + s, h)),
        pl.BlockSpec((seq_len, vw), lambda s, h: (rb0 + s, z_cb0 + h)),
        pl.BlockSpec((heads, n_chunks, 2 * DN_CHUNK, 4), lambda s, h: (h, rb0 + s, 0, 0)),
        pl.BlockSpec((heads, n_chunks, 4, 2 * DN_CHUNK), lambda s, h: (h, rb0 + s, 0, 0)),
        pl.BlockSpec((None, 1, DN_DV), lambda s, h: (layer, 0, 0)),
    ]
    args = [qk, qk, v, p, gates_col, gates_row, norm_w]
    og_shape = jax.ShapeDtypeStruct((m, DN_V_DIM), BF16)
    og_spec = pl.BlockSpec((seq_len, vw), lambda s, h: (rb0 + s, h))
    aliases = {}
    if has_init:
        st_spec = pl.BlockSpec((None, None, 2 * heads, DN_DK, DN_DV), lambda s, h: (s, layer, h, 0, 0))
        in_specs += [st_spec, st_spec, pl.BlockSpec(memory_space=pl.ANY)]
        args += [init_states[0], init_states[1], og_prev]
        aliases = {len(args) - 1: 0}
        out_specs = og_spec
        out_shape = og_shape
    else:
        so_spec = pl.BlockSpec((None, None, 2 * heads, DN_DK, DN_DV), lambda s, h: (s, 0, h, 0, 0))
        so_shape = jax.ShapeDtypeStruct((n_seq, 1, DN_HEADS_V, DN_DK, DN_DV), F32)
        out_specs = [og_spec, so_spec, so_spec]
        out_shape = [og_shape, so_shape, so_shape]
    return pl.pallas_call(
        functools.partial(_dn_core_kernel, n_chunks=n_chunks, group=group, heads=heads, has_init=has_init,
                          has_state_out=not has_init),
        grid=(n_seq, DN_HEADS_K // heads),
        in_specs=in_specs,
        out_specs=out_specs,
        out_shape=out_shape,
        scratch_shapes=[
            pltpu.VMEM((n_e, DN_DK, pair), BF16),
            pltpu.VMEM((n_e, DN_DK, pair), F32),
            pltpu.VMEM((n_e, 1, pair), F32),
            pltpu.VMEM((n_e, pc, pc), BF16),
            pltpu.VMEM((n_e, pc, DN_DK), BF16),
            pltpu.VMEM((n_e, pc, DN_DV), F32),
            pltpu.VMEM((n_e, DN_DK, pair), BF16),
            pltpu.VMEM((2 * heads, DN_DK, pair), F32),
        ],
        input_output_aliases=aliases,
        compiler_params=_params("arbitrary", "arbitrary"),
        name="dn_core_lat" if has_init else "dn_core_ctx",
    )(*args)


def _cm_gate_kernel(u_ref, v_ref, lw_ref, lb_ref, ws_ref, bs_ref, o_ref):
    v = v_ref[...]
    mu = jnp.mean(v, axis=-1, keepdims=True)
    vc = v - mu
    vn = vc * lax.rsqrt(jnp.mean(vc * vc, axis=-1, keepdims=True) + EPS) * lw_ref[...] + lb_ref[...]
    vn = vn.astype(BF16)
    for g in range(CM_GROUPS):
        lanes = slice(g * CM_GDIM, (g + 1) * CM_GDIM)
        sp = jnp.dot(ws_ref[g].astype(BF16), vn[:, lanes], preferred_element_type=F32) + bs_ref[g]
        o_ref[:, lanes] = (u_ref[:, lanes] * sp).astype(o_ref.dtype)


def cm_gate(zz, ln_w, ln_b, w_s, b_s_col, layer):
    m = zz.shape[0]
    row = pl.BlockSpec((None, 1, CM_DIM), lambda i: (layer, 0, 0))
    return pl.pallas_call(
        _cm_gate_kernel,
        grid=(m // CM_CHUNK,),
        in_specs=[
            pl.BlockSpec((CM_CHUNK, CM_DIM), lambda i: (i, 0)),
            pl.BlockSpec((CM_CHUNK, CM_DIM), lambda i: (i, 1)),
            row,
            row,
            pl.BlockSpec((None, CM_GROUPS, CM_CHUNK, CM_CHUNK), lambda i: (layer, 0, 0, 0)),
            pl.BlockSpec((None, CM_GROUPS, CM_CHUNK, 1), lambda i: (layer, 0, 0, 0)),
        ],
        out_specs=pl.BlockSpec((CM_CHUNK, CM_DIM), lambda i: (i, 0)),
        out_shape=jax.ShapeDtypeStruct((m, CM_DIM), BF16),
        compiler_params=_params("arbitrary"),
        name="cm_gate",
    )(zz, zz, ln_w, ln_b, w_s, b_s_col)


def _ffn_kernel(x_ref, nw_ref, sh_ref, sc_ref, g_ref, w1_ref, w2_ref, fw_ref, o_ref, h_ref, *, final_norm):
    kk = pl.program_id(1)

    @pl.when(kk == 0)
    def _():
        y = _rms(x_ref[...], nw_ref[...])
        h_ref[...] = (y * (1.0 + sc_ref[...]) + sh_ref[...]).astype(BF16)
        o_ref[...] = jnp.zeros_like(o_ref)

    a = jnp.dot(h_ref[...], w1_ref[...].astype(BF16), preferred_element_type=F32)
    a = jnp.square(jnp.maximum(a, 0.0)).astype(BF16)
    o_ref[...] += jnp.dot(a, w2_ref[...].astype(BF16), preferred_element_type=F32)

    @pl.when(kk == pl.num_programs(1) - 1)
    def _():
        r = x_ref[...] + g_ref[...] * o_ref[...]
        if final_norm:
            r = _rms(r, fw_ref[...])
        o_ref[...] = r


def ffn(x, norm_w, layer, shift, scale, gate, w1, w2, final_w, *, final_norm, tm=1024, tk=512):
    m, d = x.shape
    ff = w1.shape[2]
    mod_spec = pl.BlockSpec((None, 1, d), lambda i, k: (_group_of_row(i * tm), 0, 0))
    return pl.pallas_call(
        functools.partial(_ffn_kernel, final_norm=final_norm),
        grid=(m // tm, ff // tk),
        in_specs=[
            pl.BlockSpec((tm, d), lambda i, k: (i, 0), pipeline_mode=pl.Buffered(1)),
            pl.BlockSpec((None, 1, d), lambda i, k: (layer, 0, 0)),
            mod_spec,
            mod_spec,
            mod_spec,
            pl.BlockSpec((None, d, tk), lambda i, k: (layer, 0, k)),
            pl.BlockSpec((None, tk, d), lambda i, k: (layer, k, 0)),
            pl.BlockSpec((1, d), lambda i, k: (0, 0)),
        ],
        out_specs=pl.BlockSpec((tm, d), lambda i, k: (i, 0), pipeline_mode=pl.Buffered(1)),
        out_shape=jax.ShapeDtypeStruct((m, d), F32),
        scratch_shapes=[pltpu.VMEM((tm, d), BF16)],
        compiler_params=_params("arbitrary", "arbitrary"),
        name="ffn",
    )(x, norm_w.reshape(norm_w.shape[0], 1, d), shift, scale, gate, w1, w2, final_w.reshape(1, d))


def _grid_pos_embed(n_tokens):
    rows = n_tokens // GRID_W
    r = jnp.repeat(jnp.arange(rows), GRID_W).astype(F32)
    col = jnp.tile(jnp.arange(GRID_W), rows).astype(F32)
    quarter = D_MODEL // 4
    freq = 1.0 / (10000.0 ** (jnp.arange(quarter, dtype=F32) / quarter))
    ar = r[:, None] * freq[None, :]
    ac = col[:, None] * freq[None, :]
    return jnp.concatenate([jnp.sin(ar), jnp.cos(ar), jnp.sin(ac), jnp.cos(ac)], axis=-1)


def _deltanet_layer(x, h, j, gate, state_f, state_b, dn_w_in, dn_conv_w, dn_a_log, dn_dt_bias, dn_norm_w, dn_w_out):
    n_main = DN_QKV_DIM + DN_V_DIM
    p = matmul_ws(h, dn_w_in, j, n_main, tm=1024, tn=1024, epilogue=_ep_plain, name="dn_in")
    ab = matmul_ws(h, dn_w_in, j, DN_GATE_COLS, tm=1024, tn=DN_GATE_COLS, epilogue=_ep_plain,
                   col_block0=n_main // DN_GATE_COLS, name="dn_in_gates")
    qk = conv_silu(p, dn_conv_w, j, 0, 2 * DN_K_DIM, normalize=True, out_dtype=BF16)
    v = conv_silu(p, dn_conv_w, j, 2 * DN_K_DIM, DN_V_DIM, normalize=False, out_dtype=F32)

    zeros = jnp.zeros((DN_HEADS_V,), F32)
    a_log_row = jnp.concatenate([dn_a_log[j, 0], zeros, dn_a_log[j, 1], zeros])[None, :]
    dt_row = jnp.concatenate([dn_dt_bias[j, 0], zeros, dn_dt_bias[j, 1], zeros])[None, :]
    gates = dn_gates(ab, a_log_row, dt_row)
    m = gates.shape[0]
    g5 = gates.reshape(m // DN_CHUNK, DN_CHUNK, 4, DN_HEADS_K, 2)
    gates_col = g5.transpose(3, 0, 4, 1, 2).reshape(DN_HEADS_K, m // DN_CHUNK, 2 * DN_CHUNK, 4)
    gates_row = g5.transpose(3, 0, 2, 4, 1).reshape(DN_HEADS_K, m // DN_CHUNK, 4, 2 * DN_CHUNK)
    norm_w = dn_norm_w.reshape(dn_norm_w.shape[0], 1, DN_DV)

    og, s_f, s_b = dn_core(qk, v, p, gates_col, gates_row, norm_w, j, seq_len=SEQ, n_seq=BATCH, row0=0,
                           heads=2, group=SEQ // DN_CHUNK)
    og = dn_core(qk, v, p, gates_col, gates_row, norm_w, j, seq_len=DEC_SEQ, n_seq=DEC_BATCH, row0=N_CTX_TOK,
                 heads=1, group=8, init_states=(state_f, state_b), og_prev=og)
    x = matmul_resid_gate(og, dn_w_out, j, x, gate, tm=1024, tn=512, name="dn_out")
    return x, s_f, s_b


def _chunk_mlp_layer(x, h, j, gate, cm_w_in, cm_b_in, cm_ln_w, cm_ln_b, cm_w_s, cm_b_s, cm_w_out):
    n_in = 2 * CM_DIM
    tn = 1024
    bias = (cm_b_in.reshape(cm_b_in.shape[0], 1, n_in), pl.BlockSpec((None, 1, tn), lambda jj, i: (j, 0, jj)))
    zz = matmul_ws(h, cm_w_in, j, n_in, tm=1024, tn=tn, epilogue=_ep_bias_gelu, extras=(bias,), name="cm_in")
    n_b = cm_ln_w.shape[0]
    uv = cm_gate(zz, cm_ln_w.reshape(n_b, 1, CM_DIM), cm_ln_b.reshape(n_b, 1, CM_DIM), cm_w_s,
                 cm_b_s[..., None], j)
    return matmul_resid_gate(uv, cm_w_out, j, x, gate, tm=1024, tn=512, name="cm_out")


def kernel(x_prompt, x_sample, state_dn_fwd, state_dn_bwd, c, c_ctx, norm_mix_w, norm_mlp_w, w_ada, b_ada, dn_w_in, dn_conv_w, dn_A_log, dn_dt_bias, dn_norm_w, dn_w_out, cm_w_in, cm_b_in, cm_ln_w, cm_ln_b, cm_w_s, cm_b_s, cm_w_out, w_ff1, w_ff2, final_norm_w):
    lat = x_sample + _grid_pos_embed(DEC_SEQ)[None]
    x = jnp.concatenate([x_prompt.reshape(N_CTX_TOK, D_MODEL), lat.reshape(N_LAT_TOK, D_MODEL)], axis=0)
    cond = jnp.concatenate([c_ctx[None, :], c, jnp.zeros((N_COND - 1 - DEC_BATCH, D_MODEL), F32)], axis=0)
    mod = adaln_all(cond, w_ada, b_ada)
    mod = mod.reshape(DEPTH, N_COND, N_MOD, D_MODEL).transpose(0, 2, 1, 3)[:, :, :, None, :]

    new_fwd, new_bwd = [], []
    for i in range(DEPTH):
        j = i // N_MIXERS
        shift, scale, gate, shift2, scale2, gate2 = [mod[i, t] for t in range(N_MOD)]
        h = norm_modulate(x, norm_mix_w, i, shift, scale)
        if i % N_MIXERS == 0:
            x, s_f, s_b = _deltanet_layer(x, h, j, gate, state_dn_fwd, state_dn_bwd, dn_w_in, dn_conv_w,
                                          dn_A_log, dn_dt_bias, dn_norm_w, dn_w_out)
            new_fwd.append(s_f)
            new_bwd.append(s_b)
        else:
            x = _chunk_mlp_layer(x, h, j, gate, cm_w_in, cm_b_in, cm_ln_w, cm_ln_b, cm_w_s, cm_b_s, cm_w_out)
        x = ffn(x, norm_mlp_w, i, shift2, scale2, gate2, w_ff1, w_ff2, final_norm_w, final_norm=(i == DEPTH - 1))

    y_prompt = x[:N_CTX_TOK].reshape(BATCH, SEQ, D_MODEL)
    y_sample = x[N_CTX_TOK:].reshape(DEC_BATCH, DEC_SEQ, D_MODEL)
    return (y_prompt, y_sample, jnp.concatenate(new_fwd, axis=1), jnp.concatenate(new_bwd, axis=1))
```

```python
import functools
import math

import jax
import jax.numpy as jnp
import numpy as np
from jax import lax
from jax.experimental import pallas as pl
from jax.experimental.pallas import tpu as pltpu

F32 = jnp.float32
BF16 = jnp.bfloat16

D_MODEL = 2048
BATCH = 16
SEQ = 256
DEPTH = 2
DEC_BATCH = 2
DEC_SEQ = 1024
GRID_W = 64
N_MIXERS = 2
DN_DK = 128
DN_DV = 128
DN_HEADS_K = D_MODEL // DN_DK
DN_HEADS_V = 2 * DN_HEADS_K
DN_K_DIM = DN_HEADS_K * DN_DK
DN_V_DIM = DN_HEADS_V * DN_DV
DN_QKV_DIM = 2 * DN_K_DIM + DN_V_DIM
DN_GATE_COLS = 4 * DN_HEADS_V
DN_CONV = 5
DN_CHUNK = 64
CM_DIM = 2 * D_MODEL
CM_CHUNK = 128
CM_GROUPS = 16
CM_GDIM = CM_DIM // CM_GROUPS
FF_DIM = 4 * D_MODEL
N_MOD = 6
EPS = 1e-6

N_CTX_TOK = BATCH * SEQ
N_LAT_TOK = DEC_BATCH * DEC_SEQ
N_TOK = N_CTX_TOK + N_LAT_TOK
N_COND = 8
SUBLANES = 8
CONV_HALO = SUBLANES
VMEM_LIMIT_BYTES = 56 * 1024 * 1024


def _group_of_row(row0):
    return jnp.where(row0 < N_CTX_TOK, 0, 1 + (row0 - N_CTX_TOK) // DEC_SEQ)


def _params(*sem):
    return pltpu.CompilerParams(dimension_semantics=sem, vmem_limit_bytes=VMEM_LIMIT_BYTES)


def _rms(x, w):
    return x * lax.rsqrt(jnp.mean(x * x, axis=-1, keepdims=True) + EPS) * w


def _silu(x):
    half = 0.5 * x
    return half + half * jnp.tanh(half)


def _gelu_tanh(x):
    return 0.5 * x * (1.0 + jnp.tanh(math.sqrt(2.0 / math.pi) * (x + 0.044715 * (x * x * x))))


def _adaln_kernel(c_ref, w_ref, b_ref, o_ref):
    x = _silu(c_ref[...]).astype(BF16)
    acc = jnp.dot(x, w_ref[...].astype(BF16), preferred_element_type=F32)
    o_ref[...] = acc + b_ref[...]


def adaln_all(cond, w_ada, b_ada, tn=1024):
    depth, d, n = w_ada.shape
    return pl.pallas_call(
        _adaln_kernel,
        grid=(depth, n // tn),
        in_specs=[
            pl.BlockSpec((N_COND, d), lambda l, j: (0, 0)),
            pl.BlockSpec((None, d, tn), lambda l, j: (l, 0, j)),
            pl.BlockSpec((None, 1, tn), lambda l, j: (l, 0, j)),
        ],
        out_specs=pl.BlockSpec((None, N_COND, tn), lambda l, j: (l, 0, j)),
        out_shape=jax.ShapeDtypeStruct((depth, N_COND, n), F32),
        compiler_params=_params("arbitrary", "arbitrary"),
        name="adaln",
    )(cond, w_ada, b_ada.reshape(depth, 1, n))


def _normmod_kernel(x_ref, w_ref, sh_ref, sc_ref, o_ref):
    y = _rms(x_ref[...], w_ref[...])
    o_ref[...] = (y * (1.0 + sc_ref[...]) + sh_ref[...]).astype(BF16)


def norm_modulate(x, norm_w, layer, shift, scale, tm=512):
    m, d = x.shape
    mod_spec = pl.BlockSpec((None, 1, d), lambda i: (_group_of_row(i * tm), 0, 0))
    return pl.pallas_call(
        _normmod_kernel,
        grid=(m // tm,),
        in_specs=[
            pl.BlockSpec((tm, d), lambda i: (i, 0)),
            pl.BlockSpec((None, 1, d), lambda i: (layer, 0, 0)),
            mod_spec,
            mod_spec,
        ],
        out_specs=pl.BlockSpec((tm, d), lambda i: (i, 0)),
        out_shape=jax.ShapeDtypeStruct((m, d), BF16),
        compiler_params=_params("arbitrary"),
        name="norm_modulate",
    )(x, norm_w.reshape(norm_w.shape[0], 1, d), shift, scale)


def _mm_kernel(*refs, part_tiles, n_extra, epilogue):
    n_parts = len(part_tiles)
    x_refs = refs[:n_parts]
    w_ref = refs[n_parts]
    extra = refs[n_parts + 1:n_parts + 1 + n_extra]
    o_ref = refs[n_parts + 1 + n_extra]
    wb_ref = refs[n_parts + 2 + n_extra]
    i = pl.program_id(1)

    @pl.when(i == 0)
    def _():
        wb_ref[...] = w_ref[...].astype(BF16)

    def run(x_ref):
        acc = jnp.dot(x_ref[...], wb_ref[...], preferred_element_type=F32)
        o_ref[...] = epilogue(acc, *[r[...] for r in extra]).astype(o_ref.dtype)

    if n_parts == 1:
        run(x_refs[0])
    else:
        first = 0
        for x_ref, tiles in zip(x_refs, part_tiles):
            pl.when(jnp.logical_and(i >= first, i < first + tiles))(functools.partial(run, x_ref))
            first += tiles


def matmul_ws(x, w, layer, n_out, *, tm, tn, epilogue, extras=(), out_dtype=F32, col_block0=0, name="matmul"):
    parts = list(x) if isinstance(x, (list, tuple)) else [x]
    k = parts[0].shape[1]
    part_tiles = [part.shape[0] // tm for part in parts]
    m = sum(part.shape[0] for part in parts)
    in_specs = []
    first = 0
    for tiles in part_tiles:
        in_specs.append(pl.BlockSpec(
            (tm, k), lambda j, i, first=first, tiles=tiles: (jnp.clip(i - first, 0, tiles - 1), 0)))
        first += tiles
    in_specs.append(pl.BlockSpec((None, k, tn), lambda j, i: (layer, 0, j + col_block0)))
    in_specs += [spec for _, spec in extras]
    return pl.pallas_call(
        functools.partial(_mm_kernel, part_tiles=tuple(part_tiles), n_extra=len(extras), epilogue=epilogue),
        grid=(n_out // tn, m // tm),
        in_specs=in_specs,
        out_specs=pl.BlockSpec((tm, tn), lambda j, i: (i, j)),
        out_shape=jax.ShapeDtypeStruct((m, n_out), out_dtype),
        scratch_shapes=[pltpu.VMEM((k, tn), BF16)],
        compiler_params=_params("arbitrary", "arbitrary"),
        name=name,
    )(*parts, w, *[a for a, _ in extras])


def _ep_plain(acc):
    return acc


def _ep_bias_gelu(acc, b):
    return _gelu_tanh(acc + b)


def _ep_resid_gate(acc, resid, gate):
    return resid + gate * acc


def matmul_resid_gate(x, w, layer, resid, gate, *, tm, tn, name):
    n_out = resid.shape[1]
    extras = (
        (resid, pl.BlockSpec((tm, tn), lambda j, i: (i, j))),
        (gate, pl.BlockSpec((None, 1, tn), lambda j, i: (_group_of_row(i * tm), 0, j))),
    )
    return matmul_ws(x, w, layer, n_out, tm=tm, tn=tn, epilogue=_ep_resid_gate, extras=extras, name=name)


def _conv_kernel(main_ref, prev_ref, next_ref, w_ref, o_ref, ext_ref, *, rb, normalize, n_q_tiles):
    i = pl.program_id(0)
    j = pl.program_id(1)
    row0 = i * rb
    in_lat = row0 >= N_CTX_TOK
    first = jnp.logical_or(jnp.logical_not(in_lat), (row0 - N_CTX_TOK) % DEC_SEQ == 0)
    last = jnp.logical_or(jnp.logical_not(in_lat), (row0 + rb - N_CTX_TOK) % DEC_SEQ == 0)
    pad = DN_CONV // 2
    scale = jnp.where(j < n_q_tiles, DN_DK ** -0.5, 1.0)
    ext_ref[:CONV_HALO] = jnp.where(first, 0.0, prev_ref[...])
    ext_ref[CONV_HALO:CONV_HALO + rb] = main_ref[...]
    ext_ref[CONV_HALO + rb:] = jnp.where(last, 0.0, next_ref[...])

    def slab(cs, carry):
        lanes = pl.ds(pl.multiple_of(cs * DN_DK, DN_DK), DN_DK)
        acc = None
        for t in range(DN_CONV):
            term = ext_ref[pl.ds(CONV_HALO - pad + t, rb), lanes] * w_ref[t:t + 1, lanes]
            acc = term if acc is None else acc + term
        y = _silu(acc)
        if normalize:
            y = y * (lax.rsqrt(jnp.sum(y * y, axis=-1, keepdims=True) + EPS) * scale)
        o_ref[:, lanes] = y.astype(o_ref.dtype)
        return carry

    lax.fori_loop(0, main_ref.shape[1] // DN_DK, slab, 0, unroll=2)


def conv_silu(p, conv_w, layer, col0, n_cols, *, normalize, out_dtype, rb=SEQ, tc=2048):
    assert rb == SEQ and DEC_SEQ % rb == 0 and col0 % tc == 0 and n_cols % tc == 0
    m = p.shape[0]
    cb0 = col0 // tc
    hb = rb // CONV_HALO
    n_hblk = m // CONV_HALO
    return pl.pallas_call(
        functools.partial(_conv_kernel, rb=rb, normalize=normalize, n_q_tiles=DN_K_DIM // tc),
        grid=(m // rb, n_cols // tc),
        in_specs=[
            pl.BlockSpec((rb, tc), lambda i, j: (i, cb0 + j)),
            pl.BlockSpec((CONV_HALO, tc), lambda i, j: (jnp.maximum(i * hb - 1, 0), cb0 + j)),
            pl.BlockSpec((CONV_HALO, tc), lambda i, j: (jnp.minimum((i + 1) * hb, n_hblk - 1), cb0 + j)),
            pl.BlockSpec((None, DN_CONV, tc), lambda i, j: (layer, 0, cb0 + j)),
        ],
        out_specs=pl.BlockSpec((rb, tc), lambda i, j: (i, j)),
        out_shape=jax.ShapeDtypeStruct((m, n_cols), out_dtype),
        scratch_shapes=[pltpu.VMEM((rb + 2 * CONV_HALO, tc), F32)],
        compiler_params=_params("arbitrary", "arbitrary"),
        name="dn_conv_norm" if normalize else "dn_conv",
    )(p, p, p, conv_w)


def _split3(x):
    hi = x.astype(BF16)
    r1 = x - hi.astype(F32)
    mid = r1.astype(BF16)
    lo = (r1 - mid.astype(F32)).astype(BF16)
    return hi, mid, lo


def _gates_kernel(ab_ref, alog_ref, dtb_ref, o_ref):
    c = DN_CHUNK
    ii = lax.broadcasted_iota(jnp.int32, (c, c), 0)
    jj = lax.broadcasted_iota(jnp.int32, (c, c), 1)
    tril = (ii >= jj).astype(BF16)
    triu = (ii <= jj).astype(BF16)
    lane = lax.broadcasted_iota(jnp.int32, (c, ab_ref.shape[1]), 1)
    kind = lane // DN_HEADS_V
    for n in range(ab_ref.shape[0] // c):
        ab = ab_ref[n * c:(n + 1) * c, :]
        log_g = -jnp.exp(alog_ref[...]) * jax.nn.softplus(ab + dtb_ref[...])
        beta = jax.nn.sigmoid(ab)
        parts = _split3(log_g)
        cum_f = sum(jnp.dot(tril, part, preferred_element_type=F32) for part in parts)
        cum_b = sum(jnp.dot(triu, part, preferred_element_type=F32) for part in parts)
        o_ref[n * c:(n + 1) * c, :] = jnp.where(kind == 0, cum_f, jnp.where(kind == 2, cum_b, beta))


def dn_gates(ab, a_log_row, dt_bias_row, chunks_per_step=8):
    m, n = ab.shape
    rows = chunks_per_step * DN_CHUNK
    return pl.pallas_call(
        _gates_kernel,
        grid=(m // rows,),
        in_specs=[
            pl.BlockSpec((rows, n), lambda i: (i, 0)),
            pl.BlockSpec((1, n), lambda i: (0, 0)),
            pl.BlockSpec((1, n), lambda i: (0, 0)),
        ],
        out_specs=pl.BlockSpec((rows, n), lambda i: (i, 0)),
        out_shape=jax.ShapeDtypeStruct((m, n), F32),
        compiler_params=_params("arbitrary"),
        name="dn_gates",
    )(ab, a_log_row, dt_bias_row)


def _dot(a, b):
    return jnp.dot(a, b, preferred_element_type=F32)


def _inv_unit_triangular_many(a_list):
    n = a_list[0].shape[0]
    ii = lax.broadcasted_iota(jnp.int32, (n, n), 0)
    jj = lax.broadcasted_iota(jnp.int32, (n, n), 1)
    eye = jnp.where(ii == jj, 1.0, 0.0)
    xs = [eye - jnp.where((ii >> 1) == (jj >> 1), a, 0.0) for a in a_list]
    for level in range(1, int(math.log2(DN_CHUNK))):
        joins = jnp.logical_and((ii >> (level + 1)) == (jj >> (level + 1)), (ii >> level) != (jj >> level))
        ns = [jnp.where(joins, a, 0.0).astype(BF16) for a in a_list]
        xbs = [x.astype(BF16) for x in xs]
        ys = [_dot(xb, nn).astype(BF16) for xb, nn in zip(xbs, ns)]
        xs = [x - _dot(y, xb) for x, y, xb in zip(xs, ys, xbs)]
    return xs


def _dn_core_kernel(*refs, n_chunks, group, heads, has_init, has_state_out):
    q_ref, k_ref, v_ref, z_ref, gr_ref, nw_ref = refs[:6]
    pos = 6
    if has_init:
        s0f_ref, s0b_ref = refs[pos:pos + 2]
        pos += 2
    og_ref = refs[pos]
    pos += 1
    if has_state_out:
        sf_ref, sb_ref = refs[pos:pos + 2]
        pos += 2
    wm_ref, cm_ref, egl_ref, p_ref, u_ref, wv_ref, sh_ref, s_ref = refs[pos:pos + 8]

    c = DN_CHUNK
    pc = 2 * c
    dv = DN_DV
    if has_init:
        for hl in range(heads):
            s_ref[2 * hl] = jnp.concatenate([s0f_ref[2 * hl], s0f_ref[2 * hl + 1]], axis=1)
            s_ref[2 * hl + 1] = jnp.concatenate([s0b_ref[2 * hl], s0b_ref[2 * hl + 1]], axis=1)
    else:
        s_ref[...] = jnp.zeros_like(s_ref)

    ii = lax.broadcasted_iota(jnp.int32, (pc, pc), 0)
    jj = lax.broadcasted_iota(jnp.int32, (pc, pc), 1)
    chunk_shift = int(math.log2(c))
    same_head = (ii >> chunk_shift) == (jj >> chunk_shift)
    top = ii < c
    top_col = lax.broadcasted_iota(jnp.int32, (pc, 1), 0) < c
    nt_dims = (((1,), (1,)), ((), ()))

    def rows_of(chunk):
        start = chunk * c
        return pl.ds(start if isinstance(start, int) else pl.multiple_of(start, c), c)

    def gate_cols(hl, chunk, d):
        colv = gr_ref[hl, chunk].T
        return colv[:, 2 * d:2 * d + 1], colv[:, 2 * d + 1:2 * d + 2]

    def entry(hl, chunk, d):
        return (hl * n_chunks + chunk) * 2 + d

    def qk_lanes(hl):
        return slice(hl * DN_DK, (hl + 1) * DN_DK)

    def v_lanes(hl, hh):
        return slice((2 * hl + hh) * dv, (2 * hl + hh + 1) * dv)

    def phase_a(base):
        units = [(hl, base + i) for hl in range(heads) for i in range(group)]
        chains = [(n, d) for n in range(len(units)) for d in (0, 1)]
        k2s, grams = [], []
        for hl, ch in units:
            kc = k_ref[rows_of(ch), qk_lanes(hl)]
            qc = q_ref[rows_of(ch), qk_lanes(hl)]
            k2 = jnp.concatenate([kc, kc], axis=0)
            k2s.append(k2)
            lhs = jnp.concatenate([kc, kc, qc, qc], axis=0)
            grams.append(lax.dot_general(lhs, k2, nt_dims, preferred_element_type=F32))
        a_mats, p_mats, gcols, bcols = [], [], [], []
        for n, d in chains:
            hl, ch = units[n]
            gcol, bcol = gate_cols(hl, ch, d)
            grow = gr_ref[hl, ch][2 * d:2 * d + 1, :]
            incl = jnp.logical_and(same_head, (ii >= jj) if d == 0 else (ii <= jj))
            strict = jnp.logical_and(same_head, (ii > jj) if d == 0 else (ii < jj))
            decay = jnp.exp(jnp.where(incl, gcol - grow, -1e30))
            a_mats.append(jnp.where(strict, bcol * grams[n][:pc] * decay, 0.0))
            p_mats.append(grams[n][pc:] * decay)
            gcols.append(gcol)
            bcols.append(bcol)
        t_invs = _inv_unit_triangular_many(a_mats)
        rhss, kfs = [], []
        for (n, d), gcol, bcol in zip(chains, gcols, bcols):
            hl, ch = units[n]
            kf = k2s[n].astype(F32)
            rows = rows_of(ch)
            vp = jnp.concatenate([v_ref[rows, v_lanes(hl, 0)], v_ref[rows, v_lanes(hl, 1)]], axis=0)
            rhss.append(jnp.concatenate([(bcol * jnp.exp(gcol)) * kf, bcol * vp], axis=1).astype(BF16))
            kfs.append(kf)
        uws = [_dot(t.astype(BF16), r) for t, r in zip(t_invs, rhss)]
        kdts, x2s, egls = [], [], []
        for (n, d), gcol, kf, uw in zip(chains, gcols, kfs, uws):
            r0 = c - 1 if d == 0 else 0
            gl0 = gcol[r0:r0 + 1]
            gl1 = gcol[c + r0:c + r0 + 1]
            gl = jnp.where(top_col, gl0, gl1)
            kdts.append((kf * jnp.exp(gl - gcol)).T.astype(BF16))
            u, wv = uw[:, :dv], uw[:, dv:]
            x2s.append(jnp.concatenate([jnp.where(top, u, 0.0), jnp.where(top, 0.0, u),
                                        jnp.where(top, wv, 0.0), jnp.where(top, 0.0, wv)], axis=1).astype(BF16))
            egls.append(jnp.concatenate([jnp.broadcast_to(jnp.exp(gl0), (1, dv)),
                                         jnp.broadcast_to(jnp.exp(gl1), (1, dv))], axis=1))
        wcs = [_dot(kdt, x2) for kdt, x2 in zip(kdts, x2s)]
        for (n, d), p_mat, uw, wc, egl in zip(chains, p_mats, uws, wcs, egls):
            e = entry(*units[n], d)
            wm_ref[e] = wc[:, :2 * dv].astype(BF16)
            cm_ref[e] = wc[:, 2 * dv:]
            egl_ref[e] = egl
            p_ref[e] = p_mat.astype(BF16)
            u_ref[e] = uw[:, :dv].astype(BF16)
            wv_ref[e] = uw[:, dv:]

    def phase_b(n, carry):
        lines = [(hl, d) for hl in range(heads) for d in (0, 1)]
        es = [entry(hl, n if d == 0 else n_chunks - 1 - n, d) for hl, d in lines]
        ss = [s_ref[2 * hl + d] for hl, d in lines]
        sbs = [s.astype(BF16) for s in ss]
        for e, sb in zip(es, sbs):
            sh_ref[e] = sb
        wms = [wm_ref[e] for e in es]
        wss = [jnp.concatenate([_dot(wm[:, :dv], sb[:, :dv]), _dot(wm[:, dv:], sb[:, dv:])], axis=1)
               for wm, sb in zip(wms, sbs)]
        for (hl, d), e, s, ws in zip(lines, es, ss, wss):
            s_ref[2 * hl + d] = egl_ref[e] * s - ws + cm_ref[e]
        return carry

    def phase_c(base):
        units = [(hl, base + i) for hl in range(heads) for i in range(group)]
        chains = [(n, d) for n in range(len(units)) for d in (0, 1)]
        es = [entry(*units[n], d) for n, d in chains]
        rs = []
        for (n, d), e in zip(chains, es):
            hl, ch = units[n]
            lhs = jnp.concatenate([q_ref[rows_of(ch), qk_lanes(hl)], u_ref[e]], axis=0)
            rs.append(_dot(lhs, sh_ref[e]))
        v_news, q_ss = [], []
        for e, r in zip(es, rs):
            q_ss.append(jnp.concatenate([r[:c, :dv], r[:c, dv:]], axis=0))
            u_s = jnp.concatenate([r[c:2 * c, :dv], r[2 * c:, dv:]], axis=0)
            v_news.append((wv_ref[e] - u_s).astype(BF16))
        pvs = [_dot(p_ref[e], v_new) for e, v_new in zip(es, v_news)]
        outs = []
        for (n, d), q_s, pv in zip(chains, q_ss, pvs):
            gcol, _ = gate_cols(*units[n], d)
            outs.append(jnp.exp(gcol) * q_s + pv)
        for n, (hl, ch) in enumerate(units):
            o = outs[2 * n] + outs[2 * n + 1]
            rows = rows_of(ch)
            for hh in range(2):
                y = _rms(o[hh * c:(hh + 1) * c], nw_ref[...])
                og_ref[rows, v_lanes(hl, hh)] = (y * _silu(z_ref[rows, v_lanes(hl, hh)])).astype(og_ref.dtype)

    def over_groups(phase):
        n_groups = n_chunks // group
        if n_groups == 1:
            phase(0)
        else:
            def body(g, carry):
                phase(g * group)
                return carry
            lax.fori_loop(0, n_groups, body, 0)

    over_groups(phase_a)
    lax.fori_loop(0, n_chunks, phase_b, 0)
    over_groups(phase_c)

    if has_state_out:
        for hl in range(heads):
            for hh in range(2):
                sf_ref[2 * hl + hh] = s_ref[2 * hl][:, hh * dv:(hh + 1) * dv]
                sb_ref[2 * hl + hh] = s_ref[2 * hl + 1][:, hh * dv:(hh + 1) * dv]


def dn_core(qk, v, p, gates_row, norm_w, layer, *, seq_len, n_seq, row0, heads, group, init_states=None):
    n_chunks = seq_len // DN_CHUNK
    rb0 = row0 // seq_len
    qkw = heads * DN_DK
    pair = 2 * DN_DV
    vw = heads * pair
    pc = 2 * DN_CHUNK
    n_e = 2 * n_chunks * heads
    assert n_chunks % group == 0 and DN_HEADS_K % heads == 0
    z_cb0 = DN_QKV_DIM // vw
    has_init = init_states is not None
    in_specs = [
        pl.BlockSpec((seq_len, qkw), lambda s, h: (rb0 + s, h)),
        pl.BlockSpec((seq_len, qkw), lambda s, h: (rb0 + s, DN_HEADS_K // heads + h)),
        pl.BlockSpec((seq_len, vw), lambda s, h: (rb0 + s, h)),
        pl.BlockSpec((seq_len, vw), lambda s, h: (rb0 + s, z_cb0 + h)),
        pl.BlockSpec((heads, n_chunks, SUBLANES, 2 * DN_CHUNK), lambda s, h: (h, rb0 + s, 0, 0)),
        pl.BlockSpec((None, 1, DN_DV), lambda s, h: (layer, 0, 0)),
    ]
    args = [qk, qk, v, p, gates_row, norm_w]
    og_shape = jax.ShapeDtypeStruct((n_seq * seq_len, DN_V_DIM), BF16)
    og_spec = pl.BlockSpec((seq_len, vw), lambda s, h: (s, h))
    if has_init:
        st_spec = pl.BlockSpec((None, None, 2 * heads, DN_DK, DN_DV), lambda s, h: (s, layer, h, 0, 0))
        in_specs += [st_spec, st_spec]
        args += [init_states[0], init_states[1]]
        out_specs = og_spec
        out_shape = og_shape
    else:
        so_spec = pl.BlockSpec((None, None, 2 * heads, DN_DK, DN_DV), lambda s, h: (s, 0, h, 0, 0))
        so_shape = jax.ShapeDtypeStruct((n_seq, 1, DN_HEADS_V, DN_DK, DN_DV), F32)
        out_specs = [og_spec, so_spec, so_spec]
        out_shape = [og_shape, so_shape, so_shape]
    return pl.pallas_call(
        functools.partial(_dn_core_kernel, n_chunks=n_chunks, group=group, heads=heads, has_init=has_init,
                          has_state_out=not has_init),
        grid=(n_seq, DN_HEADS_K // heads),
        in_specs=in_specs,
        out_specs=out_specs,
        out_shape=out_shape,
        scratch_shapes=[
            pltpu.VMEM((n_e, DN_DK, pair), BF16),
            pltpu.VMEM((n_e, DN_DK, pair), F32),
            pltpu.VMEM((n_e, 1, pair), F32),
            pltpu.VMEM((n_e, pc, pc), BF16),
            pltpu.VMEM((n_e, pc, DN_DK), BF16),
            pltpu.VMEM((n_e, pc, DN_DV), F32),
            pltpu.VMEM((n_e, DN_DK, pair), BF16),
            pltpu.VMEM((2 * heads, DN_DK, pair), F32),
        ],
        compiler_params=_params("arbitrary", "arbitrary"),
        name="dn_core_lat" if has_init else "dn_core_ctx",
    )(*args)


def _cm_gate_kernel(u_ref, v_ref, lw_ref, lb_ref, ws_ref, bs_ref, o_ref):
    v = v_ref[...]
    mu = jnp.mean(v, axis=-1, keepdims=True)
    vc = v - mu
    vn = vc * lax.rsqrt(jnp.mean(vc * vc, axis=-1, keepdims=True) + EPS) * lw_ref[...] + lb_ref[...]
    vn = vn.astype(BF16)
    for g in range(CM_GROUPS):
        lanes = slice(g * CM_GDIM, (g + 1) * CM_GDIM)
        sp = jnp.dot(ws_ref[g].astype(BF16), vn[:, lanes], preferred_element_type=F32) + bs_ref[g]
        o_ref[:, lanes] = (u_ref[:, lanes] * sp).astype(o_ref.dtype)


def cm_gate(zz, ln_w, ln_b, w_s, b_s_col, layer):
    m = zz.shape[0]
    row = pl.BlockSpec((None, 1, CM_DIM), lambda i: (layer, 0, 0))
    return pl.pallas_call(
        _cm_gate_kernel,
        grid=(m // CM_CHUNK,),
        in_specs=[
            pl.BlockSpec((CM_CHUNK, CM_DIM), lambda i: (i, 0)),
            pl.BlockSpec((CM_CHUNK, CM_DIM), lambda i: (i, 1)),
            row,
            row,
            pl.BlockSpec((None, CM_GROUPS, CM_CHUNK, CM_CHUNK), lambda i: (layer, 0, 0, 0)),
            pl.BlockSpec((None, CM_GROUPS, CM_CHUNK, 1), lambda i: (layer, 0, 0, 0)),
        ],
        out_specs=pl.BlockSpec((CM_CHUNK, CM_DIM), lambda i: (i, 0)),
        out_shape=jax.ShapeDtypeStruct((m, CM_DIM), BF16),
        compiler_params=_params("arbitrary"),
        name="cm_gate",
    )(zz, zz, ln_w, ln_b, w_s, b_s_col)


def _ffn_kernel(x_ref, nw_ref, sh_ref, sc_ref, g_ref, w1_ref, w2_ref, fw_ref, o_ref, h_ref, *, final_norm):
    kk = pl.program_id(1)

    @pl.when(kk == 0)
    def _():
        y = _rms(x_ref[...], nw_ref[...])
        h_ref[...] = (y * (1.0 + sc_ref[...]) + sh_ref[...]).astype(BF16)
        o_ref[...] = jnp.zeros_like(o_ref)

    a = jnp.dot(h_ref[...], w1_ref[...].astype(BF16), preferred_element_type=F32)
    a = jnp.square(jnp.maximum(a, 0.0)).astype(BF16)
    o_ref[...] += jnp.dot(a, w2_ref[...].astype(BF16), preferred_element_type=F32)

    @pl.when(kk == pl.num_programs(1) - 1)
    def _():
        r = x_ref[...] + g_ref[...] * o_ref[...]
        if final_norm:
            r = _rms(r, fw_ref[...])
        o_ref[...] = r


def ffn(x, norm_w, layer, shift, scale, gate, w1, w2, final_w, *, final_norm, tm=1024, tk=512):
    m, d = x.shape
    ff = w1.shape[2]
    mod_spec = pl.BlockSpec((None, 1, d), lambda i, k: (_group_of_row(i * tm), 0, 0))
    return pl.pallas_call(
        functools.partial(_ffn_kernel, final_norm=final_norm),
        grid=(m // tm, ff // tk),
        in_specs=[
            pl.BlockSpec((tm, d), lambda i, k: (i, 0), pipeline_mode=pl.Buffered(1)),
            pl.BlockSpec((None, 1, d), lambda i, k: (layer, 0, 0)),
            mod_spec,
            mod_spec,
            mod_spec,
            pl.BlockSpec((None, d, tk), lambda i, k: (layer, 0, k)),
            pl.BlockSpec((None, tk, d), lambda i, k: (layer, k, 0)),
            pl.BlockSpec((1, d), lambda i, k: (0, 0)),
        ],
        out_specs=pl.BlockSpec((tm, d), lambda i, k: (i, 0), pipeline_mode=pl.Buffered(1)),
        out_shape=jax.ShapeDtypeStruct((m, d), F32),
        scratch_shapes=[pltpu.VMEM((tm, d), BF16)],
        compiler_params=_params("arbitrary", "arbitrary"),
        name="ffn",
    )(x, norm_w.reshape(norm_w.shape[0], 1, d), shift, scale, gate, w1, w2, final_w.reshape(1, d))


def _grid_pos_embed(n_tokens):
    rows = n_tokens // GRID_W
    r = np.repeat(np.arange(rows), GRID_W).astype(np.float64)
    col = np.tile(np.arange(GRID_W), rows).astype(np.float64)
    quarter = D_MODEL // 4
    freq = 1.0 / (10000.0 ** (np.arange(quarter, dtype=np.float64) / quarter))
    ar = r[:, None] * freq[None, :]
    ac = col[:, None] * freq[None, :]
    return np.concatenate([np.sin(ar), np.cos(ar), np.sin(ac), np.cos(ac)], axis=-1)


def _deltanet_layer(x, h, j, gate, state_f, state_b, dn_w_in, dn_conv_w, dn_a_log, dn_dt_bias, dn_norm_w, dn_w_out):
    n_main = DN_QKV_DIM + DN_V_DIM
    p = matmul_ws(h, dn_w_in, j, n_main, tm=1024, tn=1024, epilogue=_ep_plain, name="dn_in")
    ab = matmul_ws(h, dn_w_in, j, DN_GATE_COLS, tm=1024, tn=DN_GATE_COLS, epilogue=_ep_plain,
                   col_block0=n_main // DN_GATE_COLS, name="dn_in_gates")
    qk = conv_silu(p, dn_conv_w, j, 0, 2 * DN_K_DIM, normalize=True, out_dtype=BF16)
    v = conv_silu(p, dn_conv_w, j, 2 * DN_K_DIM, DN_V_DIM, normalize=False, out_dtype=F32)

    zeros = jnp.zeros((DN_HEADS_V,), F32)
    a_log_row = jnp.concatenate([dn_a_log[j, 0], zeros, dn_a_log[j, 1], zeros])[None, :]
    dt_row = jnp.concatenate([dn_dt_bias[j, 0], zeros, dn_dt_bias[j, 1], zeros])[None, :]
    gates = dn_gates(ab, a_log_row, dt_row)
    m = gates.shape[0]
    g5 = gates.reshape(m // DN_CHUNK, DN_CHUNK, 4, DN_HEADS_K, 2)
    gates_row = g5.transpose(3, 0, 2, 4, 1).reshape(DN_HEADS_K, m // DN_CHUNK, 4, 2 * DN_CHUNK)
    gates_row = jnp.pad(gates_row, ((0, 0), (0, 0), (0, SUBLANES - 4), (0, 0)))
    norm_w = dn_norm_w.reshape(dn_norm_w.shape[0], 1, DN_DV)

    og_ctx, s_f, s_b = dn_core(qk, v, p, gates_row, norm_w, j, seq_len=SEQ, n_seq=BATCH, row0=0,
                               heads=2, group=SEQ // DN_CHUNK)
    og_lat = dn_core(qk, v, p, gates_row, norm_w, j, seq_len=DEC_SEQ, n_seq=DEC_BATCH, row0=N_CTX_TOK,
                     heads=1, group=8, init_states=(state_f, state_b))
    x = matmul_resid_gate([og_ctx, og_lat], dn_w_out, j, x, gate, tm=512, tn=512, name="dn_out")
    return x, s_f, s_b


def _chunk_mlp_layer(x, h, j, gate, cm_w_in, cm_b_in, cm_ln_w, cm_ln_b, cm_w_s, cm_b_s, cm_w_out):
    n_in = 2 * CM_DIM
    tn = 1024
    bias = (cm_b_in.reshape(cm_b_in.shape[0], 1, n_in), pl.BlockSpec((None, 1, tn), lambda jj, i: (j, 0, jj)))
    zz = matmul_ws(h, cm_w_in, j, n_in, tm=1024, tn=tn, epilogue=_ep_bias_gelu, extras=(bias,), name="cm_in")
    n_b = cm_ln_w.shape[0]
    uv = cm_gate(zz, cm_ln_w.reshape(n_b, 1, CM_DIM), cm_ln_b.reshape(n_b, 1, CM_DIM), cm_w_s,
                 cm_b_s[..., None], j)
    return matmul_resid_gate(uv, cm_w_out, j, x, gate, tm=1024, tn=512, name="cm_out")


def kernel(x_prompt, x_sample, state_dn_fwd, state_dn_bwd, c, c_ctx, norm_mix_w, norm_mlp_w, w_ada, b_ada, dn_w_in, dn_conv_w, dn_A_log, dn_dt_bias, dn_norm_w, dn_w_out, cm_w_in, cm_b_in, cm_ln_w, cm_ln_b, cm_w_s, cm_b_s, cm_w_out, w_ff1, w_ff2, final_norm_w):
    lat = x_sample + jnp.asarray(_grid_pos_embed(DEC_SEQ), x_sample.dtype)[None]
    x = jnp.concatenate([x_prompt.reshape(N_CTX_TOK, D_MODEL), lat.reshape(N_LAT_TOK, D_MODEL)], axis=0)
    cond = jnp.concatenate([c_ctx[None, :], c, jnp.zeros((N_COND - 1 - DEC_BATCH, D_MODEL), F32)], axis=0)
    mod = adaln_all(cond, w_ada, b_ada)
    mod = mod.reshape(DEPTH, N_COND, N_MOD, D_MODEL).transpose(0, 2, 1, 3)[:, :, :, None, :]

    new_fwd, new_bwd = [], []
    for i in range(DEPTH):
        j = i // N_MIXERS
        shift, scale, gate, shift2, scale2, gate2 = [mod[i, t] for t in range(N_MOD)]
        h = norm_modulate(x, norm_mix_w, i, shift, scale)
        if i % N_MIXERS == 0:
            x, s_f, s_b = _deltanet_layer(x, h, j, gate, state_dn_fwd, state_dn_bwd, dn_w_in, dn_conv_w,
                                          dn_A_log, dn_dt_bias, dn_norm_w, dn_w_out)
            new_fwd.append(s_f)
            new_bwd.append(s_b)
        else:
            x = _chunk_mlp_layer(x, h, j, gate, cm_w_in, cm_b_in, cm_ln_w, cm_ln_b, cm_w_s, cm_b_s, cm_w_out)
        x = ffn(x, norm_mlp_w, i, shift2, scale2, gate2, w_ff1, w_ff2, final_norm_w, final_norm=(i == DEPTH - 1))

    y_prompt = x[:N_CTX_TOK].reshape(BATCH, SEQ, D_MODEL)
    y_sample = x[N_CTX_TOK:].reshape(DEC_BATCH, DEC_SEQ, D_MODEL)
    return (y_prompt, y_sample, jnp.concatenate(new_fwd, axis=1), jnp.concatenate(new_bwd, axis=1))
```

```python
import functools
import math

import jax
import jax.numpy as jnp
import numpy as np
from jax import lax
from jax.experimental import pallas as pl
from jax.experimental.pallas import tpu as pltpu

F32 = jnp.float32
BF16 = jnp.bfloat16

D_MODEL = 2048
BATCH = 16
SEQ = 256
DEPTH = 2
DEC_BATCH = 2
DEC_SEQ = 1024
GRID_W = 64
N_MIXERS = 2
DN_DK = 128
DN_DV = 128
DN_HEADS_K = D_MODEL // DN_DK
DN_HEADS_V = 2 * DN_HEADS_K
DN_K_DIM = DN_HEADS_K * DN_DK
DN_V_DIM = DN_HEADS_V * DN_DV
DN_QKV_DIM = 2 * DN_K_DIM + DN_V_DIM
DN_GATE_COLS = 4 * DN_HEADS_V
DN_CONV = 5
DN_CHUNK = 64
CM_DIM = 2 * D_MODEL
CM_CHUNK = 128
CM_GROUPS = 16
CM_GDIM = CM_DIM // CM_GROUPS
FF_DIM = 4 * D_MODEL
N_MOD = 6
EPS = 1e-6

N_CTX_TOK = BATCH * SEQ
N_LAT_TOK = DEC_BATCH * DEC_SEQ
N_TOK = N_CTX_TOK + N_LAT_TOK
N_COND = 8
SUBLANES = 8
CONV_HALO = SUBLANES
VMEM_LIMIT_BYTES = 56 * 1024 * 1024


def _group_of_row(row0):
    return jnp.where(row0 < N_CTX_TOK, 0, 1 + (row0 - N_CTX_TOK) // DEC_SEQ)


def _params(*sem):
    return pltpu.CompilerParams(dimension_semantics=sem, vmem_limit_bytes=VMEM_LIMIT_BYTES)


def _rms(x, w):
    return x * lax.rsqrt(jnp.mean(x * x, axis=-1, keepdims=True) + EPS) * w


def _silu(x):
    half = 0.5 * x
    return half + half * jnp.tanh(half)


def _gelu_tanh(x):
    return 0.5 * x * (1.0 + jnp.tanh(math.sqrt(2.0 / math.pi) * (x + 0.044715 * (x * x * x))))


def _adaln_kernel(c_ref, w_ref, b_ref, o_ref):
    x = _silu(c_ref[...]).astype(BF16)
    acc = jnp.dot(x, w_ref[...].astype(BF16), preferred_element_type=F32)
    o_ref[...] = acc + b_ref[...]


def adaln_all(cond, w_ada, b_ada, tn=1024):
    depth, d, n = w_ada.shape
    return pl.pallas_call(
        _adaln_kernel,
        grid=(depth, n // tn),
        in_specs=[
            pl.BlockSpec((N_COND, d), lambda l, j: (0, 0)),
            pl.BlockSpec((None, d, tn), lambda l, j: (l, 0, j)),
            pl.BlockSpec((None, 1, tn), lambda l, j: (l, 0, j)),
        ],
        out_specs=pl.BlockSpec((None, N_COND, tn), lambda l, j: (l, 0, j)),
        out_shape=jax.ShapeDtypeStruct((depth, N_COND, n), F32),
        compiler_params=_params("arbitrary", "arbitrary"),
        name="adaln",
    )(cond, w_ada, b_ada.reshape(depth, 1, n))


def _embed_kernel(xc_ref, xl_ref, pe_ref, w_ref, sh_ref, sc_ref, x_ref, h_ref, *, ctx_tiles):
    def emit(x):
        x_ref[...] = x
        h_ref[...] = (_rms(x, w_ref[...]) * (1.0 + sc_ref[...]) + sh_ref[...]).astype(BF16)

    i = pl.program_id(0)
    pl.when(i < ctx_tiles)(lambda: emit(xc_ref[...]))
    pl.when(i >= ctx_tiles)(lambda: emit(xl_ref[...] + pe_ref[...]))


def embed_norm_modulate(x_ctx, x_lat, pos_table, norm_w, layer, shift, scale, tm=512):
    d = x_ctx.shape[1]
    ctx_tiles = x_ctx.shape[0] // tm
    lat_tiles = x_lat.shape[0] // tm
    pos_tiles = pos_table.shape[0] // tm
    m = x_ctx.shape[0] + x_lat.shape[0]
    mod_spec = pl.BlockSpec((None, 1, d), lambda i: (_group_of_row(i * tm), 0, 0))
    row_spec = pl.BlockSpec((tm, d), lambda i: (i, 0))
    return pl.pallas_call(
        functools.partial(_embed_kernel, ctx_tiles=ctx_tiles),
        grid=(ctx_tiles + lat_tiles,),
        in_specs=[
            pl.BlockSpec((tm, d), lambda i: (jnp.minimum(i, ctx_tiles - 1), 0)),
            pl.BlockSpec((tm, d), lambda i: (jnp.clip(i - ctx_tiles, 0, lat_tiles - 1), 0)),
            pl.BlockSpec((tm, d), lambda i: (jnp.maximum(i - ctx_tiles, 0) % pos_tiles, 0)),
            pl.BlockSpec((None, 1, d), lambda i: (layer, 0, 0)),
            mod_spec,
            mod_spec,
        ],
        out_specs=[row_spec, row_spec],
        out_shape=[jax.ShapeDtypeStruct((m, d), F32), jax.ShapeDtypeStruct((m, d), BF16)],
        compiler_params=_params("arbitrary"),
        name="embed_norm_modulate",
    )(x_ctx, x_lat, pos_table, norm_w.reshape(norm_w.shape[0], 1, d), shift, scale)


def _mm_kernel(*refs, part_tiles, n_extra, epilogue):
    n_parts = len(part_tiles)
    x_refs = refs[:n_parts]
    w_ref = refs[n_parts]
    extra = refs[n_parts + 1:n_parts + 1 + n_extra]
    o_ref = refs[n_parts + 1 + n_extra]
    wb_ref = refs[n_parts + 2 + n_extra]
    i = pl.program_id(1)

    @pl.when(i == 0)
    def _():
        wb_ref[...] = w_ref[...].astype(BF16)

    def run(x_ref):
        acc = jnp.dot(x_ref[...], wb_ref[...], preferred_element_type=F32)
        o_ref[...] = epilogue(acc, *[r[...] for r in extra]).astype(o_ref.dtype)

    if n_parts == 1:
        run(x_refs[0])
    else:
        first = 0
        for x_ref, tiles in zip(x_refs, part_tiles):
            pl.when(jnp.logical_and(i >= first, i < first + tiles))(functools.partial(run, x_ref))
            first += tiles


def matmul_ws(x, w, layer, n_out, *, tm, tn, epilogue, extras=(), out_dtype=F32, col_block0=0, name="matmul"):
    parts = list(x) if isinstance(x, (list, tuple)) else [x]
    k = parts[0].shape[1]
    part_tiles = [part.shape[0] // tm for part in parts]
    m = sum(part.shape[0] for part in parts)
    in_specs = []
    first = 0
    for tiles in part_tiles:
        in_specs.append(pl.BlockSpec(
            (tm, k), lambda j, i, first=first, tiles=tiles: (jnp.clip(i - first, 0, tiles - 1), 0)))
        first += tiles
    in_specs.append(pl.BlockSpec((None, k, tn), lambda j, i: (layer, 0, j + col_block0)))
    in_specs += [spec for _, spec in extras]
    return pl.pallas_call(
        functools.partial(_mm_kernel, part_tiles=tuple(part_tiles), n_extra=len(extras), epilogue=epilogue),
        grid=(n_out // tn, m // tm),
        in_specs=in_specs,
        out_specs=pl.BlockSpec((tm, tn), lambda j, i: (i, j)),
        out_shape=jax.ShapeDtypeStruct((m, n_out), out_dtype),
        scratch_shapes=[pltpu.VMEM((k, tn), BF16)],
        compiler_params=_params("arbitrary", "arbitrary"),
        name=name,
    )(*parts, w, *[a for a, _ in extras])


def _ep_plain(acc):
    return acc


def _ep_bias_gelu(acc, b):
    return _gelu_tanh(acc + b)


def _ep_resid_gate(acc, resid, gate):
    return resid + gate * acc


def matmul_resid_gate(x, w, layer, resid, gate, *, tm, tn, name):
    n_out = resid.shape[1]
    extras = (
        (resid, pl.BlockSpec((tm, tn), lambda j, i: (i, j))),
        (gate, pl.BlockSpec((None, 1, tn), lambda j, i: (_group_of_row(i * tm), 0, j))),
    )
    return matmul_ws(x, w, layer, n_out, tm=tm, tn=tn, epilogue=_ep_resid_gate, extras=extras, name=name)


def _conv_kernel(main_ref, prev_ref, next_ref, w_ref, o_ref, ext_ref, *, rb, normalize, n_q_tiles):
    i = pl.program_id(0)
    j = pl.program_id(1)
    row0 = i * rb
    in_lat = row0 >= N_CTX_TOK
    first = jnp.logical_or(jnp.logical_not(in_lat), (row0 - N_CTX_TOK) % DEC_SEQ == 0)
    last = jnp.logical_or(jnp.logical_not(in_lat), (row0 + rb - N_CTX_TOK) % DEC_SEQ == 0)
    pad = DN_CONV // 2
    scale = jnp.where(j < n_q_tiles, DN_DK ** -0.5, 1.0)
    ext_ref[:CONV_HALO] = jnp.where(first, 0.0, prev_ref[...])
    ext_ref[CONV_HALO:CONV_HALO + rb] = main_ref[...]
    ext_ref[CONV_HALO + rb:] = jnp.where(last, 0.0, next_ref[...])

    def slab(cs, carry):
        lanes = pl.ds(pl.multiple_of(cs * DN_DK, DN_DK), DN_DK)
        acc = None
        for t in range(DN_CONV):
            term = ext_ref[pl.ds(CONV_HALO - pad + t, rb), lanes] * w_ref[t:t + 1, lanes]
            acc = term if acc is None else acc + term
        y = _silu(acc)
        if normalize:
            y = y * (lax.rsqrt(jnp.sum(y * y, axis=-1, keepdims=True) + EPS) * scale)
        o_ref[:, lanes] = y.astype(o_ref.dtype)
        return carry

    lax.fori_loop(0, main_ref.shape[1] // DN_DK, slab, 0, unroll=2)


def conv_silu(p, conv_w, layer, col0, n_cols, *, normalize, out_dtype, rb=SEQ, tc=2048):
    assert rb == SEQ and DEC_SEQ % rb == 0 and col0 % tc == 0 and n_cols % tc == 0
    m = p.shape[0]
    cb0 = col0 // tc
    hb = rb // CONV_HALO
    n_hblk = m // CONV_HALO
    return pl.pallas_call(
        functools.partial(_conv_kernel, rb=rb, normalize=normalize, n_q_tiles=DN_K_DIM // tc),
        grid=(m // rb, n_cols // tc),
        in_specs=[
            pl.BlockSpec((rb, tc), lambda i, j: (i, cb0 + j)),
            pl.BlockSpec((CONV_HALO, tc), lambda i, j: (jnp.maximum(i * hb - 1, 0), cb0 + j)),
            pl.BlockSpec((CONV_HALO, tc), lambda i, j: (jnp.minimum((i + 1) * hb, n_hblk - 1), cb0 + j)),
            pl.BlockSpec((None, DN_CONV, tc), lambda i, j: (layer, 0, cb0 + j)),
        ],
        out_specs=pl.BlockSpec((rb, tc), lambda i, j: (i, j)),
        out_shape=jax.ShapeDtypeStruct((m, n_cols), out_dtype),
        scratch_shapes=[pltpu.VMEM((rb + 2 * CONV_HALO, tc), F32)],
        compiler_params=_params("arbitrary", "arbitrary"),
        name="dn_conv_norm" if normalize else "dn_conv",
    )(p, p, p, conv_w)


def _split3(x):
    hi = x.astype(BF16)
    r1 = x - hi.astype(F32)
    mid = r1.astype(BF16)
    lo = (r1 - mid.astype(F32)).astype(BF16)
    return hi, mid, lo


def _gates_kernel(ab_ref, alog_ref, dtb_ref, o_ref):
    c = DN_CHUNK
    ii = lax.broadcasted_iota(jnp.int32, (c, c), 0)
    jj = lax.broadcasted_iota(jnp.int32, (c, c), 1)
    tril = (ii >= jj).astype(BF16)
    triu = (ii <= jj).astype(BF16)
    lane = lax.broadcasted_iota(jnp.int32, (c, ab_ref.shape[1]), 1)
    kind = lane // DN_HEADS_V
    for n in range(ab_ref.shape[0] // c):
        ab = ab_ref[n * c:(n + 1) * c, :]
        log_g = -jnp.exp(alog_ref[...]) * jax.nn.softplus(ab + dtb_ref[...])
        beta = jax.nn.sigmoid(ab)
        parts = _split3(log_g)
        cum_f = sum(jnp.dot(tril, part, preferred_element_type=F32) for part in parts)
        cum_b = sum(jnp.dot(triu, part, preferred_element_type=F32) for part in parts)
        o_ref[n * c:(n + 1) * c, :] = jnp.where(kind == 0, cum_f, jnp.where(kind == 2, cum_b, beta))


def dn_gates(ab, a_log_row, dt_bias_row, chunks_per_step=8):
    m, n = ab.shape
    rows = chunks_per_step * DN_CHUNK
    return pl.pallas_call(
        _gates_kernel,
        grid=(m // rows,),
        in_specs=[
            pl.BlockSpec((rows, n), lambda i: (i, 0)),
            pl.BlockSpec((1, n), lambda i: (0, 0)),
            pl.BlockSpec((1, n), lambda i: (0, 0)),
        ],
        out_specs=pl.BlockSpec((rows, n), lambda i: (i, 0)),
        out_shape=jax.ShapeDtypeStruct((m, n), F32),
        compiler_params=_params("arbitrary"),
        name="dn_gates",
    )(ab, a_log_row, dt_bias_row)


def _dot(a, b):
    return jnp.dot(a, b, preferred_element_type=F32)


def _inv_unit_triangular_many(a_list):
    n = a_list[0].shape[0]
    ii = lax.broadcasted_iota(jnp.int32, (n, n), 0)
    jj = lax.broadcasted_iota(jnp.int32, (n, n), 1)
    eye = jnp.where(ii == jj, 1.0, 0.0)
    xs = [eye - jnp.where((ii >> 1) == (jj >> 1), a, 0.0) for a in a_list]
    for level in range(1, int(math.log2(DN_CHUNK))):
        joins = jnp.logical_and((ii >> (level + 1)) == (jj >> (level + 1)), (ii >> level) != (jj >> level))
        ns = [jnp.where(joins, a, 0.0).astype(BF16) for a in a_list]
        xbs = [x.astype(BF16) for x in xs]
        ys = [_dot(xb, nn).astype(BF16) for xb, nn in zip(xbs, ns)]
        xs = [x - _dot(y, xb) for x, y, xb in zip(xs, ys, xbs)]
    return xs


def _dn_core_kernel(*refs, n_chunks, group, heads, has_init, has_state_out):
    q_ref, k_ref, v_ref, z_ref, gr_ref, nw_ref = refs[:6]
    pos = 6
    if has_init:
        s0f_ref, s0b_ref = refs[pos:pos + 2]
        pos += 2
    og_ref = refs[pos]
    pos += 1
    if has_state_out:
        sf_ref, sb_ref = refs[pos:pos + 2]
        pos += 2
    wm_ref, cm_ref, egl_ref, p_ref, u_ref, wv_ref, sh_ref, s_ref = refs[pos:pos + 8]

    c = DN_CHUNK
    pc = 2 * c
    dv = DN_DV
    if has_init:
        for hl in range(heads):
            s_ref[2 * hl] = jnp.concatenate([s0f_ref[2 * hl], s0f_ref[2 * hl + 1]], axis=1)
            s_ref[2 * hl + 1] = jnp.concatenate([s0b_ref[2 * hl], s0b_ref[2 * hl + 1]], axis=1)
    else:
        s_ref[...] = jnp.zeros_like(s_ref)

    ii = lax.broadcasted_iota(jnp.int32, (pc, pc), 0)
    jj = lax.broadcasted_iota(jnp.int32, (pc, pc), 1)
    chunk_shift = int(math.log2(c))
    same_head = (ii >> chunk_shift) == (jj >> chunk_shift)
    top = ii < c
    top_col = lax.broadcasted_iota(jnp.int32, (pc, 1), 0) < c
    nt_dims = (((1,), (1,)), ((), ()))

    def rows_of(chunk):
        start = chunk * c
        return pl.ds(start if isinstance(start, int) else pl.multiple_of(start, c), c)

    def gate_columns(units):
        return [gr_ref[hl, ch].T for hl, ch in units]

    def entry(hl, chunk, d):
        return (hl * n_chunks + chunk) * 2 + d

    def qk_lanes(hl):
        return slice(hl * DN_DK, (hl + 1) * DN_DK)

    def v_lanes(hl, hh):
        return slice((2 * hl + hh) * dv, (2 * hl + hh + 1) * dv)

    def phase_a(base):
        units = [(hl, base + i) for hl in range(heads) for i in range(group)]
        chains = [(n, d) for n in range(len(units)) for d in (0, 1)]
        k2s, grams = [], []
        for hl, ch in units:
            kc = k_ref[rows_of(ch), qk_lanes(hl)]
            qc = q_ref[rows_of(ch), qk_lanes(hl)]
            k2 = jnp.concatenate([kc, kc], axis=0)
            k2s.append(k2)
            lhs = jnp.concatenate([kc, kc, qc, qc], axis=0)
            grams.append(lax.dot_general(lhs, k2, nt_dims, preferred_element_type=F32))
        colvs = gate_columns(units)
        a_mats, p_mats, gcols, bcols = [], [], [], []
        for n, d in chains:
            hl, ch = units[n]
            gcol, bcol = colvs[n][:, 2 * d:2 * d + 1], colvs[n][:, 2 * d + 1:2 * d + 2]
            grow = gr_ref[hl, ch][2 * d:2 * d + 1, :]
            incl = jnp.logical_and(same_head, (ii >= jj) if d == 0 else (ii <= jj))
            strict = jnp.logical_and(same_head, (ii > jj) if d == 0 else (ii < jj))
            decay = jnp.exp(jnp.where(incl, gcol - grow, -1e30))
            a_mats.append(jnp.where(strict, bcol * grams[n][:pc] * decay, 0.0))
            p_mats.append(grams[n][pc:] * decay)
            gcols.append(gcol)
            bcols.append(bcol)
        t_invs = _inv_unit_triangular_many(a_mats)
        rhss, kfs = [], []
        for (n, d), gcol, bcol in zip(chains, gcols, bcols):
            hl, ch = units[n]
            kf = k2s[n].astype(F32)
            rows = rows_of(ch)
            vp = jnp.concatenate([v_ref[rows, v_lanes(hl, 0)], v_ref[rows, v_lanes(hl, 1)]], axis=0)
            rhss.append(jnp.concatenate([(bcol * jnp.exp(gcol)) * kf, bcol * vp], axis=1).astype(BF16))
            kfs.append(kf)
        uws = [_dot(t.astype(BF16), r) for t, r in zip(t_invs, rhss)]
        kdts, x2s, egls = [], [], []
        for (n, d), gcol, kf, uw in zip(chains, gcols, kfs, uws):
            r0 = c - 1 if d == 0 else 0
            gl0 = gcol[r0:r0 + 1]
            gl1 = gcol[c + r0:c + r0 + 1]
            gl = jnp.where(top_col, gl0, gl1)
            kdts.append((kf * jnp.exp(gl - gcol)).T.astype(BF16))
            u, wv = uw[:, :dv], uw[:, dv:]
            x2s.append(jnp.concatenate([jnp.where(top, u, 0.0), jnp.where(top, 0.0, u),
                                        jnp.where(top, wv, 0.0), jnp.where(top, 0.0, wv)], axis=1).astype(BF16))
            egls.append(jnp.concatenate([jnp.broadcast_to(jnp.exp(gl0), (1, dv)),
                                         jnp.broadcast_to(jnp.exp(gl1), (1, dv))], axis=1))
        wcs = [_dot(kdt, x2) for kdt, x2 in zip(kdts, x2s)]
        for (n, d), p_mat, uw, wc, egl in zip(chains, p_mats, uws, wcs, egls):
            e = entry(*units[n], d)
            wm_ref[e] = wc[:, :2 * dv].astype(BF16)
            cm_ref[e] = wc[:, 2 * dv:]
            egl_ref[e] = egl
            p_ref[e] = p_mat.astype(BF16)
            u_ref[e] = uw[:, :dv].astype(BF16)
            wv_ref[e] = uw[:, dv:]

    def phase_b(n, carry):
        lines = [(hl, d) for hl in range(heads) for d in (0, 1)]
        es = [entry(hl, n if d == 0 else n_chunks - 1 - n, d) for hl, d in lines]
        ss = [s_ref[2 * hl + d] for hl, d in lines]
        sbs = [s.astype(BF16) for s in ss]
        for e, sb in zip(es, sbs):
            sh_ref[e] = sb
        wms = [wm_ref[e] for e in es]
        wss = [jnp.concatenate([_dot(wm[:, :dv], sb[:, :dv]), _dot(wm[:, dv:], sb[:, dv:])], axis=1)
               for wm, sb in zip(wms, sbs)]
        for (hl, d), e, s, ws in zip(lines, es, ss, wss):
            s_ref[2 * hl + d] = egl_ref[e] * s - ws + cm_ref[e]
        return carry

    def phase_c(base):
        units = [(hl, base + i) for hl in range(heads) for i in range(group)]
        chains = [(n, d) for n in range(len(units)) for d in (0, 1)]
        es = [entry(*units[n], d) for n, d in chains]
        rs = []
        for (n, d), e in zip(chains, es):
            hl, ch = units[n]
            lhs = jnp.concatenate([q_ref[rows_of(ch), qk_lanes(hl)], u_ref[e]], axis=0)
            rs.append(_dot(lhs, sh_ref[e]))
        v_news, q_ss = [], []
        for e, r in zip(es, rs):
            q_ss.append(jnp.concatenate([r[:c, :dv], r[:c, dv:]], axis=0))
            u_s = jnp.concatenate([r[c:2 * c, :dv], r[2 * c:, dv:]], axis=0)
            v_news.append((wv_ref[e] - u_s).astype(BF16))
        pvs = [_dot(p_ref[e], v_new) for e, v_new in zip(es, v_news)]
        colvs = gate_columns(units)
        outs = [jnp.exp(colvs[n][:, 2 * d:2 * d + 1]) * q_s + pv for (n, d), q_s, pv in zip(chains, q_ss, pvs)]
        for n, (hl, ch) in enumerate(units):
            o = outs[2 * n] + outs[2 * n + 1]
            rows = rows_of(ch)
            for hh in range(2):
                y = _rms(o[hh * c:(hh + 1) * c], nw_ref[...])
                og_ref[rows, v_lanes(hl, hh)] = (y * _silu(z_ref[rows, v_lanes(hl, hh)])).astype(og_ref.dtype)

    def over_groups(phase):
        n_groups = n_chunks // group
        if n_groups == 1:
            phase(0)
        else:
            def body(g, carry):
                phase(g * group)
                return carry
            lax.fori_loop(0, n_groups, body, 0)

    over_groups(phase_a)
    lax.fori_loop(0, n_chunks, phase_b, 0)
    over_groups(phase_c)

    if has_state_out:
        for hl in range(heads):
            for hh in range(2):
                sf_ref[2 * hl + hh] = s_ref[2 * hl][:, hh * dv:(hh + 1) * dv]
                sb_ref[2 * hl + hh] = s_ref[2 * hl + 1][:, hh * dv:(hh + 1) * dv]


def dn_core(qk, v, p, gates_row, norm_w, layer, *, seq_len, n_seq, row0, heads, group, init_states=None):
    n_chunks = seq_len // DN_CHUNK
    rb0 = row0 // seq_len
    qkw = heads * DN_DK
    pair = 2 * DN_DV
    vw = heads * pair
    pc = 2 * DN_CHUNK
    n_e = 2 * n_chunks * heads
    assert n_chunks % group == 0 and DN_HEADS_K % heads == 0
    z_cb0 = DN_QKV_DIM // vw
    has_init = init_states is not None
    in_specs = [
        pl.BlockSpec((seq_len, qkw), lambda s, h: (rb0 + s, h)),
        pl.BlockSpec((seq_len, qkw), lambda s, h: (rb0 + s, DN_HEADS_K // heads + h)),
        pl.BlockSpec((seq_len, vw), lambda s, h: (rb0 + s, h)),
        pl.BlockSpec((seq_len, vw), lambda s, h: (rb0 + s, z_cb0 + h)),
        pl.BlockSpec((heads, n_chunks, SUBLANES, 2 * DN_CHUNK), lambda s, h: (h, rb0 + s, 0, 0)),
        pl.BlockSpec((None, 1, DN_DV), lambda s, h: (layer, 0, 0)),
    ]
    args = [qk, qk, v, p, gates_row, norm_w]
    og_shape = jax.ShapeDtypeStruct((n_seq * seq_len, DN_V_DIM), BF16)
    og_spec = pl.BlockSpec((seq_len, vw), lambda s, h: (s, h))
    if has_init:
        st_spec = pl.BlockSpec((None, None, 2 * heads, DN_DK, DN_DV), lambda s, h: (s, layer, h, 0, 0))
        in_specs += [st_spec, st_spec]
        args += [init_states[0], init_states[1]]
        out_specs = og_spec
        out_shape = og_shape
    else:
        so_spec = pl.BlockSpec((None, None, 2 * heads, DN_DK, DN_DV), lambda s, h: (s, 0, h, 0, 0))
        so_shape = jax.ShapeDtypeStruct((n_seq, 1, DN_HEADS_V, DN_DK, DN_DV), F32)
        out_specs = [og_spec, so_spec, so_spec]
        out_shape = [og_shape, so_shape, so_shape]
    return pl.pallas_call(
        functools.partial(_dn_core_kernel, n_chunks=n_chunks, group=group, heads=heads, has_init=has_init,
                          has_state_out=not has_init),
        grid=(n_seq, DN_HEADS_K // heads),
        in_specs=in_specs,
        out_specs=out_specs,
        out_shape=out_shape,
        scratch_shapes=[
            pltpu.VMEM((n_e, DN_DK, pair), BF16),
            pltpu.VMEM((n_e, DN_DK, pair), F32),
            pltpu.VMEM((n_e, 1, pair), F32),
            pltpu.VMEM((n_e, pc, pc), BF16),
            pltpu.VMEM((n_e, pc, DN_DK), BF16),
            pltpu.VMEM((n_e, pc, DN_DV), F32),
            pltpu.VMEM((n_e, DN_DK, pair), BF16),
            pltpu.VMEM((2 * heads, DN_DK, pair), F32),
        ],
        compiler_params=_params("arbitrary", "arbitrary"),
        name="dn_core_lat" if has_init else "dn_core_ctx",
    )(*args)


def _cm_gate_kernel(u_ref, v_ref, lw_ref, lb_ref, ws_ref, bs_ref, o_ref):
    for n in range(u_ref.shape[0] // CM_CHUNK):
        rows = slice(n * CM_CHUNK, (n + 1) * CM_CHUNK)
        v = v_ref[rows, :].astype(F32)
        mu = jnp.mean(v, axis=-1, keepdims=True)
        vc = v - mu
        vn = vc * lax.rsqrt(jnp.mean(vc * vc, axis=-1, keepdims=True) + EPS) * lw_ref[...] + lb_ref[...]
        vn = vn.astype(BF16)
        for g in range(CM_GROUPS):
            lanes = slice(g * CM_GDIM, (g + 1) * CM_GDIM)
            sp = jnp.dot(ws_ref[g].astype(BF16), vn[:, lanes], preferred_element_type=F32) + bs_ref[g]
            o_ref[rows, lanes] = (u_ref[rows, lanes].astype(F32) * sp).astype(o_ref.dtype)


def cm_gate(zz, ln_w, ln_b, w_s, b_s_col, layer, chunks_per_step=2):
    m = zz.shape[0]
    rows = chunks_per_step * CM_CHUNK
    row = pl.BlockSpec((None, 1, CM_DIM), lambda i: (layer, 0, 0))
    return pl.pallas_call(
        _cm_gate_kernel,
        grid=(m // rows,),
        in_specs=[
            pl.BlockSpec((rows, CM_DIM), lambda i: (i, 0)),
            pl.BlockSpec((rows, CM_DIM), lambda i: (i, 1)),
            row,
            row,
            pl.BlockSpec((None, CM_GROUPS, CM_CHUNK, CM_CHUNK), lambda i: (layer, 0, 0, 0)),
            pl.BlockSpec((None, CM_GROUPS, CM_CHUNK, 1), lambda i: (layer, 0, 0, 0)),
        ],
        out_specs=pl.BlockSpec((rows, CM_DIM), lambda i: (i, 0)),
        out_shape=jax.ShapeDtypeStruct((m, CM_DIM), BF16),
        compiler_params=_params("arbitrary"),
        name="cm_gate",
    )(zz, zz, ln_w, ln_b, w_s, b_s_col)


def _ffn_kernel(*refs, tail):
    x_ref, nw_ref, sh_ref, sc_ref, g_ref, w1_ref, w2_ref, tw_ref = refs[:8]
    if tail == "next_h":
        tsh_ref, tsc_ref, o_ref, hn_ref, h_ref = refs[8:]
    else:
        o_ref, h_ref = refs[8:]
    kk = pl.program_id(1)

    @pl.when(kk == 0)
    def _():
        y = _rms(x_ref[...], nw_ref[...])
        h_ref[...] = (y * (1.0 + sc_ref[...]) + sh_ref[...]).astype(BF16)
        o_ref[...] = jnp.zeros_like(o_ref)

    a = jnp.dot(h_ref[...], w1_ref[...].astype(BF16), preferred_element_type=F32)
    a = jnp.square(jnp.maximum(a, 0.0)).astype(BF16)
    o_ref[...] += jnp.dot(a, w2_ref[...].astype(BF16), preferred_element_type=F32)

    @pl.when(kk == pl.num_programs(1) - 1)
    def _():
        r = x_ref[...] + g_ref[...] * o_ref[...]
        if tail == "final_norm":
            r = _rms(r, tw_ref[...])
        else:
            hn_ref[...] = (_rms(r, tw_ref[...]) * (1.0 + tsc_ref[...]) + tsh_ref[...]).astype(BF16)
        o_ref[...] = r


def ffn(x, norm_w, layer, shift, scale, gate, w1, w2, *, tail, tail_w, tail_mod=None, row0=0, n_rows=None,
        tm=1024, tk=512):
    d = x.shape[1]
    n_rows = x.shape[0] if n_rows is None else n_rows
    ff = w1.shape[2]
    t0 = row0 // tm
    mod_spec = pl.BlockSpec((None, 1, d), lambda i, k: (_group_of_row((t0 + i) * tm), 0, 0))
    row_spec = pl.BlockSpec((tm, d), lambda i, k: (i, 0), pipeline_mode=pl.Buffered(1))
    in_specs = [
        pl.BlockSpec((tm, d), lambda i, k: (t0 + i, 0), pipeline_mode=pl.Buffered(1)),
        pl.BlockSpec((None, 1, d), lambda i, k: (layer, 0, 0)),
        mod_spec,
        mod_spec,
        mod_spec,
        pl.BlockSpec((None, d, tk), lambda i, k: (layer, 0, k)),
        pl.BlockSpec((None, tk, d), lambda i, k: (layer, k, 0)),
        pl.BlockSpec((1, d), lambda i, k: (0, 0)),
    ]
    args = [x, norm_w.reshape(norm_w.shape[0], 1, d), shift, scale, gate, w1, w2, tail_w.reshape(1, d)]
    out_specs, out_shape = row_spec, jax.ShapeDtypeStruct((n_rows, d), F32)
    if tail == "next_h":
        in_specs += [mod_spec, mod_spec]
        args += list(tail_mod)
        out_specs = [row_spec, row_spec]
        out_shape = [out_shape, jax.ShapeDtypeStruct((n_rows, d), BF16)]
    return pl.pallas_call(
        functools.partial(_ffn_kernel, tail=tail),
        grid=(n_rows // tm, ff // tk),
        in_specs=in_specs,
        out_specs=out_specs,
        out_shape=out_shape,
        scratch_shapes=[pltpu.VMEM((tm, d), BF16)],
        compiler_params=_params("arbitrary", "arbitrary"),
        name="ffn_" + tail,
    )(*args)


def _grid_pos_embed(n_tokens):
    rows = n_tokens // GRID_W
    r = np.repeat(np.arange(rows), GRID_W).astype(np.float64)
    col = np.tile(np.arange(GRID_W), rows).astype(np.float64)
    quarter = D_MODEL // 4
    freq = 1.0 / (10000.0 ** (np.arange(quarter, dtype=np.float64) / quarter))
    ar = r[:, None] * freq[None, :]
    ac = col[:, None] * freq[None, :]
    return np.concatenate([np.sin(ar), np.cos(ar), np.sin(ac), np.cos(ac)], axis=-1)


def _deltanet_layer(x, h, j, gate, state_f, state_b, dn_w_in, dn_conv_w, dn_a_log, dn_dt_bias, dn_norm_w, dn_w_out):
    n_main = DN_QKV_DIM + DN_V_DIM
    p = matmul_ws(h, dn_w_in, j, n_main, tm=1024, tn=1024, epilogue=_ep_plain, name="dn_in")
    ab = matmul_ws(h, dn_w_in, j, DN_GATE_COLS, tm=1024, tn=DN_GATE_COLS, epilogue=_ep_plain,
                   col_block0=n_main // DN_GATE_COLS, name="dn_in_gates")
    qk = conv_silu(p, dn_conv_w, j, 0, 2 * DN_K_DIM, normalize=True, out_dtype=BF16)
    v = conv_silu(p, dn_conv_w, j, 2 * DN_K_DIM, DN_V_DIM, normalize=False, out_dtype=F32)

    zeros = jnp.zeros((DN_HEADS_V,), F32)
    a_log_row = jnp.concatenate([dn_a_log[j, 0], zeros, dn_a_log[j, 1], zeros])[None, :]
    dt_row = jnp.concatenate([dn_dt_bias[j, 0], zeros, dn_dt_bias[j, 1], zeros])[None, :]
    gates = dn_gates(ab, a_log_row, dt_row)
    m = gates.shape[0]
    g5 = gates.reshape(m // DN_CHUNK, DN_CHUNK, 4, DN_HEADS_K, 2)
    gates_row = g5.transpose(3, 0, 2, 4, 1).reshape(DN_HEADS_K, m // DN_CHUNK, 4, 2 * DN_CHUNK)
    gates_row = jnp.pad(gates_row, ((0, 0), (0, 0), (0, SUBLANES - 4), (0, 0)))
    norm_w = dn_norm_w.reshape(dn_norm_w.shape[0], 1, DN_DV)

    og_ctx, s_f, s_b = dn_core(qk, v, p, gates_row, norm_w, j, seq_len=SEQ, n_seq=BATCH, row0=0,
                               heads=4, group=2)
    og_lat = dn_core(qk, v, p, gates_row, norm_w, j, seq_len=DEC_SEQ, n_seq=DEC_BATCH, row0=N_CTX_TOK,
                     heads=2, group=4, init_states=(state_f, state_b))
    x = matmul_resid_gate([og_ctx, og_lat], dn_w_out, j, x, gate, tm=512, tn=512, name="dn_out")
    return x, s_f, s_b


def _chunk_mlp_layer(x, h, j, gate, cm_w_in, cm_b_in, cm_ln_w, cm_ln_b, cm_w_s, cm_b_s, cm_w_out):
    n_in = 2 * CM_DIM
    tn = 1024
    bias = (cm_b_in.reshape(cm_b_in.shape[0], 1, n_in), pl.BlockSpec((None, 1, tn), lambda jj, i: (j, 0, jj)))
    zz = matmul_ws(h, cm_w_in, j, n_in, tm=1024, tn=tn, epilogue=_ep_bias_gelu, extras=(bias,), out_dtype=BF16,
                   name="cm_in")
    n_b = cm_ln_w.shape[0]
    uv = cm_gate(zz, cm_ln_w.reshape(n_b, 1, CM_DIM), cm_ln_b.reshape(n_b, 1, CM_DIM), cm_w_s,
                 cm_b_s[..., None], j)
    return matmul_resid_gate(uv, cm_w_out, j, x, gate, tm=1024, tn=512, name="cm_out")


def kernel(x_prompt, x_sample, state_dn_fwd, state_dn_bwd, c, c_ctx, norm_mix_w, norm_mlp_w, w_ada, b_ada, dn_w_in, dn_conv_w, dn_A_log, dn_dt_bias, dn_norm_w, dn_w_out, cm_w_in, cm_b_in, cm_ln_w, cm_ln_b, cm_w_s, cm_b_s, cm_w_out, w_ff1, w_ff2, final_norm_w):
    cond = jnp.concatenate([c_ctx[None, :], c, jnp.zeros((N_COND - 1 - DEC_BATCH, D_MODEL), F32)], axis=0)
    mod = adaln_all(cond, w_ada, b_ada)
    mod = mod.reshape(DEPTH, N_COND, N_MOD, D_MODEL).transpose(0, 2, 1, 3)[:, :, :, None, :]
    mods = [[mod[i, t] for t in range(N_MOD)] for i in range(DEPTH)]

    x, h = embed_norm_modulate(x_prompt.reshape(N_CTX_TOK, D_MODEL), x_sample.reshape(N_LAT_TOK, D_MODEL),
                               jnp.asarray(_grid_pos_embed(DEC_SEQ), x_sample.dtype), norm_mix_w, 0,
                               mods[0][0], mods[0][1])
    new_fwd, new_bwd = [], []
    for i in range(DEPTH):
        j = i // N_MIXERS
        _, _, gate, shift2, scale2, gate2 = mods[i]
        if i % N_MIXERS == 0:
            x, s_f, s_b = _deltanet_layer(x, h, j, gate, state_dn_fwd, state_dn_bwd, dn_w_in, dn_conv_w,
                                          dn_A_log, dn_dt_bias, dn_norm_w, dn_w_out)
            new_fwd.append(s_f)
            new_bwd.append(s_b)
        else:
            x = _chunk_mlp_layer(x, h, j, gate, cm_w_in, cm_b_in, cm_ln_w, cm_ln_b, cm_w_s, cm_b_s, cm_w_out)
        mlp = functools.partial(ffn, x, norm_mlp_w, i, shift2, scale2, gate2, w_ff1, w_ff2)
        if i < DEPTH - 1:
            x, h = mlp(tail="next_h", tail_w=norm_mix_w[i + 1], tail_mod=mods[i + 1][:2])
        else:
            y_ctx = mlp(tail="final_norm", tail_w=final_norm_w, row0=0, n_rows=N_CTX_TOK)
            y_lat = mlp(tail="final_norm", tail_w=final_norm_w, row0=N_CTX_TOK, n_rows=N_LAT_TOK)

    y_prompt = y_ctx.reshape(BATCH, SEQ, D_MODEL)
    y_sample = y_lat.reshape(DEC_BATCH, DEC_SEQ, D_MODEL)
    return (y_prompt, y_sample, jnp.concatenate(new_fwd, axis=1), jnp.concatenate(new_bwd, axis=1))
```

```python
import functools
import math

import jax
import jax.numpy as jnp
import numpy as np
from jax import lax
from jax.experimental import pallas as pl
from jax.experimental.pallas import tpu as pltpu

F32 = jnp.float32
BF16 = jnp.bfloat16

D_MODEL = 2048
BATCH = 16
SEQ = 256
DEPTH = 2
DEC_BATCH = 2
DEC_SEQ = 1024
GRID_W = 64
N_MIXERS = 2
DN_DK = 128
DN_DV = 128
DN_HEADS_K = D_MODEL // DN_DK
DN_HEADS_V = 2 * DN_HEADS_K
DN_K_DIM = DN_HEADS_K * DN_DK
DN_V_DIM = DN_HEADS_V * DN_DV
DN_QKV_DIM = 2 * DN_K_DIM + DN_V_DIM
DN_GATE_COLS = 4 * DN_HEADS_V
DN_CONV = 5
DN_CHUNK = 64
CM_DIM = 2 * D_MODEL
CM_CHUNK = 128
CM_GROUPS = 16
CM_GDIM = CM_DIM // CM_GROUPS
FF_DIM = 4 * D_MODEL
N_MOD = 6
EPS = 1e-6

N_CTX_TOK = BATCH * SEQ
N_LAT_TOK = DEC_BATCH * DEC_SEQ
N_TOK = N_CTX_TOK + N_LAT_TOK
N_COND = 8
SUBLANES = 8
CONV_HALO = SUBLANES
VMEM_LIMIT_BYTES = 56 * 1024 * 1024


def _group_of_row(row0):
    return jnp.where(row0 < N_CTX_TOK, 0, 1 + (row0 - N_CTX_TOK) // DEC_SEQ)


def _params(*sem):
    return pltpu.CompilerParams(dimension_semantics=sem, vmem_limit_bytes=VMEM_LIMIT_BYTES)


def _rms(x, w):
    return x * lax.rsqrt(jnp.mean(x * x, axis=-1, keepdims=True) + EPS) * w


def _silu(x):
    half = 0.5 * x
    return half + half * jnp.tanh(half)


def _gelu_tanh(x):
    return 0.5 * x * (1.0 + jnp.tanh(math.sqrt(2.0 / math.pi) * (x + 0.044715 * (x * x * x))))


def _adaln_kernel(c_ref, w_ref, b_ref, o_ref):
    x = _silu(c_ref[...]).astype(BF16)
    acc = jnp.dot(x, w_ref[...].astype(BF16), preferred_element_type=F32)
    o_ref[...] = acc + b_ref[...]


def adaln_all(cond, w_ada, b_ada, tn=1024):
    depth, d, n = w_ada.shape
    return pl.pallas_call(
        _adaln_kernel,
        grid=(depth, n // tn),
        in_specs=[
            pl.BlockSpec((N_COND, d), lambda l, j: (0, 0)),
            pl.BlockSpec((None, d, tn), lambda l, j: (l, 0, j)),
            pl.BlockSpec((None, 1, tn), lambda l, j: (l, 0, j)),
        ],
        out_specs=pl.BlockSpec((None, N_COND, tn), lambda l, j: (l, 0, j)),
        out_shape=jax.ShapeDtypeStruct((depth, N_COND, n), F32),
        compiler_params=_params("arbitrary", "arbitrary"),
        name="adaln",
    )(cond, w_ada, b_ada.reshape(depth, 1, n))


def _embed_kernel(xc_ref, xl_ref, pe_ref, w_ref, sh_ref, sc_ref, x_ref, h_ref, *, ctx_tiles):
    def emit(x):
        x_ref[...] = x
        h_ref[...] = (_rms(x, w_ref[...]) * (1.0 + sc_ref[...]) + sh_ref[...]).astype(BF16)

    i = pl.program_id(0)
    pl.when(i < ctx_tiles)(lambda: emit(xc_ref[...]))
    pl.when(i >= ctx_tiles)(lambda: emit(xl_ref[...] + pe_ref[...]))


def embed_norm_modulate(x_ctx, x_lat, pos_table, norm_w, layer, shift, scale, tm=512):
    d = x_ctx.shape[1]
    ctx_tiles = x_ctx.shape[0] // tm
    lat_tiles = x_lat.shape[0] // tm
    pos_tiles = pos_table.shape[0] // tm
    m = x_ctx.shape[0] + x_lat.shape[0]
    mod_spec = pl.BlockSpec((None, 1, d), lambda i: (_group_of_row(i * tm), 0, 0))
    row_spec = pl.BlockSpec((tm, d), lambda i: (i, 0))
    return pl.pallas_call(
        functools.partial(_embed_kernel, ctx_tiles=ctx_tiles),
        grid=(ctx_tiles + lat_tiles,),
        in_specs=[
            pl.BlockSpec((tm, d), lambda i: (jnp.minimum(i, ctx_tiles - 1), 0)),
            pl.BlockSpec((tm, d), lambda i: (jnp.clip(i - ctx_tiles, 0, lat_tiles - 1), 0)),
            pl.BlockSpec((tm, d), lambda i: (jnp.maximum(i - ctx_tiles, 0) % pos_tiles, 0)),
            pl.BlockSpec((None, 1, d), lambda i: (layer, 0, 0)),
            mod_spec,
            mod_spec,
        ],
        out_specs=[row_spec, row_spec],
        out_shape=[jax.ShapeDtypeStruct((m, d), F32), jax.ShapeDtypeStruct((m, d), BF16)],
        compiler_params=_params("arbitrary"),
        name="embed_norm_modulate",
    )(x_ctx, x_lat, pos_table, norm_w.reshape(norm_w.shape[0], 1, d), shift, scale)


def _mm_kernel(*refs, part_tiles, n_extra, epilogue, row_pieces):
    n_parts = len(part_tiles)
    x_refs = refs[:n_parts]
    w_ref = refs[n_parts]
    extra = refs[n_parts + 1:n_parts + 1 + n_extra]
    o_ref = refs[n_parts + 1 + n_extra]
    wb_ref = refs[n_parts + 2 + n_extra]
    i = pl.program_id(1)

    @pl.when(i == 0)
    def _():
        wb_ref[...] = w_ref[...].astype(BF16)

    def run(x_ref):
        extra_vals = [r[...] for r in extra]
        if row_pieces == 1:
            epilogue(jnp.dot(x_ref[...], wb_ref[...], preferred_element_type=F32), o_ref, *extra_vals)
            return
        rows = x_ref.shape[0] // row_pieces
        accs = []
        for s in range(row_pieces):
            accs.append(jnp.dot(x_ref[s * rows:(s + 1) * rows, :], wb_ref[...], preferred_element_type=F32))
            if s >= 1:
                epilogue(accs, s - 1, o_ref, *extra_vals)
        epilogue(accs, row_pieces - 1, o_ref, *extra_vals)

    if n_parts == 1:
        run(x_refs[0])
    else:
        first = 0
        for x_ref, tiles in zip(x_refs, part_tiles):
            pl.when(jnp.logical_and(i >= first, i < first + tiles))(functools.partial(run, x_ref))
            first += tiles


def matmul_ws(x, w, layer, n_out, *, tm, tn, epilogue, extras=(), out_dtype=F32, col_block0=0, row_pieces=1,
              name="matmul"):
    parts = list(x) if isinstance(x, (list, tuple)) else [x]
    k = parts[0].shape[1]
    part_tiles = [part.shape[0] // tm for part in parts]
    m = sum(part.shape[0] for part in parts)
    in_specs = []
    first = 0
    for tiles in part_tiles:
        in_specs.append(pl.BlockSpec(
            (tm, k), lambda j, i, first=first, tiles=tiles: (jnp.clip(i - first, 0, tiles - 1), 0)))
        first += tiles
    in_specs.append(pl.BlockSpec((None, k, tn), lambda j, i: (layer, 0, j + col_block0)))
    in_specs += [spec for _, spec in extras]
    return pl.pallas_call(
        functools.partial(_mm_kernel, part_tiles=tuple(part_tiles), n_extra=len(extras), epilogue=epilogue,
                          row_pieces=row_pieces),
        grid=(n_out // tn, m // tm),
        in_specs=in_specs,
        out_specs=pl.BlockSpec((tm, tn), lambda j, i: (i, j)),
        out_shape=jax.ShapeDtypeStruct((m, n_out), out_dtype),
        scratch_shapes=[pltpu.VMEM((k, tn), BF16)],
        compiler_params=_params("arbitrary", "arbitrary"),
        name=name,
    )(*parts, w, *[a for a, _ in extras])


def _ep_plain(acc, o_ref):
    o_ref[...] = acc.astype(o_ref.dtype)


def _ep_bias_gelu(acc, o_ref, b):
    o_ref[...] = _gelu_tanh(acc + b).astype(o_ref.dtype)


def _ep_resid_gate(acc, o_ref, resid, gate):
    o_ref[...] = (resid + gate * acc).astype(o_ref.dtype)


def _ep_conv_silu(accs, s, o_ref, w, *, normalize, n_q_tiles):
    pieces = o_ref.shape[0] // SEQ
    tn = o_ref.shape[1]
    assert o_ref.shape[0] == DEC_SEQ and accs[s].shape[0] == SEQ
    every_piece_is_a_sequence = pl.program_id(1) * o_ref.shape[0] < N_CTX_TOK
    pad = DN_CONV // 2
    n_ext = SEQ + 2 * CONV_HALO
    zeros = jnp.zeros((CONV_HALO, tn), F32)
    prev = zeros if s == 0 else jnp.where(every_piece_is_a_sequence, 0.0, accs[s - 1][SEQ - CONV_HALO:])
    nxt = zeros if s == pieces - 1 else jnp.where(every_piece_is_a_sequence, 0.0, accs[s + 1][:CONV_HALO])
    ext = jnp.concatenate([prev, accs[s], nxt], axis=0)
    y = None
    for t in range(DN_CONV):
        shifted = ext if t == pad else pltpu.roll(ext, (pad - t) % n_ext, 0)
        term = shifted[CONV_HALO:CONV_HALO + SEQ] * w[t:t + 1, :]
        y = term if y is None else y + term
    y = _silu(y)
    rows = slice(s * SEQ, (s + 1) * SEQ)
    if normalize:
        scale = jnp.where(pl.program_id(0) < n_q_tiles, DN_DK ** -0.5, 1.0)
        for hh in range(tn // DN_DK):
            lanes = slice(hh * DN_DK, (hh + 1) * DN_DK)
            sl = y[:, lanes]
            inv = lax.rsqrt(jnp.sum(sl * sl, axis=-1, keepdims=True) + EPS) * scale
            o_ref[rows, lanes] = (sl * inv).astype(o_ref.dtype)
    else:
        o_ref[rows, :] = y.astype(o_ref.dtype)


def matmul_resid_gate(x, w, layer, resid, gate, *, tm, tn, name):
    n_out = resid.shape[1]
    extras = (
        (resid, pl.BlockSpec((tm, tn), lambda j, i: (i, j))),
        (gate, pl.BlockSpec((None, 1, tn), lambda j, i: (_group_of_row(i * tm), 0, j))),
    )
    return matmul_ws(x, w, layer, n_out, tm=tm, tn=tn, epilogue=_ep_resid_gate, extras=extras, name=name)


def _split3(x):
    hi = x.astype(BF16)
    r1 = x - hi.astype(F32)
    mid = r1.astype(BF16)
    lo = (r1 - mid.astype(F32)).astype(BF16)
    return hi, mid, lo


def _gates_kernel(ab_ref, alog_ref, dtb_ref, o_ref):
    c = DN_CHUNK
    ii = lax.broadcasted_iota(jnp.int32, (c, c), 0)
    jj = lax.broadcasted_iota(jnp.int32, (c, c), 1)
    tril = (ii >= jj).astype(BF16)
    triu = (ii <= jj).astype(BF16)
    lane = lax.broadcasted_iota(jnp.int32, (c, ab_ref.shape[1]), 1)
    kind = lane // DN_HEADS_V
    for n in range(ab_ref.shape[0] // c):
        ab = ab_ref[n * c:(n + 1) * c, :]
        log_g = -jnp.exp(alog_ref[...]) * jax.nn.softplus(ab + dtb_ref[...])
        beta = jax.nn.sigmoid(ab)
        parts = _split3(log_g)
        cum_f = sum(jnp.dot(tril, part, preferred_element_type=F32) for part in parts)
        cum_b = sum(jnp.dot(triu, part, preferred_element_type=F32) for part in parts)
        o_ref[n * c:(n + 1) * c, :] = jnp.where(kind == 0, cum_f, jnp.where(kind == 2, cum_b, beta))


def dn_gates(ab, a_log_row, dt_bias_row, chunks_per_step=8):
    m, n = ab.shape
    rows = chunks_per_step * DN_CHUNK
    return pl.pallas_call(
        _gates_kernel,
        grid=(m // rows,),
        in_specs=[
            pl.BlockSpec((rows, n), lambda i: (i, 0)),
            pl.BlockSpec((1, n), lambda i: (0, 0)),
            pl.BlockSpec((1, n), lambda i: (0, 0)),
        ],
        out_specs=pl.BlockSpec((rows, n), lambda i: (i, 0)),
        out_shape=jax.ShapeDtypeStruct((m, n), F32),
        compiler_params=_params("arbitrary"),
        name="dn_gates",
    )(ab, a_log_row, dt_bias_row)


def _dot(a, b):
    return jnp.dot(a, b, preferred_element_type=F32)


def _inv_unit_triangular_many(a_list):
    n = a_list[0].shape[0]
    ii = lax.broadcasted_iota(jnp.int32, (n, n), 0)
    jj = lax.broadcasted_iota(jnp.int32, (n, n), 1)
    eye = jnp.where(ii == jj, 1.0, 0.0)
    xs = [eye - jnp.where((ii >> 1) == (jj >> 1), a, 0.0) for a in a_list]
    for level in range(1, int(math.log2(DN_CHUNK))):
        joins = jnp.logical_and((ii >> (level + 1)) == (jj >> (level + 1)), (ii >> level) != (jj >> level))
        ns = [jnp.where(joins, a, 0.0).astype(BF16) for a in a_list]
        xbs = [x.astype(BF16) for x in xs]
        ys = [_dot(xb, nn).astype(BF16) for xb, nn in zip(xbs, ns)]
        xs = [x - _dot(y, xb) for x, y, xb in zip(xs, ys, xbs)]
    return xs


def _dn_core_kernel(*refs, n_chunks, group, heads, has_init, has_state_out):
    q_ref, k_ref, v_ref, z_ref, gr_ref, nw_ref = refs[:6]
    pos = 6
    if has_init:
        s0f_ref, s0b_ref = refs[pos:pos + 2]
        pos += 2
    og_ref = refs[pos]
    pos += 1
    if has_state_out:
        sf_ref, sb_ref = refs[pos:pos + 2]
        pos += 2
    wm_ref, cm_ref, egl_ref, p_ref, u_ref, wv_ref, sh_ref, s_ref = refs[pos:pos + 8]

    c = DN_CHUNK
    pc = 2 * c
    dv = DN_DV
    if has_init:
        for hl in range(heads):
            s_ref[2 * hl] = jnp.concatenate([s0f_ref[2 * hl], s0f_ref[2 * hl + 1]], axis=1)
            s_ref[2 * hl + 1] = jnp.concatenate([s0b_ref[2 * hl], s0b_ref[2 * hl + 1]], axis=1)
    else:
        s_ref[...] = jnp.zeros_like(s_ref)

    ii = lax.broadcasted_iota(jnp.int32, (pc, pc), 0)
    jj = lax.broadcasted_iota(jnp.int32, (pc, pc), 1)
    chunk_shift = int(math.log2(c))
    same_head = (ii >> chunk_shift) == (jj >> chunk_shift)
    top = ii < c
    top_col = lax.broadcasted_iota(jnp.int32, (pc, 1), 0) < c
    nt_dims = (((1,), (1,)), ((), ()))

    def rows_of(chunk):
        start = chunk * c
        return pl.ds(start if isinstance(start, int) else pl.multiple_of(start, c), c)

    def gate_columns(units):
        return [gr_ref[hl, ch].T for hl, ch in units]

    def entry(hl, chunk, d):
        return (hl * n_chunks + chunk) * 2 + d

    def qk_lanes(hl):
        return slice(hl * DN_DK, (hl + 1) * DN_DK)

    def v_lanes(hl, hh):
        return slice((2 * hl + hh) * dv, (2 * hl + hh + 1) * dv)

    def phase_a(base):
        units = [(hl, base + i) for hl in range(heads) for i in range(group)]
        chains = [(n, d) for n in range(len(units)) for d in (0, 1)]
        k2s, grams = [], []
        for hl, ch in units:
            kc = k_ref[rows_of(ch), qk_lanes(hl)]
            qc = q_ref[rows_of(ch), qk_lanes(hl)]
            k2 = jnp.concatenate([kc, kc], axis=0)
            k2s.append(k2)
            lhs = jnp.concatenate([kc, kc, qc, qc], axis=0)
            grams.append(lax.dot_general(lhs, k2, nt_dims, preferred_element_type=F32))
        colvs = gate_columns(units)
        a_mats, p_mats, gcols, bcols = [], [], [], []
        for n, d in chains:
            hl, ch = units[n]
            gcol, bcol = colvs[n][:, 2 * d:2 * d + 1], colvs[n][:, 2 * d + 1:2 * d + 2]
            grow = gr_ref[hl, ch][2 * d:2 * d + 1, :]
            incl = jnp.logical_and(same_head, (ii >= jj) if d == 0 else (ii <= jj))
            strict = jnp.logical_and(same_head, (ii > jj) if d == 0 else (ii < jj))
            decay = jnp.exp(jnp.where(incl, gcol - grow, -1e30))
            a_mats.append(jnp.where(strict, bcol * grams[n][:pc] * decay, 0.0))
            p_mats.append(grams[n][pc:] * decay)
            gcols.append(gcol)
            bcols.append(bcol)
        t_invs = _inv_unit_triangular_many(a_mats)
        rhss, kfs = [], []
        for (n, d), gcol, bcol in zip(chains, gcols, bcols):
            hl, ch = units[n]
            kf = k2s[n].astype(F32)
            rows = rows_of(ch)
            vp = jnp.concatenate([v_ref[rows, v_lanes(hl, 0)], v_ref[rows, v_lanes(hl, 1)]], axis=0)
            rhss.append(jnp.concatenate([(bcol * jnp.exp(gcol)) * kf, bcol * vp], axis=1).astype(BF16))
            kfs.append(kf)
        uws = [_dot(t.astype(BF16), r) for t, r in zip(t_invs, rhss)]
        kdts, x2s, egls = [], [], []
        for (n, d), gcol, kf, uw in zip(chains, gcols, kfs, uws):
            r0 = c - 1 if d == 0 else 0
            gl0 = gcol[r0:r0 + 1]
            gl1 = gcol[c + r0:c + r0 + 1]
            gl = jnp.where(top_col, gl0, gl1)
            kdts.append((kf * jnp.exp(gl - gcol)).T.astype(BF16))
            u, wv = uw[:, :dv], uw[:, dv:]
            x2s.append(jnp.concatenate([jnp.where(top, u, 0.0), jnp.where(top, 0.0, u),
                                        jnp.where(top, wv, 0.0), jnp.where(top, 0.0, wv)], axis=1).astype(BF16))
            egls.append(jnp.concatenate([jnp.broadcast_to(jnp.exp(gl0), (1, dv)),
                                         jnp.broadcast_to(jnp.exp(gl1), (1, dv))], axis=1))
        wcs = [_dot(kdt, x2) for kdt, x2 in zip(kdts, x2s)]
        for (n, d), p_mat, uw, wc, egl in zip(chains, p_mats, uws, wcs, egls):
            e = entry(*units[n], d)
            wm_ref[e] = wc[:, :2 * dv].astype(BF16)
            cm_ref[e] = wc[:, 2 * dv:]
            egl_ref[e] = egl
            p_ref[e] = p_mat.astype(BF16)
            u_ref[e] = uw[:, :dv].astype(BF16)
            wv_ref[e] = uw[:, dv:]

    def phase_b(n, carry):
        lines = [(hl, d) for hl in range(heads) for d in (0, 1)]
        es = [entry(hl, n if d == 0 else n_chunks - 1 - n, d) for hl, d in lines]
        ss = [s_ref[2 * hl + d] for hl, d in lines]
        sbs = [s.astype(BF16) for s in ss]
        for e, sb in zip(es, sbs):
            sh_ref[e] = sb
        wms = [wm_ref[e] for e in es]
        wss = [jnp.concatenate([_dot(wm[:, :dv], sb[:, :dv]), _dot(wm[:, dv:], sb[:, dv:])], axis=1)
               for wm, sb in zip(wms, sbs)]
        for (hl, d), e, s, ws in zip(lines, es, ss, wss):
            s_ref[2 * hl + d] = egl_ref[e] * s - ws + cm_ref[e]
        return carry

    def phase_c(base):
        units = [(hl, base + i) for hl in range(heads) for i in range(group)]
        chains = [(n, d) for n in range(len(units)) for d in (0, 1)]
        es = [entry(*units[n], d) for n, d in chains]
        rs = []
        for (n, d), e in zip(chains, es):
            hl, ch = units[n]
            lhs = jnp.concatenate([q_ref[rows_of(ch), qk_lanes(hl)], u_ref[e]], axis=0)
            rs.append(_dot(lhs, sh_ref[e]))
        v_news, q_ss = [], []
        for e, r in zip(es, rs):
            q_ss.append(jnp.concatenate([r[:c, :dv], r[:c, dv:]], axis=0))
            u_s = jnp.concatenate([r[c:2 * c, :dv], r[2 * c:, dv:]], axis=0)
            v_news.append((wv_ref[e] - u_s).astype(BF16))
        pvs = [_dot(p_ref[e], v_new) for e, v_new in zip(es, v_news)]
        colvs = gate_columns(units)
        outs = [jnp.exp(colvs[n][:, 2 * d:2 * d + 1]) * q_s + pv for (n, d), q_s, pv in zip(chains, q_ss, pvs)]
        for n, (hl, ch) in enumerate(units):
            o = outs[2 * n] + outs[2 * n + 1]
            rows = rows_of(ch)
            for hh in range(2):
                y = _rms(o[hh * c:(hh + 1) * c], nw_ref[...])
                og_ref[rows, v_lanes(hl, hh)] = (y * _silu(z_ref[rows, v_lanes(hl, hh)])).astype(og_ref.dtype)

    def over_groups(phase):
        n_groups = n_chunks // group
        if n_groups == 1:
            phase(0)
        else:
            def body(g, carry):
                phase(g * group)
                return carry
            lax.fori_loop(0, n_groups, body, 0)

    over_groups(phase_a)
    lax.fori_loop(0, n_chunks, phase_b, 0)
    over_groups(phase_c)

    if has_state_out:
        for hl in range(heads):
            for hh in range(2):
                sf_ref[2 * hl + hh] = s_ref[2 * hl][:, hh * dv:(hh + 1) * dv]
                sb_ref[2 * hl + hh] = s_ref[2 * hl + 1][:, hh * dv:(hh + 1) * dv]


def dn_core(qk, v, z, gates_row, norm_w, layer, *, seq_len, n_seq, row0, heads, group, init_states=None):
    n_chunks = seq_len // DN_CHUNK
    rb0 = row0 // seq_len
    qkw = heads * DN_DK
    pair = 2 * DN_DV
    vw = heads * pair
    pc = 2 * DN_CHUNK
    n_e = 2 * n_chunks * heads
    assert n_chunks % group == 0 and DN_HEADS_K % heads == 0
    has_init = init_states is not None
    in_specs = [
        pl.BlockSpec((seq_len, qkw), lambda s, h: (rb0 + s, h)),
        pl.BlockSpec((seq_len, qkw), lambda s, h: (rb0 + s, DN_HEADS_K // heads + h)),
        pl.BlockSpec((seq_len, vw), lambda s, h: (rb0 + s, h)),
        pl.BlockSpec((seq_len, vw), lambda s, h: (rb0 + s, h)),
        pl.BlockSpec((heads, n_chunks, SUBLANES, 2 * DN_CHUNK), lambda s, h: (h, rb0 + s, 0, 0)),
        pl.BlockSpec((None, 1, DN_DV), lambda s, h: (layer, 0, 0)),
    ]
    args = [qk, qk, v, z, gates_row, norm_w]
    og_shape = jax.ShapeDtypeStruct((n_seq * seq_len, DN_V_DIM), BF16)
    og_spec = pl.BlockSpec((seq_len, vw), lambda s, h: (s, h))
    if has_init:
        st_spec = pl.BlockSpec((None, None, 2 * heads, DN_DK, DN_DV), lambda s, h: (s, layer, h, 0, 0))
        in_specs += [st_spec, st_spec]
        args += [init_states[0], init_states[1]]
        out_specs = og_spec
        out_shape = og_shape
    else:
        so_spec = pl.BlockSpec((None, None, 2 * heads, DN_DK, DN_DV), lambda s, h: (s, 0, h, 0, 0))
        so_shape = jax.ShapeDtypeStruct((n_seq, 1, DN_HEADS_V, DN_DK, DN_DV), F32)
        out_specs = [og_spec, so_spec, so_spec]
        out_shape = [og_shape, so_shape, so_shape]
    return pl.pallas_call(
        functools.partial(_dn_core_kernel, n_chunks=n_chunks, group=group, heads=heads, has_init=has_init,
                          has_state_out=not has_init),
        grid=(n_seq, DN_HEADS_K // heads),
        in_specs=in_specs,
        out_specs=out_specs,
        out_shape=out_shape,
        scratch_shapes=[
            pltpu.VMEM((n_e, DN_DK, pair), BF16),
            pltpu.VMEM((n_e, DN_DK, pair), F32),
            pltpu.VMEM((n_e, 1, pair), F32),
            pltpu.VMEM((n_e, pc, pc), BF16),
            pltpu.VMEM((n_e, pc, DN_DK), BF16),
            pltpu.VMEM((n_e, pc, DN_DV), F32),
            pltpu.VMEM((n_e, DN_DK, pair), BF16),
            pltpu.VMEM((2 * heads, DN_DK, pair), F32),
        ],
        compiler_params=_params("arbitrary", "arbitrary"),
        name="dn_core_lat" if has_init else "dn_core_ctx",
    )(*args)


def _cm_gate_kernel(u_ref, v_ref, lw_ref, lb_ref, ws_ref, bs_ref, o_ref):
    for n in range(u_ref.shape[0] // CM_CHUNK):
        rows = slice(n * CM_CHUNK, (n + 1) * CM_CHUNK)
        v = v_ref[rows, :].astype(F32)
        mu = jnp.mean(v, axis=-1, keepdims=True)
        vc = v - mu
        vn = vc * lax.rsqrt(jnp.mean(vc * vc, axis=-1, keepdims=True) + EPS) * lw_ref[...] + lb_ref[...]
        vn = vn.astype(BF16)
        for g in range(CM_GROUPS):
            lanes = slice(g * CM_GDIM, (g + 1) * CM_GDIM)
            sp = jnp.dot(ws_ref[g].astype(BF16), vn[:, lanes], preferred_element_type=F32) + bs_ref[g]
            o_ref[rows, lanes] = (u_ref[rows, lanes].astype(F32) * sp).astype(o_ref.dtype)


def cm_gate(zz, ln_w, ln_b, w_s, b_s_col, layer, chunks_per_step=2):
    m = zz.shape[0]
    rows = chunks_per_step * CM_CHUNK
    row = pl.BlockSpec((None, 1, CM_DIM), lambda i: (layer, 0, 0))
    return pl.pallas_call(
        _cm_gate_kernel,
        grid=(m // rows,),
        in_specs=[
            pl.BlockSpec((rows, CM_DIM), lambda i: (i, 0)),
            pl.BlockSpec((rows, CM_DIM), lambda i: (i, 1)),
            row,
            row,
            pl.BlockSpec((None, CM_GROUPS, CM_CHUNK, CM_CHUNK), lambda i: (layer, 0, 0, 0)),
            pl.BlockSpec((None, CM_GROUPS, CM_CHUNK, 1), lambda i: (layer, 0, 0, 0)),
        ],
        out_specs=pl.BlockSpec((rows, CM_DIM), lambda i: (i, 0)),
        out_shape=jax.ShapeDtypeStruct((m, CM_DIM), BF16),
        compiler_params=_params("arbitrary"),
        name="cm_gate",
    )(zz, zz, ln_w, ln_b, w_s, b_s_col)


def _ffn_kernel(*refs, tail):
    x_ref, nw_ref, sh_ref, sc_ref, g_ref, w1_ref, w2_ref, tw_ref = refs[:8]
    if tail == "next_h":
        tsh_ref, tsc_ref, o_ref, hn_ref, h_ref = refs[8:]
    else:
        o_ref, h_ref = refs[8:]
    kk = pl.program_id(1)

    @pl.when(kk == 0)
    def _():
        y = _rms(x_ref[...], nw_ref[...])
        h_ref[...] = (y * (1.0 + sc_ref[...]) + sh_ref[...]).astype(BF16)
        o_ref[...] = jnp.zeros_like(o_ref)

    a = jnp.dot(h_ref[...], w1_ref[...].astype(BF16), preferred_element_type=F32)
    a = jnp.square(jnp.maximum(a, 0.0)).astype(BF16)
    o_ref[...] += jnp.dot(a, w2_ref[...].astype(BF16), preferred_element_type=F32)

    @pl.when(kk == pl.num_programs(1) - 1)
    def _():
        r = x_ref[...] + g_ref[...] * o_ref[...]
        if tail == "final_norm":
            r = _rms(r, tw_ref[...])
        else:
            hn_ref[...] = (_rms(r, tw_ref[...]) * (1.0 + tsc_ref[...]) + tsh_ref[...]).astype(BF16)
        o_ref[...] = r


def ffn(x, norm_w, layer, shift, scale, gate, w1, w2, *, tail, tail_w, tail_mod=None, row0=0, n_rows=None,
        tm=1024, tk=512):
    d = x.shape[1]
    n_rows = x.shape[0] if n_rows is None else n_rows
    ff = w1.shape[2]
    t0 = row0 // tm
    mod_spec = pl.BlockSpec((None, 1, d), lambda i, k: (_group_of_row((t0 + i) * tm), 0, 0))
    row_spec = pl.BlockSpec((tm, d), lambda i, k: (i, 0), pipeline_mode=pl.Buffered(1))
    in_specs = [
        pl.BlockSpec((tm, d), lambda i, k: (t0 + i, 0), pipeline_mode=pl.Buffered(1)),
        pl.BlockSpec((None, 1, d), lambda i, k: (layer, 0, 0)),
        mod_spec,
        mod_spec,
        mod_spec,
        pl.BlockSpec((None, d, tk), lambda i, k: (layer, 0, k)),
        pl.BlockSpec((None, tk, d), lambda i, k: (layer, k, 0)),
        pl.BlockSpec((1, d), lambda i, k: (0, 0)),
    ]
    args = [x, norm_w.reshape(norm_w.shape[0], 1, d), shift, scale, gate, w1, w2, tail_w.reshape(1, d)]
    out_specs, out_shape = row_spec, jax.ShapeDtypeStruct((n_rows, d), F32)
    if tail == "next_h":
        in_specs += [mod_spec, mod_spec]
        args += list(tail_mod)
        out_specs = [row_spec, row_spec]
        out_shape = [out_shape, jax.ShapeDtypeStruct((n_rows, d), BF16)]
    return pl.pallas_call(
        functools.partial(_ffn_kernel, tail=tail),
        grid=(n_rows // tm, ff // tk),
        in_specs=in_specs,
        out_specs=out_specs,
        out_shape=out_shape,
        scratch_shapes=[pltpu.VMEM((tm, d), BF16)],
        compiler_params=_params("arbitrary", "arbitrary"),
        name="ffn_" + tail,
    )(*args)


def _grid_pos_embed(n_tokens):
    rows = n_tokens // GRID_W
    r = np.repeat(np.arange(rows), GRID_W).astype(np.float64)
    col = np.tile(np.arange(GRID_W), rows).astype(np.float64)
    quarter = D_MODEL // 4
    freq = 1.0 / (10000.0 ** (np.arange(quarter, dtype=np.float64) / quarter))
    ar = r[:, None] * freq[None, :]
    ac = col[:, None] * freq[None, :]
    return np.concatenate([np.sin(ar), np.cos(ar), np.sin(ac), np.cos(ac)], axis=-1)


def _deltanet_layer(x, h, j, gate, state_f, state_b, dn_w_in, dn_conv_w, dn_a_log, dn_dt_bias, dn_norm_w, dn_w_out):
    tm = tn = 1024

    def project(col0, n_cols, name, epilogue=_ep_plain, extras=(), out_dtype=F32, tn=tn, row_pieces=1):
        return matmul_ws(h, dn_w_in, j, n_cols, tm=tm, tn=tn, epilogue=epilogue, extras=extras,
                         out_dtype=out_dtype, col_block0=col0 // tn, row_pieces=row_pieces, name=name)

    def conv_w_for(col0):
        return (dn_conv_w, pl.BlockSpec((None, DN_CONV, tn), lambda jj, i: (j, 0, col0 // tn + jj)))

    qk = project(0, 2 * DN_K_DIM, "dn_in_qk", extras=(conv_w_for(0),), out_dtype=BF16, row_pieces=tm // SEQ,
                 epilogue=functools.partial(_ep_conv_silu, normalize=True, n_q_tiles=DN_K_DIM // tn))
    v = project(2 * DN_K_DIM, DN_V_DIM, "dn_in_v", extras=(conv_w_for(2 * DN_K_DIM),), row_pieces=tm // SEQ,
                epilogue=functools.partial(_ep_conv_silu, normalize=False, n_q_tiles=0))
    z = project(DN_QKV_DIM, DN_V_DIM, "dn_in_z")
    ab = project(DN_QKV_DIM + DN_V_DIM, DN_GATE_COLS, "dn_in_gates", tn=DN_GATE_COLS)

    zeros = jnp.zeros((DN_HEADS_V,), F32)
    a_log_row = jnp.concatenate([dn_a_log[j, 0], zeros, dn_a_log[j, 1], zeros])[None, :]
    dt_row = jnp.concatenate([dn_dt_bias[j, 0], zeros, dn_dt_bias[j, 1], zeros])[None, :]
    gates = dn_gates(ab, a_log_row, dt_row)
    m = gates.shape[0]
    g5 = gates.reshape(m // DN_CHUNK, DN_CHUNK, 4, DN_HEADS_K, 2)
    gates_row = g5.transpose(3, 0, 2, 4, 1).reshape(DN_HEADS_K, m // DN_CHUNK, 4, 2 * DN_CHUNK)
    gates_row = jnp.pad(gates_row, ((0, 0), (0, 0), (0, SUBLANES - 4), (0, 0)))
    norm_w = dn_norm_w.reshape(dn_norm_w.shape[0], 1, DN_DV)

    og_ctx, s_f, s_b = dn_core(qk, v, z, gates_row, norm_w, j, seq_len=SEQ, n_seq=BATCH, row0=0,
                               heads=4, group=2)
    og_lat = dn_core(qk, v, z, gates_row, norm_w, j, seq_len=DEC_SEQ, n_seq=DEC_BATCH, row0=N_CTX_TOK,
                     heads=2, group=4, init_states=(state_f, state_b))
    x = matmul_resid_gate([og_ctx, og_lat], dn_w_out, j, x, gate, tm=512, tn=512, name="dn_out")
    return x, s_f, s_b


def _chunk_mlp_layer(x, h, j, gate, cm_w_in, cm_b_in, cm_ln_w, cm_ln_b, cm_w_s, cm_b_s, cm_w_out):
    n_in = 2 * CM_DIM
    tn = 1024
    bias = (cm_b_in.reshape(cm_b_in.shape[0], 1, n_in), pl.BlockSpec((None, 1, tn), lambda jj, i: (j, 0, jj)))
    zz = matmul_ws(h, cm_w_in, j, n_in, tm=1024, tn=tn, epilogue=_ep_bias_gelu, extras=(bias,), out_dtype=BF16,
                   name="cm_in")
    n_b = cm_ln_w.shape[0]
    uv = cm_gate(zz, cm_ln_w.reshape(n_b, 1, CM_DIM), cm_ln_b.reshape(n_b, 1, CM_DIM), cm_w_s,
                 cm_b_s[..., None], j)
    return matmul_resid_gate(uv, cm_w_out, j, x, gate, tm=1024, tn=512, name="cm_out")


def kernel(x_prompt, x_sample, state_dn_fwd, state_dn_bwd, c, c_ctx, norm_mix_w, norm_mlp_w, w_ada, b_ada, dn_w_in, dn_conv_w, dn_A_log, dn_dt_bias, dn_norm_w, dn_w_out, cm_w_in, cm_b_in, cm_ln_w, cm_ln_b, cm_w_s, cm_b_s, cm_w_out, w_ff1, w_ff2, final_norm_w):
    cond = jnp.concatenate([c_ctx[None, :], c, jnp.zeros((N_COND - 1 - DEC_BATCH, D_MODEL), F32)], axis=0)
    mod = adaln_all(cond, w_ada, b_ada)
    mod = mod.reshape(DEPTH, N_COND, N_MOD, D_MODEL).transpose(0, 2, 1, 3)[:, :, :, None, :]
    mods = [[mod[i, t] for t in range(N_MOD)] for i in range(DEPTH)]

    x, h = embed_norm_modulate(x_prompt.reshape(N_CTX_TOK, D_MODEL), x_sample.reshape(N_LAT_TOK, D_MODEL),
                               jnp.asarray(_grid_pos_embed(DEC_SEQ), x_sample.dtype), norm_mix_w, 0,
                               mods[0][0], mods[0][1])
    new_fwd, new_bwd = [], []
    for i in range(DEPTH):
        j = i // N_MIXERS
        _, _, gate, shift2, scale2, gate2 = mods[i]
        if i % N_MIXERS == 0:
            x, s_f, s_b = _deltanet_layer(x, h, j, gate, state_dn_fwd, state_dn_bwd, dn_w_in, dn_conv_w,
                                          dn_A_log, dn_dt_bias, dn_norm_w, dn_w_out)
            new_fwd.append(s_f)
            new_bwd.append(s_b)
        else:
            x = _chunk_mlp_layer(x, h, j, gate, cm_w_in, cm_b_in, cm_ln_w, cm_ln_b, cm_w_s, cm_b_s, cm_w_out)
        mlp = functools.partial(ffn, x, norm_mlp_w, i, shift2, scale2, gate2, w_ff1, w_ff2)
        if i < DEPTH - 1:
            x, h = mlp(tail="next_h", tail_w=norm_mix_w[i + 1], tail_mod=mods[i + 1][:2])
        else:
            y_ctx = mlp(tail="final_norm", tail_w=final_norm_w, row0=0, n_rows=N_CTX_TOK)
            y_lat = mlp(tail="final_norm", tail_w=final_norm_w, row0=N_CTX_TOK, n_rows=N_LAT_TOK)

    y_prompt = y_ctx.reshape(BATCH, SEQ, D_MODEL)
    y_sample = y_lat.reshape(DEC_BATCH, DEC_SEQ, D_MODEL)
    return (y_prompt, y_sample, jnp.concatenate(new_fwd, axis=1), jnp.concatenate(new_bwd, axis=1))
```

```python
import functools
import math

import jax
import jax.numpy as jnp
import numpy as np
from jax import lax
from jax.experimental import pallas as pl
from jax.experimental.pallas import tpu as pltpu

F32 = jnp.float32
BF16 = jnp.bfloat16

D_MODEL = 2048
BATCH = 16
SEQ = 256
DEPTH = 2
DEC_BATCH = 2
DEC_SEQ = 1024
GRID_W = 64
N_MIXERS = 2
DN_DK = 128
DN_DV = 128
DN_HEADS_K = D_MODEL // DN_DK
DN_HEADS_V = 2 * DN_HEADS_K
DN_K_DIM = DN_HEADS_K * DN_DK
DN_V_DIM = DN_HEADS_V * DN_DV
DN_QKV_DIM = 2 * DN_K_DIM + DN_V_DIM
DN_GATE_COLS = 4 * DN_HEADS_V
DN_CONV = 5
DN_CHUNK = 64
DN_UNROLLED_SCAN_CHUNKS = 4
CM_DIM = 2 * D_MODEL
CM_CHUNK = 128
CM_GROUPS = 16
CM_GDIM = CM_DIM // CM_GROUPS
FF_DIM = 4 * D_MODEL
N_MOD = 6
EPS = 1e-6

N_CTX_TOK = BATCH * SEQ
N_LAT_TOK = DEC_BATCH * DEC_SEQ
N_TOK = N_CTX_TOK + N_LAT_TOK
N_COND = 8
SUBLANES = 8
CONV_HALO = SUBLANES
VMEM_LIMIT_BYTES = 56 * 1024 * 1024


def _group_of_row(row0):
    return jnp.where(row0 < N_CTX_TOK, 0, 1 + (row0 - N_CTX_TOK) // DEC_SEQ)


def _params(*sem):
    return pltpu.CompilerParams(dimension_semantics=sem, vmem_limit_bytes=VMEM_LIMIT_BYTES)


def _rms(x, w):
    return x * lax.rsqrt(jnp.mean(x * x, axis=-1, keepdims=True) + EPS) * w


def _silu(x):
    half = 0.5 * x
    return half + half * jnp.tanh(half)


def _gelu_tanh(x):
    return 0.5 * x * (1.0 + jnp.tanh(math.sqrt(2.0 / math.pi) * (x + 0.044715 * (x * x * x))))


def _adaln_kernel(c_ref, w_ref, b_ref, o_ref):
    x = _silu(c_ref[...]).astype(BF16)
    acc = jnp.dot(x, w_ref[...].astype(BF16), preferred_element_type=F32)
    o_ref[...] = acc + b_ref[...]


def adaln_all(cond, w_ada, b_ada, tn=1024):
    depth, d, n = w_ada.shape
    return pl.pallas_call(
        _adaln_kernel,
        grid=(depth, n // tn),
        in_specs=[
            pl.BlockSpec((N_COND, d), lambda l, j: (0, 0)),
            pl.BlockSpec((None, d, tn), lambda l, j: (l, 0, j)),
            pl.BlockSpec((None, 1, tn), lambda l, j: (l, 0, j)),
        ],
        out_specs=pl.BlockSpec((None, N_COND, tn), lambda l, j: (l, 0, j)),
        out_shape=jax.ShapeDtypeStruct((depth, N_COND, n), F32),
        compiler_params=_params("arbitrary", "arbitrary"),
        name="adaln",
    )(cond, w_ada, b_ada.reshape(depth, 1, n))


def _embed_kernel(xc_ref, xl_ref, pe_ref, w_ref, sh_ref, sc_ref, x_ref, h_ref, *, ctx_tiles):
    def emit(x):
        x_ref[...] = x
        h_ref[...] = (_rms(x, w_ref[...]) * (1.0 + sc_ref[...]) + sh_ref[...]).astype(BF16)

    i = pl.program_id(0)
    pl.when(i < ctx_tiles)(lambda: emit(xc_ref[...]))
    pl.when(i >= ctx_tiles)(lambda: emit(xl_ref[...] + pe_ref[...]))


def embed_norm_modulate(x_ctx, x_lat, pos_table, norm_w, layer, shift, scale, tm=512):
    d = x_ctx.shape[1]
    ctx_tiles = x_ctx.shape[0] // tm
    lat_tiles = x_lat.shape[0] // tm
    pos_tiles = pos_table.shape[0] // tm
    m = x_ctx.shape[0] + x_lat.shape[0]
    mod_spec = pl.BlockSpec((None, 1, d), lambda i: (_group_of_row(i * tm), 0, 0))
    row_spec = pl.BlockSpec((tm, d), lambda i: (i, 0))
    return pl.pallas_call(
        functools.partial(_embed_kernel, ctx_tiles=ctx_tiles),
        grid=(ctx_tiles + lat_tiles,),
        in_specs=[
            pl.BlockSpec((tm, d), lambda i: (jnp.minimum(i, ctx_tiles - 1), 0)),
            pl.BlockSpec((tm, d), lambda i: (jnp.clip(i - ctx_tiles, 0, lat_tiles - 1), 0)),
            pl.BlockSpec((tm, d), lambda i: (jnp.maximum(i - ctx_tiles, 0) % pos_tiles, 0)),
            pl.BlockSpec((None, 1, d), lambda i: (layer, 0, 0)),
            mod_spec,
            mod_spec,
        ],
        out_specs=[row_spec, row_spec],
        out_shape=[jax.ShapeDtypeStruct((m, d), F32), jax.ShapeDtypeStruct((m, d), BF16)],
        compiler_params=_params("arbitrary"),
        name="embed_norm_modulate",
    )(x_ctx, x_lat, pos_table, norm_w.reshape(norm_w.shape[0], 1, d), shift, scale)


def _mm_kernel(*refs, part_tiles, n_extra, epilogue, row_pieces):
    n_parts = len(part_tiles)
    x_refs = refs[:n_parts]
    w_ref = refs[n_parts]
    extra = refs[n_parts + 1:n_parts + 1 + n_extra]
    o_ref = refs[n_parts + 1 + n_extra]
    wb_ref = refs[n_parts + 2 + n_extra]
    i = pl.program_id(1)

    @pl.when(i == 0)
    def _():
        wb_ref[...] = w_ref[...].astype(BF16)

    def run(x_ref):
        extra_vals = [r[...] for r in extra]
        if row_pieces == 1:
            epilogue(jnp.dot(x_ref[...], wb_ref[...], preferred_element_type=F32), o_ref, *extra_vals)
            return
        rows = x_ref.shape[0] // row_pieces
        accs = []
        for s in range(row_pieces):
            accs.append(jnp.dot(x_ref[s * rows:(s + 1) * rows, :], wb_ref[...], preferred_element_type=F32))
            if s >= 1:
                epilogue(accs, s - 1, o_ref, *extra_vals)
        epilogue(accs, row_pieces - 1, o_ref, *extra_vals)

    if n_parts == 1:
        run(x_refs[0])
    else:
        first = 0
        for x_ref, tiles in zip(x_refs, part_tiles):
            pl.when(jnp.logical_and(i >= first, i < first + tiles))(functools.partial(run, x_ref))
            first += tiles


def matmul_ws(x, w, layer, n_out, *, tm, tn, epilogue, extras=(), out_dtype=F32, col_block0=0, row_pieces=1,
              name="matmul"):
    parts = list(x) if isinstance(x, (list, tuple)) else [x]
    k = parts[0].shape[1]
    part_tiles = [part.shape[0] // tm for part in parts]
    m = sum(part.shape[0] for part in parts)
    in_specs = []
    first = 0
    for tiles in part_tiles:
        in_specs.append(pl.BlockSpec(
            (tm, k), lambda j, i, first=first, tiles=tiles: (jnp.clip(i - first, 0, tiles - 1), 0)))
        first += tiles
    in_specs.append(pl.BlockSpec((None, k, tn), lambda j, i: (layer, 0, j + col_block0)))
    in_specs += [spec for _, spec in extras]
    return pl.pallas_call(
        functools.partial(_mm_kernel, part_tiles=tuple(part_tiles), n_extra=len(extras), epilogue=epilogue,
                          row_pieces=row_pieces),
        grid=(n_out // tn, m // tm),
        in_specs=in_specs,
        out_specs=pl.BlockSpec((tm, tn), lambda j, i: (i, j)),
        out_shape=jax.ShapeDtypeStruct((m, n_out), out_dtype),
        scratch_shapes=[pltpu.VMEM((k, tn), BF16)],
        compiler_params=_params("arbitrary", "arbitrary"),
        name=name,
    )(*parts, w, *[a for a, _ in extras])


def _ep_plain(acc, o_ref):
    o_ref[...] = acc.astype(o_ref.dtype)


def _ep_bias_gelu(acc, o_ref, b):
    o_ref[...] = _gelu_tanh(acc + b).astype(o_ref.dtype)


def _ep_resid_gate(acc, o_ref, resid, gate):
    o_ref[...] = (resid + gate * acc).astype(o_ref.dtype)


def _ep_conv_silu(accs, s, o_ref, w, *, normalize, n_q_tiles):
    pieces = o_ref.shape[0] // SEQ
    tn = o_ref.shape[1]
    assert o_ref.shape[0] == DEC_SEQ and accs[s].shape[0] == SEQ
    every_piece_is_a_sequence = pl.program_id(1) * o_ref.shape[0] < N_CTX_TOK
    pad = DN_CONV // 2
    n_ext = SEQ + 2 * CONV_HALO
    zeros = jnp.zeros((CONV_HALO, tn), F32)
    prev = zeros if s == 0 else jnp.where(every_piece_is_a_sequence, 0.0, accs[s - 1][SEQ - CONV_HALO:])
    nxt = zeros if s == pieces - 1 else jnp.where(every_piece_is_a_sequence, 0.0, accs[s + 1][:CONV_HALO])
    ext = jnp.concatenate([prev, accs[s], nxt], axis=0)
    y = None
    for t in range(DN_CONV):
        shifted = ext if t == pad else pltpu.roll(ext, (pad - t) % n_ext, 0)
        term = shifted[CONV_HALO:CONV_HALO + SEQ] * w[t:t + 1, :]
        y = term if y is None else y + term
    y = _silu(y)
    rows = slice(s * SEQ, (s + 1) * SEQ)
    if normalize:
        scale = jnp.where(pl.program_id(0) < n_q_tiles, DN_DK ** -0.5, 1.0)
        for hh in range(tn // DN_DK):
            lanes = slice(hh * DN_DK, (hh + 1) * DN_DK)
            sl = y[:, lanes]
            inv = lax.rsqrt(jnp.sum(sl * sl, axis=-1, keepdims=True) + EPS) * scale
            o_ref[rows, lanes] = (sl * inv).astype(o_ref.dtype)
    else:
        o_ref[rows, :] = y.astype(o_ref.dtype)


def matmul_resid_gate(x, w, layer, resid, gate, *, tm, tn, name):
    n_out = resid.shape[1]
    extras = (
        (resid, pl.BlockSpec((tm, tn), lambda j, i: (i, j))),
        (gate, pl.BlockSpec((None, 1, tn), lambda j, i: (_group_of_row(i * tm), 0, j))),
    )
    return matmul_ws(x, w, layer, n_out, tm=tm, tn=tn, epilogue=_ep_resid_gate, extras=extras, name=name)


def _split3(x):
    hi = x.astype(BF16)
    r1 = x - hi.astype(F32)
    mid = r1.astype(BF16)
    lo = (r1 - mid.astype(F32)).astype(BF16)
    return hi, mid, lo


def _gates_kernel(ab_ref, alog_ref, dtb_ref, o_ref):
    c = DN_CHUNK
    ii = lax.broadcasted_iota(jnp.int32, (c, c), 0)
    jj = lax.broadcasted_iota(jnp.int32, (c, c), 1)
    tril = (ii >= jj).astype(BF16)
    triu = (ii <= jj).astype(BF16)
    lane = lax.broadcasted_iota(jnp.int32, (c, ab_ref.shape[1]), 1)
    kind = lane // DN_HEADS_V
    for n in range(ab_ref.shape[0] // c):
        ab = ab_ref[n * c:(n + 1) * c, :]
        log_g = -jnp.exp(alog_ref[...]) * jax.nn.softplus(ab + dtb_ref[...])
        beta = jax.nn.sigmoid(ab)
        parts = _split3(log_g)
        cum_f = sum(jnp.dot(tril, part, preferred_element_type=F32) for part in parts)
        cum_b = sum(jnp.dot(triu, part, preferred_element_type=F32) for part in parts)
        o_ref[n * c:(n + 1) * c, :] = jnp.where(kind == 0, cum_f, jnp.where(kind == 2, cum_b, beta))


def dn_gates(ab, a_log_row, dt_bias_row, chunks_per_step=8):
    m, n = ab.shape
    rows = chunks_per_step * DN_CHUNK
    return pl.pallas_call(
        _gates_kernel,
        grid=(m // rows,),
        in_specs=[
            pl.BlockSpec((rows, n), lambda i: (i, 0)),
            pl.BlockSpec((1, n), lambda i: (0, 0)),
            pl.BlockSpec((1, n), lambda i: (0, 0)),
        ],
        out_specs=pl.BlockSpec((rows, n), lambda i: (i, 0)),
        out_shape=jax.ShapeDtypeStruct((m, n), F32),
        compiler_params=_params("arbitrary"),
        name="dn_gates",
    )(ab, a_log_row, dt_bias_row)


def _dot(a, b):
    return jnp.dot(a, b, preferred_element_type=F32)


def _inv_unit_triangular_many(a_list):
    n = a_list[0].shape[0]
    ii = lax.broadcasted_iota(jnp.int32, (n, n), 0)
    jj = lax.broadcasted_iota(jnp.int32, (n, n), 1)
    eye = jnp.where(ii == jj, 1.0, 0.0)
    xs = [eye - jnp.where((ii >> 1) == (jj >> 1), a, 0.0) for a in a_list]
    for level in range(1, int(math.log2(DN_CHUNK))):
        joins = jnp.logical_and((ii >> (level + 1)) == (jj >> (level + 1)), (ii >> level) != (jj >> level))
        ns = [jnp.where(joins, a, 0.0).astype(BF16) for a in a_list]
        xbs = [x.astype(BF16) for x in xs]
        ys = [_dot(xb, nn).astype(BF16) for xb, nn in zip(xbs, ns)]
        xs = [x - _dot(y, xb) for x, y, xb in zip(xs, ys, xbs)]
    return xs


def _dn_core_kernel(*refs, n_chunks, group, heads, has_init, has_state_out):
    q_ref, k_ref, v_ref, z_ref, gr_ref, nw_ref = refs[:6]
    pos = 6
    if has_init:
        s0f_ref, s0b_ref = refs[pos:pos + 2]
        pos += 2
    og_ref = refs[pos]
    pos += 1
    if has_state_out:
        sf_ref, sb_ref = refs[pos:pos + 2]
        pos += 2
    wm_ref, cm_ref, egl_ref, p_ref, u_ref, wv_ref, sh_ref, s_ref = refs[pos:pos + 8]

    c = DN_CHUNK
    pc = 2 * c
    dv = DN_DV
    if has_init:
        for hl in range(heads):
            s_ref[2 * hl] = jnp.concatenate([s0f_ref[2 * hl], s0f_ref[2 * hl + 1]], axis=1)
            s_ref[2 * hl + 1] = jnp.concatenate([s0b_ref[2 * hl], s0b_ref[2 * hl + 1]], axis=1)
    else:
        s_ref[...] = jnp.zeros_like(s_ref)

    ii = lax.broadcasted_iota(jnp.int32, (pc, pc), 0)
    jj = lax.broadcasted_iota(jnp.int32, (pc, pc), 1)
    chunk_shift = int(math.log2(c))
    same_head = (ii >> chunk_shift) == (jj >> chunk_shift)
    top = ii < c
    top_col = lax.broadcasted_iota(jnp.int32, (pc, 1), 0) < c
    nt_dims = (((1,), (1,)), ((), ()))

    def rows_of(chunk):
        start = chunk * c
        return pl.ds(start if isinstance(start, int) else pl.multiple_of(start, c), c)

    def gate_columns(units):
        return [gr_ref[hl, ch].T for hl, ch in units]

    def entry(hl, chunk, d):
        return (hl * n_chunks + chunk) * 2 + d

    def qk_lanes(hl):
        return slice(hl * DN_DK, (hl + 1) * DN_DK)

    def v_lanes(hl, hh):
        return slice((2 * hl + hh) * dv, (2 * hl + hh + 1) * dv)

    def phase_a(base, size):
        units = [(hl, base + i) for hl in range(heads) for i in range(size)]
        chains = [(n, d) for n in range(len(units)) for d in (0, 1)]
        k2s, grams = [], []
        for hl, ch in units:
            kc = k_ref[rows_of(ch), qk_lanes(hl)]
            qc = q_ref[rows_of(ch), qk_lanes(hl)]
            k2 = jnp.concatenate([kc, kc], axis=0)
            k2s.append(k2)
            lhs = jnp.concatenate([kc, kc, qc, qc], axis=0)
            grams.append(lax.dot_general(lhs, k2, nt_dims, preferred_element_type=F32))
        colvs = gate_columns(units)
        a_mats, p_mats, gcols, bcols = [], [], [], []
        for n, d in chains:
            hl, ch = units[n]
            gcol, bcol = colvs[n][:, 2 * d:2 * d + 1], colvs[n][:, 2 * d + 1:2 * d + 2]
            grow = gr_ref[hl, ch][2 * d:2 * d + 1, :]
            incl = jnp.logical_and(same_head, (ii >= jj) if d == 0 else (ii <= jj))
            strict = jnp.logical_and(same_head, (ii > jj) if d == 0 else (ii < jj))
            decay = jnp.exp(jnp.where(incl, gcol - grow, -1e30))
            a_mats.append(jnp.where(strict, bcol * grams[n][:pc] * decay, 0.0))
            p_mats.append(grams[n][pc:] * decay)
            gcols.append(gcol)
            bcols.append(bcol)
        t_invs = _inv_unit_triangular_many(a_mats)
        rhss, kfs = [], []
        for (n, d), gcol, bcol in zip(chains, gcols, bcols):
            hl, ch = units[n]
            kf = k2s[n].astype(F32)
            rows = rows_of(ch)
            vp = jnp.concatenate([v_ref[rows, v_lanes(hl, 0)], v_ref[rows, v_lanes(hl, 1)]], axis=0)
            rhss.append(jnp.concatenate([(bcol * jnp.exp(gcol)) * kf, bcol * vp], axis=1).astype(BF16))
            kfs.append(kf)
        uws = [_dot(t.astype(BF16), r) for t, r in zip(t_invs, rhss)]
        kdts, x2s, egls = [], [], []
        for (n, d), gcol, kf, uw in zip(chains, gcols, kfs, uws):
            r0 = c - 1 if d == 0 else 0
            gl0 = gcol[r0:r0 + 1]
            gl1 = gcol[c + r0:c + r0 + 1]
            gl = jnp.where(top_col, gl0, gl1)
            kdts.append((kf * jnp.exp(gl - gcol)).T.astype(BF16))
            u, wv = uw[:, :dv], uw[:, dv:]
            x2s.append(jnp.concatenate([jnp.where(top, u, 0.0), jnp.where(top, 0.0, u),
                                        jnp.where(top, wv, 0.0), jnp.where(top, 0.0, wv)], axis=1).astype(BF16))
            egls.append(jnp.concatenate([jnp.broadcast_to(jnp.exp(gl0), (1, dv)),
                                         jnp.broadcast_to(jnp.exp(gl1), (1, dv))], axis=1))
        wcs = [_dot(kdt, x2) for kdt, x2 in zip(kdts, x2s)]
        for (n, d), p_mat, uw, wc, egl in zip(chains, p_mats, uws, wcs, egls):
            e = entry(*units[n], d)
            wm_ref[e] = wc[:, :2 * dv].astype(BF16)
            cm_ref[e] = wc[:, 2 * dv:]
            egl_ref[e] = egl
            p_ref[e] = p_mat.astype(BF16)
            u_ref[e] = uw[:, :dv].astype(BF16)
            wv_ref[e] = uw[:, dv:]

    def phase_b(n, carry):
        lines = [(hl, d) for hl in range(heads) for d in (0, 1)]
        es = [entry(hl, n if d == 0 else n_chunks - 1 - n, d) for hl, d in lines]
        ss = [s_ref[2 * hl + d] for hl, d in lines]
        sbs = [s.astype(BF16) for s in ss]
        for e, sb in zip(es, sbs):
            sh_ref[e] = sb
        wms = [wm_ref[e] for e in es]
        wss = [jnp.concatenate([_dot(wm[:, :dv], sb[:, :dv]), _dot(wm[:, dv:], sb[:, dv:])], axis=1)
               for wm, sb in zip(wms, sbs)]
        for (hl, d), e, s, ws in zip(lines, es, ss, wss):
            s_ref[2 * hl + d] = egl_ref[e] * s - ws + cm_ref[e]
        return carry

    def phase_c(base, size):
        units = [(hl, base + i) for hl in range(heads) for i in range(size)]
        chains = [(n, d) for n in range(len(units)) for d in (0, 1)]
        es = [entry(*units[n], d) for n, d in chains]
        rs = []
        for (n, d), e in zip(chains, es):
            hl, ch = units[n]
            lhs = jnp.concatenate([q_ref[rows_of(ch), qk_lanes(hl)], u_ref[e]], axis=0)
            rs.append(_dot(lhs, sh_ref[e]))
        v_news, q_ss = [], []
        for e, r in zip(es, rs):
            q_ss.append(jnp.concatenate([r[:c, :dv], r[:c, dv:]], axis=0))
            u_s = jnp.concatenate([r[c:2 * c, :dv], r[2 * c:, dv:]], axis=0)
            v_news.append((wv_ref[e] - u_s).astype(BF16))
        pvs = [_dot(p_ref[e], v_new) for e, v_new in zip(es, v_news)]
        colvs = gate_columns(units)
        outs = [jnp.exp(colvs[n][:, 2 * d:2 * d + 1]) * q_s + pv for (n, d), q_s, pv in zip(chains, q_ss, pvs)]
        for n, (hl, ch) in enumerate(units):
            o = outs[2 * n] + outs[2 * n + 1]
            rows = rows_of(ch)
            for hh in range(2):
                y = _rms(o[hh * c:(hh + 1) * c], nw_ref[...])
                og_ref[rows, v_lanes(hl, hh)] = (y * _silu(z_ref[rows, v_lanes(hl, hh)])).astype(og_ref.dtype)

    def over_groups(phase):
        n_groups = n_chunks // group
        if n_groups == 1:
            phase(0, group)
        else:
            def body(g, carry):
                phase(g * group, group)
                return carry
            lax.fori_loop(0, n_groups, body, 0)

    over_groups(phase_a)
    if n_chunks <= DN_UNROLLED_SCAN_CHUNKS:
        for n in range(n_chunks):
            phase_b(n, 0)
        phase_c(0, n_chunks)
    else:
        lax.fori_loop(0, n_chunks, phase_b, 0)
        over_groups(phase_c)

    if has_state_out:
        for hl in range(heads):
            for hh in range(2):
                sf_ref[2 * hl + hh] = s_ref[2 * hl][:, hh * dv:(hh + 1) * dv]
                sb_ref[2 * hl + hh] = s_ref[2 * hl + 1][:, hh * dv:(hh + 1) * dv]


def dn_core(qk, v, z, gates_row, norm_w, layer, *, seq_len, n_seq, row0, heads, group, init_states=None):
    n_chunks = seq_len // DN_CHUNK
    rb0 = row0 // seq_len
    qkw = heads * DN_DK
    pair = 2 * DN_DV
    vw = heads * pair
    pc = 2 * DN_CHUNK
    n_e = 2 * n_chunks * heads
    assert n_chunks % group == 0 and DN_HEADS_K % heads == 0
    has_init = init_states is not None
    in_specs = [
        pl.BlockSpec((seq_len, qkw), lambda s, h: (rb0 + s, h)),
        pl.BlockSpec((seq_len, qkw), lambda s, h: (rb0 + s, DN_HEADS_K // heads + h)),
        pl.BlockSpec((seq_len, vw), lambda s, h: (rb0 + s, h)),
        pl.BlockSpec((seq_len, vw), lambda s, h: (rb0 + s, h)),
        pl.BlockSpec((heads, n_chunks, SUBLANES, 2 * DN_CHUNK), lambda s, h: (h, rb0 + s, 0, 0)),
        pl.BlockSpec((None, 1, DN_DV), lambda s, h: (layer, 0, 0)),
    ]
    args = [qk, qk, v, z, gates_row, norm_w]
    og_shape = jax.ShapeDtypeStruct((n_seq * seq_len, DN_V_DIM), BF16)
    og_spec = pl.BlockSpec((seq_len, vw), lambda s, h: (s, h))
    if has_init:
        st_spec = pl.BlockSpec((None, None, 2 * heads, DN_DK, DN_DV), lambda s, h: (s, layer, h, 0, 0))
        in_specs += [st_spec, st_spec]
        args += [init_states[0], init_states[1]]
        out_specs = og_spec
        out_shape = og_shape
    else:
        so_spec = pl.BlockSpec((None, None, 2 * heads, DN_DK, DN_DV), lambda s, h: (s, 0, h, 0, 0))
        so_shape = jax.ShapeDtypeStruct((n_seq, 1, DN_HEADS_V, DN_DK, DN_DV), F32)
        out_specs = [og_spec, so_spec, so_spec]
        out_shape = [og_shape, so_shape, so_shape]
    return pl.pallas_call(
        functools.partial(_dn_core_kernel, n_chunks=n_chunks, group=group, heads=heads, has_init=has_init,
                          has_state_out=not has_init),
        grid=(n_seq, DN_HEADS_K // heads),
        in_specs=in_specs,
        out_specs=out_specs,
        out_shape=out_shape,
        scratch_shapes=[
            pltpu.VMEM((n_e, DN_DK, pair), BF16),
            pltpu.VMEM((n_e, DN_DK, pair), F32),
            pltpu.VMEM((n_e, 1, pair), F32),
            pltpu.VMEM((n_e, pc, pc), BF16),
            pltpu.VMEM((n_e, pc, DN_DK), BF16),
            pltpu.VMEM((n_e, pc, DN_DV), F32),
            pltpu.VMEM((n_e, DN_DK, pair), BF16),
            pltpu.VMEM((2 * heads, DN_DK, pair), F32),
        ],
        compiler_params=_params("arbitrary", "arbitrary"),
        name="dn_core_lat" if has_init else "dn_core_ctx",
    )(*args)


def _cm_gate_kernel(u_ref, v_ref, lw_ref, lb_ref, ws_ref, bs_ref, o_ref):
    for n in range(u_ref.shape[0] // CM_CHUNK):
        rows = slice(n * CM_CHUNK, (n + 1) * CM_CHUNK)
        v = v_ref[rows, :].astype(F32)
        mu = jnp.mean(v, axis=-1, keepdims=True)
        vc = v - mu
        vn = vc * lax.rsqrt(jnp.mean(vc * vc, axis=-1, keepdims=True) + EPS) * lw_ref[...] + lb_ref[...]
        vn = vn.astype(BF16)
        for g in range(CM_GROUPS):
            lanes = slice(g * CM_GDIM, (g + 1) * CM_GDIM)
            sp = jnp.dot(ws_ref[g].astype(BF16), vn[:, lanes], preferred_element_type=F32) + bs_ref[g]
            o_ref[rows, lanes] = (u_ref[rows, lanes].astype(F32) * sp).astype(o_ref.dtype)


def cm_gate(zz, ln_w, ln_b, w_s, b_s_col, layer, chunks_per_step=2):
    m = zz.shape[0]
    rows = chunks_per_step * CM_CHUNK
    row = pl.BlockSpec((None, 1, CM_DIM), lambda i: (layer, 0, 0))
    return pl.pallas_call(
        _cm_gate_kernel,
        grid=(m // rows,),
        in_specs=[
            pl.BlockSpec((rows, CM_DIM), lambda i: (i, 0)),
            pl.BlockSpec((rows, CM_DIM), lambda i: (i, 1)),
            row,
            row,
            pl.BlockSpec((None, CM_GROUPS, CM_CHUNK, CM_CHUNK), lambda i: (layer, 0, 0, 0)),
            pl.BlockSpec((None, CM_GROUPS, CM_CHUNK, 1), lambda i: (layer, 0, 0, 0)),
        ],
        out_specs=pl.BlockSpec((rows, CM_DIM), lambda i: (i, 0)),
        out_shape=jax.ShapeDtypeStruct((m, CM_DIM), BF16),
        compiler_params=_params("arbitrary"),
        name="cm_gate",
    )(zz, zz, ln_w, ln_b, w_s, b_s_col)


def _ffn_kernel(*refs, tail):
    x_ref, nw_ref, sh_ref, sc_ref, g_ref, w1_ref, w2_ref, tw_ref = refs[:8]
    if tail == "next_h":
        tsh_ref, tsc_ref, o_ref, hn_ref, h_ref = refs[8:]
    else:
        o_ref, h_ref = refs[8:]
    kk = pl.program_id(1)

    @pl.when(kk == 0)
    def _():
        y = _rms(x_ref[...], nw_ref[...])
        h_ref[...] = (y * (1.0 + sc_ref[...]) + sh_ref[...]).astype(BF16)
        o_ref[...] = jnp.zeros_like(o_ref)

    a = jnp.dot(h_ref[...], w1_ref[...].astype(BF16), preferred_element_type=F32)
    a = jnp.square(jnp.maximum(a, 0.0)).astype(BF16)
    o_ref[...] += jnp.dot(a, w2_ref[...].astype(BF16), preferred_element_type=F32)

    @pl.when(kk == pl.num_programs(1) - 1)
    def _():
        r = x_ref[...] + g_ref[...] * o_ref[...]
        if tail == "final_norm":
            r = _rms(r, tw_ref[...])
        else:
            hn_ref[...] = (_rms(r, tw_ref[...]) * (1.0 + tsc_ref[...]) + tsh_ref[...]).astype(BF16)
        o_ref[...] = r


def ffn(x, norm_w, layer, shift, scale, gate, w1, w2, *, tail, tail_w, tail_mod=None, row0=0, n_rows=None,
        tm=1024, tk=512):
    d = x.shape[1]
    n_rows = x.shape[0] if n_rows is None else n_rows
    ff = w1.shape[2]
    t0 = row0 // tm
    mod_spec = pl.BlockSpec((None, 1, d), lambda i, k: (_group_of_row((t0 + i) * tm), 0, 0))
    row_spec = pl.BlockSpec((tm, d), lambda i, k: (i, 0), pipeline_mode=pl.Buffered(1))
    in_specs = [
        pl.BlockSpec((tm, d), lambda i, k: (t0 + i, 0), pipeline_mode=pl.Buffered(1)),
        pl.BlockSpec((None, 1, d), lambda i, k: (layer, 0, 0)),
        mod_spec,
        mod_spec,
        mod_spec,
        pl.BlockSpec((None, d, tk), lambda i, k: (layer, 0, k)),
        pl.BlockSpec((None, tk, d), lambda i, k: (layer, k, 0)),
        pl.BlockSpec((1, d), lambda i, k: (0, 0)),
    ]
    args = [x, norm_w.reshape(norm_w.shape[0], 1, d), shift, scale, gate, w1, w2, tail_w.reshape(1, d)]
    out_specs, out_shape = row_spec, jax.ShapeDtypeStruct((n_rows, d), F32)
    if tail == "next_h":
        in_specs += [mod_spec, mod_spec]
        args += list(tail_mod)
        out_specs = [row_spec, row_spec]
        out_shape = [out_shape, jax.ShapeDtypeStruct((n_rows, d), BF16)]
    return pl.pallas_call(
        functools.partial(_ffn_kernel, tail=tail),
        grid=(n_rows // tm, ff // tk),
        in_specs=in_specs,
        out_specs=out_specs,
        out_shape=out_shape,
        scratch_shapes=[pltpu.VMEM((tm, d), BF16)],
        compiler_params=_params("arbitrary", "arbitrary"),
        name="ffn_" + tail,
    )(*args)


def _grid_pos_embed(n_tokens):
    rows = n_tokens // GRID_W
    r = np.repeat(np.arange(rows), GRID_W).astype(np.float64)
    col = np.tile(np.arange(GRID_W), rows).astype(np.float64)
    quarter = D_MODEL // 4
    freq = 1.0 / (10000.0 ** (np.arange(quarter, dtype=np.float64) / quarter))
    ar = r[:, None] * freq[None, :]
    ac = col[:, None] * freq[None, :]
    return np.concatenate([np.sin(ar), np.cos(ar), np.sin(ac), np.cos(ac)], axis=-1)


def _deltanet_layer(x, h, j, gate, state_f, state_b, dn_w_in, dn_conv_w, dn_a_log, dn_dt_bias, dn_norm_w, dn_w_out):
    tm = tn = 1024

    def project(col0, n_cols, name, epilogue=_ep_plain, extras=(), out_dtype=F32, tn=tn, row_pieces=1):
        return matmul_ws(h, dn_w_in, j, n_cols, tm=tm, tn=tn, epilogue=epilogue, extras=extras,
                         out_dtype=out_dtype, col_block0=col0 // tn, row_pieces=row_pieces, name=name)

    def conv_w_for(col0):
        return (dn_conv_w, pl.BlockSpec((None, DN_CONV, tn), lambda jj, i: (j, 0, col0 // tn + jj)))

    qk = project(0, 2 * DN_K_DIM, "dn_in_qk", extras=(conv_w_for(0),), out_dtype=BF16, row_pieces=tm // SEQ,
                 epilogue=functools.partial(_ep_conv_silu, normalize=True, n_q_tiles=DN_K_DIM // tn))
    v = project(2 * DN_K_DIM, DN_V_DIM, "dn_in_v", extras=(conv_w_for(2 * DN_K_DIM),), row_pieces=tm // SEQ,
                epilogue=functools.partial(_ep_conv_silu, normalize=False, n_q_tiles=0))
    z = project(DN_QKV_DIM, DN_V_DIM, "dn_in_z")
    ab = project(DN_QKV_DIM + DN_V_DIM, DN_GATE_COLS, "dn_in_gates", tn=DN_GATE_COLS)

    zeros = jnp.zeros((DN_HEADS_V,), F32)
    a_log_row = jnp.concatenate([dn_a_log[j, 0], zeros, dn_a_log[j, 1], zeros])[None, :]
    dt_row = jnp.concatenate([dn_dt_bias[j, 0], zeros, dn_dt_bias[j, 1], zeros])[None, :]
    gates = dn_gates(ab, a_log_row, dt_row)
    m = gates.shape[0]
    g5 = gates.reshape(m // DN_CHUNK, DN_CHUNK, 4, DN_HEADS_K, 2)
    gates_row = g5.transpose(3, 0, 2, 4, 1).reshape(DN_HEADS_K, m // DN_CHUNK, 4, 2 * DN_CHUNK)
    gates_row = jnp.pad(gates_row, ((0, 0), (0, 0), (0, SUBLANES - 4), (0, 0)))
    norm_w = dn_norm_w.reshape(dn_norm_w.shape[0], 1, DN_DV)

    og_ctx, s_f, s_b = dn_core(qk, v, z, gates_row, norm_w, j, seq_len=SEQ, n_seq=BATCH, row0=0,
                               heads=4, group=2)
    og_lat = dn_core(qk, v, z, gates_row, norm_w, j, seq_len=DEC_SEQ, n_seq=DEC_BATCH, row0=N_CTX_TOK,
                     heads=2, group=4, init_states=(state_f, state_b))
    x = matmul_resid_gate([og_ctx, og_lat], dn_w_out, j, x, gate, tm=512, tn=512, name="dn_out")
    return x, s_f, s_b


def _chunk_mlp_layer(x, h, j, gate, cm_w_in, cm_b_in, cm_ln_w, cm_ln_b, cm_w_s, cm_b_s, cm_w_out):
    n_in = 2 * CM_DIM
    tn = 1024
    bias = (cm_b_in.reshape(cm_b_in.shape[0], 1, n_in), pl.BlockSpec((None, 1, tn), lambda jj, i: (j, 0, jj)))
    zz = matmul_ws(h, cm_w_in, j, n_in, tm=1024, tn=tn, epilogue=_ep_bias_gelu, extras=(bias,), out_dtype=BF16,
                   name="cm_in")
    n_b = cm_ln_w.shape[0]
    uv = cm_gate(zz, cm_ln_w.reshape(n_b, 1, CM_DIM), cm_ln_b.reshape(n_b, 1, CM_DIM), cm_w_s,
                 cm_b_s[..., None], j)
    return matmul_resid_gate(uv, cm_w_out, j, x, gate, tm=1024, tn=512, name="cm_out")


def kernel(x_prompt, x_sample, state_dn_fwd, state_dn_bwd, c, c_ctx, norm_mix_w, norm_mlp_w, w_ada, b_ada, dn_w_in, dn_conv_w, dn_A_log, dn_dt_bias, dn_norm_w, dn_w_out, cm_w_in, cm_b_in, cm_ln_w, cm_ln_b, cm_w_s, cm_b_s, cm_w_out, w_ff1, w_ff2, final_norm_w):
    cond = jnp.concatenate([c_ctx[None, :], c, jnp.zeros((N_COND - 1 - DEC_BATCH, D_MODEL), F32)], axis=0)
    mod = adaln_all(cond, w_ada, b_ada)
    mod = mod.reshape(DEPTH, N_COND, N_MOD, D_MODEL).transpose(0, 2, 1, 3)[:, :, :, None, :]
    mods = [[mod[i, t] for t in range(N_MOD)] for i in range(DEPTH)]

    x, h = embed_norm_modulate(x_prompt.reshape(N_CTX_TOK, D_MODEL), x_sample.reshape(N_LAT_TOK, D_MODEL),
                               jnp.asarray(_grid_pos_embed(DEC_SEQ), x_sample.dtype), norm_mix_w, 0,
                               mods[0][0], mods[0][1])
    new_fwd, new_bwd = [], []
    for i in range(DEPTH):
        j = i // N_MIXERS
        _, _, gate, shift2, scale2, gate2 = mods[i]
        if i % N_MIXERS == 0:
            x, s_f, s_b = _deltanet_layer(x, h, j, gate, state_dn_fwd, state_dn_bwd, dn_w_in, dn_conv_w,
                                          dn_A_log, dn_dt_bias, dn_norm_w, dn_w_out)
            new_fwd.append(s_f)
            new_bwd.append(s_b)
        else:
            x = _chunk_mlp_layer(x, h, j, gate, cm_w_in, cm_b_in, cm_ln_w, cm_ln_b, cm_w_s, cm_b_s, cm_w_out)
        mlp = functools.partial(ffn, x, norm_mlp_w, i, shift2, scale2, gate2, w_ff1, w_ff2)
        if i < DEPTH - 1:
            x, h = mlp(tail="next_h", tail_w=norm_mix_w[i + 1], tail_mod=mods[i + 1][:2])
        else:
            y_ctx = mlp(tail="final_norm", tail_w=final_norm_w, row0=0, n_rows=N_CTX_TOK)
            y_lat = mlp(tail="final_norm", tail_w=final_norm_w, row0=N_CTX_TOK, n_rows=N_LAT_TOK)

    y_prompt = y_ctx.reshape(BATCH, SEQ, D_MODEL)
    y_sample = y_lat.reshape(DEC_BATCH, DEC_SEQ, D_MODEL)
    return (y_prompt, y_sample, jnp.concatenate(new_fwd, axis=1), jnp.concatenate(new_bwd, axis=1))
```

```python
import functools
import math

import jax
import jax.numpy as jnp
import numpy as np
from jax import lax
from jax.experimental import pallas as pl
from jax.experimental.pallas import tpu as pltpu

F32 = jnp.float32
BF16 = jnp.bfloat16

D_MODEL = 2048
BATCH = 16
SEQ = 256
DEPTH = 2
DEC_BATCH = 2
DEC_SEQ = 1024
GRID_W = 64
N_MIXERS = 2
DN_DK = 128
DN_DV = 128
DN_HEADS_K = D_MODEL // DN_DK
DN_HEADS_V = 2 * DN_HEADS_K
DN_K_DIM = DN_HEADS_K * DN_DK
DN_V_DIM = DN_HEADS_V * DN_DV
DN_QKV_DIM = 2 * DN_K_DIM + DN_V_DIM
DN_GATE_COLS = 4 * DN_HEADS_V
DN_CONV = 5
DN_CHUNK = 64
DN_UNROLLED_SCAN_CHUNKS = 4
CM_DIM = 2 * D_MODEL
CM_CHUNK = 128
CM_GROUPS = 16
CM_GDIM = CM_DIM // CM_GROUPS
FF_DIM = 4 * D_MODEL
N_MOD = 6
EPS = 1e-6

N_CTX_TOK = BATCH * SEQ
N_LAT_TOK = DEC_BATCH * DEC_SEQ
N_TOK = N_CTX_TOK + N_LAT_TOK
N_COND = 8
SUBLANES = 8
CONV_HALO = SUBLANES
VMEM_LIMIT_BYTES = 56 * 1024 * 1024


def _group_of_row(row0):
    return jnp.where(row0 < N_CTX_TOK, 0, 1 + (row0 - N_CTX_TOK) // DEC_SEQ)


def _params(*sem):
    return pltpu.CompilerParams(dimension_semantics=sem, vmem_limit_bytes=VMEM_LIMIT_BYTES)


def _rms(x, w):
    return x * lax.rsqrt(jnp.mean(x * x, axis=-1, keepdims=True) + EPS) * w


def _silu(x):
    half = 0.5 * x
    return half + half * jnp.tanh(half)


def _gelu_tanh(x):
    return 0.5 * x * (1.0 + jnp.tanh(math.sqrt(2.0 / math.pi) * (x + 0.044715 * (x * x * x))))


def _adaln_kernel(c_ref, w_ref, b_ref, o_ref):
    x = _silu(c_ref[...]).astype(BF16)
    acc = jnp.dot(x, w_ref[...].astype(BF16), preferred_element_type=F32)
    o_ref[...] = acc + b_ref[...]


def adaln_all(cond, w_ada, b_ada, tn=1024):
    depth, d, n = w_ada.shape
    return pl.pallas_call(
        _adaln_kernel,
        grid=(depth, n // tn),
        in_specs=[
            pl.BlockSpec((N_COND, d), lambda l, j: (0, 0)),
            pl.BlockSpec((None, d, tn), lambda l, j: (l, 0, j)),
            pl.BlockSpec((None, 1, tn), lambda l, j: (l, 0, j)),
        ],
        out_specs=pl.BlockSpec((None, N_COND, tn), lambda l, j: (l, 0, j)),
        out_shape=jax.ShapeDtypeStruct((depth, N_COND, n), F32),
        compiler_params=_params("arbitrary", "arbitrary"),
        name="adaln",
    )(cond, w_ada, b_ada.reshape(depth, 1, n))


def _embed_kernel(xc_ref, xl_ref, pe_ref, w_ref, sh_ref, sc_ref, x_ref, h_ref, *, ctx_tiles):
    def emit(x):
        x_ref[...] = x
        h_ref[...] = (_rms(x, w_ref[...]) * (1.0 + sc_ref[...]) + sh_ref[...]).astype(BF16)

    i = pl.program_id(0)
    pl.when(i < ctx_tiles)(lambda: emit(xc_ref[...]))
    pl.when(i >= ctx_tiles)(lambda: emit(xl_ref[...] + pe_ref[...]))


def embed_norm_modulate(x_ctx, x_lat, pos_table, norm_w, layer, shift, scale, tm=512):
    d = x_ctx.shape[1]
    ctx_tiles = x_ctx.shape[0] // tm
    lat_tiles = x_lat.shape[0] // tm
    pos_tiles = pos_table.shape[0] // tm
    m = x_ctx.shape[0] + x_lat.shape[0]
    mod_spec = pl.BlockSpec((None, 1, d), lambda i: (_group_of_row(i * tm), 0, 0))
    row_spec = pl.BlockSpec((tm, d), lambda i: (i, 0))
    return pl.pallas_call(
        functools.partial(_embed_kernel, ctx_tiles=ctx_tiles),
        grid=(ctx_tiles + lat_tiles,),
        in_specs=[
            pl.BlockSpec((tm, d), lambda i: (jnp.minimum(i, ctx_tiles - 1), 0)),
            pl.BlockSpec((tm, d), lambda i: (jnp.clip(i - ctx_tiles, 0, lat_tiles - 1), 0)),
            pl.BlockSpec((tm, d), lambda i: (jnp.maximum(i - ctx_tiles, 0) % pos_tiles, 0)),
            pl.BlockSpec((None, 1, d), lambda i: (layer, 0, 0)),
            mod_spec,
            mod_spec,
        ],
        out_specs=[row_spec, row_spec],
        out_shape=[jax.ShapeDtypeStruct((m, d), F32), jax.ShapeDtypeStruct((m, d), BF16)],
        compiler_params=_params("arbitrary"),
        name="embed_norm_modulate",
    )(x_ctx, x_lat, pos_table, norm_w.reshape(norm_w.shape[0], 1, d), shift, scale)


def _mm_kernel(*refs, part_tiles, n_extra, epilogue, row_pieces):
    n_parts = len(part_tiles)
    x_refs = refs[:n_parts]
    w_ref = refs[n_parts]
    extra = refs[n_parts + 1:n_parts + 1 + n_extra]
    o_ref = refs[n_parts + 1 + n_extra]
    wb_ref = refs[n_parts + 2 + n_extra]
    i = pl.program_id(1)

    @pl.when(i == 0)
    def _():
        wb_ref[...] = w_ref[...].astype(BF16)

    def run(x_ref):
        extra_vals = [r[...] for r in extra]
        if row_pieces == 1:
            epilogue(jnp.dot(x_ref[...], wb_ref[...], preferred_element_type=F32), o_ref, *extra_vals)
            return
        rows = x_ref.shape[0] // row_pieces
        accs = []
        for s in range(row_pieces):
            accs.append(jnp.dot(x_ref[s * rows:(s + 1) * rows, :], wb_ref[...], preferred_element_type=F32))
            if s >= 1:
                epilogue(accs, s - 1, o_ref, *extra_vals)
        epilogue(accs, row_pieces - 1, o_ref, *extra_vals)

    if n_parts == 1:
        run(x_refs[0])
    else:
        first = 0
        for x_ref, tiles in zip(x_refs, part_tiles):
            pl.when(jnp.logical_and(i >= first, i < first + tiles))(functools.partial(run, x_ref))
            first += tiles


def matmul_ws(x, w, layer, n_out, *, tm, tn, epilogue, extras=(), out_dtype=F32, col_block0=0, row_pieces=1,
              name="matmul"):
    parts = list(x) if isinstance(x, (list, tuple)) else [x]
    k = parts[0].shape[1]
    part_tiles = [part.shape[0] // tm for part in parts]
    m = sum(part.shape[0] for part in parts)
    in_specs = []
    first = 0
    for tiles in part_tiles:
        in_specs.append(pl.BlockSpec(
            (tm, k), lambda j, i, first=first, tiles=tiles: (jnp.clip(i - first, 0, tiles - 1), 0)))
        first += tiles
    in_specs.append(pl.BlockSpec((None, k, tn), lambda j, i: (layer, 0, j + col_block0)))
    in_specs += [spec for _, spec in extras]
    return pl.pallas_call(
        functools.partial(_mm_kernel, part_tiles=tuple(part_tiles), n_extra=len(extras), epilogue=epilogue,
                          row_pieces=row_pieces),
        grid=(n_out // tn, m // tm),
        in_specs=in_specs,
        out_specs=pl.BlockSpec((tm, tn), lambda j, i: (i, j)),
        out_shape=jax.ShapeDtypeStruct((m, n_out), out_dtype),
        scratch_shapes=[pltpu.VMEM((k, tn), BF16)],
        compiler_params=_params("arbitrary", "arbitrary"),
        name=name,
    )(*parts, w, *[a for a, _ in extras])


def _ep_plain(acc, o_ref):
    o_ref[...] = acc.astype(o_ref.dtype)


def _ep_bias_gelu(acc, o_ref, b):
    o_ref[...] = _gelu_tanh(acc + b).astype(o_ref.dtype)


def _ep_resid_gate(acc, o_ref, resid, gate):
    o_ref[...] = (resid + gate * acc).astype(o_ref.dtype)


def _ep_conv_silu(accs, s, o_ref, w, *, normalize, n_q_tiles):
    pieces = o_ref.shape[0] // SEQ
    tn = o_ref.shape[1]
    assert o_ref.shape[0] == DEC_SEQ and accs[s].shape[0] == SEQ
    every_piece_is_a_sequence = pl.program_id(1) * o_ref.shape[0] < N_CTX_TOK
    pad = DN_CONV // 2
    n_ext = SEQ + 2 * CONV_HALO
    zeros = jnp.zeros((CONV_HALO, tn), F32)
    prev = zeros if s == 0 else jnp.where(every_piece_is_a_sequence, 0.0, accs[s - 1][SEQ - CONV_HALO:])
    nxt = zeros if s == pieces - 1 else jnp.where(every_piece_is_a_sequence, 0.0, accs[s + 1][:CONV_HALO])
    ext = jnp.concatenate([prev, accs[s], nxt], axis=0)
    y = None
    for t in range(DN_CONV):
        shifted = ext if t == pad else pltpu.roll(ext, (pad - t) % n_ext, 0)
        term = shifted[CONV_HALO:CONV_HALO + SEQ] * w[t:t + 1, :]
        y = term if y is None else y + term
    y = _silu(y)
    rows = slice(s * SEQ, (s + 1) * SEQ)
    if normalize:
        scale = jnp.where(pl.program_id(0) < n_q_tiles, DN_DK ** -0.5, 1.0)
        for hh in range(tn // DN_DK):
            lanes = slice(hh * DN_DK, (hh + 1) * DN_DK)
            sl = y[:, lanes]
            inv = lax.rsqrt(jnp.sum(sl * sl, axis=-1, keepdims=True) + EPS) * scale
            o_ref[rows, lanes] = (sl * inv).astype(o_ref.dtype)
    else:
        o_ref[rows, :] = y.astype(o_ref.dtype)


def matmul_resid_gate(x, w, layer, resid, gate, *, tm, tn, name):
    n_out = resid.shape[1]
    extras = (
        (resid, pl.BlockSpec((tm, tn), lambda j, i: (i, j))),
        (gate, pl.BlockSpec((None, 1, tn), lambda j, i: (_group_of_row(i * tm), 0, j))),
    )
    return matmul_ws(x, w, layer, n_out, tm=tm, tn=tn, epilogue=_ep_resid_gate, extras=extras, name=name)


def _split3(x):
    hi = x.astype(BF16)
    r1 = x - hi.astype(F32)
    mid = r1.astype(BF16)
    lo = (r1 - mid.astype(F32)).astype(BF16)
    return hi, mid, lo


def _gates_kernel(ab_ref, alog_ref, dtb_ref, o_ref):
    c = DN_CHUNK
    ii = lax.broadcasted_iota(jnp.int32, (c, c), 0)
    jj = lax.broadcasted_iota(jnp.int32, (c, c), 1)
    tril = (ii >= jj).astype(BF16)
    triu = (ii <= jj).astype(BF16)
    lane = lax.broadcasted_iota(jnp.int32, (c, ab_ref.shape[1]), 1)
    kind = lane // DN_HEADS_V
    for n in range(ab_ref.shape[0] // c):
        ab = ab_ref[n * c:(n + 1) * c, :]
        log_g = -jnp.exp(alog_ref[...]) * jax.nn.softplus(ab + dtb_ref[...])
        beta = jax.nn.sigmoid(ab)
        parts = _split3(log_g)
        cum_f = sum(jnp.dot(tril, part, preferred_element_type=F32) for part in parts)
        cum_b = sum(jnp.dot(triu, part, preferred_element_type=F32) for part in parts)
        o_ref[n * c:(n + 1) * c, :] = jnp.where(kind == 0, cum_f, jnp.where(kind == 2, cum_b, beta))


def dn_gates(ab, a_log_row, dt_bias_row, chunks_per_step=8):
    m, n = ab.shape
    rows = chunks_per_step * DN_CHUNK
    return pl.pallas_call(
        _gates_kernel,
        grid=(m // rows,),
        in_specs=[
            pl.BlockSpec((rows, n), lambda i: (i, 0)),
            pl.BlockSpec((1, n), lambda i: (0, 0)),
            pl.BlockSpec((1, n), lambda i: (0, 0)),
        ],
        out_specs=pl.BlockSpec((rows, n), lambda i: (i, 0)),
        out_shape=jax.ShapeDtypeStruct((m, n), F32),
        compiler_params=_params("arbitrary"),
        name="dn_gates",
    )(ab, a_log_row, dt_bias_row)


def _dot(a, b):
    return jnp.dot(a, b, preferred_element_type=F32)


def _inv_unit_triangular_many(a_list):
    n = a_list[0].shape[0]
    ii = lax.broadcasted_iota(jnp.int32, (n, n), 0)
    jj = lax.broadcasted_iota(jnp.int32, (n, n), 1)
    eye = jnp.where(ii == jj, 1.0, 0.0)
    xs = [eye - jnp.where((ii >> 1) == (jj >> 1), a, 0.0) for a in a_list]
    for level in range(1, int(math.log2(DN_CHUNK))):
        joins = jnp.logical_and((ii >> (level + 1)) == (jj >> (level + 1)), (ii >> level) != (jj >> level))
        ns = [jnp.where(joins, a, 0.0).astype(BF16) for a in a_list]
        xbs = [x.astype(BF16) for x in xs]
        ys = [_dot(xb, nn).astype(BF16) for xb, nn in zip(xbs, ns)]
        xs = [x - _dot(y, xb) for x, y, xb in zip(xs, ys, xbs)]
    return xs


def _dn_core_kernel(*refs, n_chunks, group, heads, has_init, has_state_out):
    q_ref, k_ref, v_ref, z_ref, gr_ref, nw_ref = refs[:6]
    pos = 6
    if has_init:
        s0f_ref, s0b_ref = refs[pos:pos + 2]
        pos += 2
    og_ref = refs[pos]
    pos += 1
    if has_state_out:
        sf_ref, sb_ref = refs[pos:pos + 2]
        pos += 2
    wm_ref, cm_ref, egl_ref, p_ref, u_ref, wv_ref, sh_ref, s_ref = refs[pos:pos + 8]

    c = DN_CHUNK
    pc = 2 * c
    dv = DN_DV
    if has_init:
        for hl in range(heads):
            s_ref[2 * hl] = jnp.concatenate([s0f_ref[2 * hl], s0f_ref[2 * hl + 1]], axis=1)
            s_ref[2 * hl + 1] = jnp.concatenate([s0b_ref[2 * hl], s0b_ref[2 * hl + 1]], axis=1)
    else:
        s_ref[...] = jnp.zeros_like(s_ref)

    ii = lax.broadcasted_iota(jnp.int32, (pc, pc), 0)
    jj = lax.broadcasted_iota(jnp.int32, (pc, pc), 1)
    chunk_shift = int(math.log2(c))
    same_head = (ii >> chunk_shift) == (jj >> chunk_shift)
    top = ii < c
    top_col = lax.broadcasted_iota(jnp.int32, (pc, 1), 0) < c
    nt_dims = (((1,), (1,)), ((), ()))

    def rows_of(chunk):
        start = chunk * c
        return pl.ds(start if isinstance(start, int) else pl.multiple_of(start, c), c)

    def gate_columns(units):
        return [gr_ref[hl, ch].T for hl, ch in units]

    def entry(hl, chunk, d):
        return (hl * n_chunks + chunk) * 2 + d

    def qk_lanes(hl):
        return slice(hl * DN_DK, (hl + 1) * DN_DK)

    def v_lanes(hl, hh):
        return slice((2 * hl + hh) * dv, (2 * hl + hh + 1) * dv)

    def phase_a(chunks):
        units = [(hl, ch) for hl in range(heads) for ch in chunks]
        chains = [(n, d) for n in range(len(units)) for d in (0, 1)]
        k2s, grams = [], []
        for hl, ch in units:
            kc = k_ref[rows_of(ch), qk_lanes(hl)]
            qc = q_ref[rows_of(ch), qk_lanes(hl)]
            k2 = jnp.concatenate([kc, kc], axis=0)
            k2s.append(k2)
            lhs = jnp.concatenate([kc, kc, qc, qc], axis=0)
            grams.append(lax.dot_general(lhs, k2, nt_dims, preferred_element_type=F32))
        colvs = gate_columns(units)
        a_mats, p_mats, gcols, bcols = [], [], [], []
        for n, d in chains:
            hl, ch = units[n]
            gcol, bcol = colvs[n][:, 2 * d:2 * d + 1], colvs[n][:, 2 * d + 1:2 * d + 2]
            grow = gr_ref[hl, ch][2 * d:2 * d + 1, :]
            incl = jnp.logical_and(same_head, (ii >= jj) if d == 0 else (ii <= jj))
            strict = jnp.logical_and(same_head, (ii > jj) if d == 0 else (ii < jj))
            decay = jnp.exp(jnp.where(incl, gcol - grow, -1e30))
            a_mats.append(jnp.where(strict, bcol * grams[n][:pc] * decay, 0.0))
            p_mats.append(grams[n][pc:] * decay)
            gcols.append(gcol)
            bcols.append(bcol)
        t_invs = _inv_unit_triangular_many(a_mats)
        rhss, kfs = [], []
        for (n, d), gcol, bcol in zip(chains, gcols, bcols):
            hl, ch = units[n]
            kf = k2s[n].astype(F32)
            rows = rows_of(ch)
            vp = jnp.concatenate([v_ref[rows, v_lanes(hl, 0)], v_ref[rows, v_lanes(hl, 1)]], axis=0)
            rhss.append(jnp.concatenate([(bcol * jnp.exp(gcol)) * kf, bcol * vp], axis=1).astype(BF16))
            kfs.append(kf)
        uws = [_dot(t.astype(BF16), r) for t, r in zip(t_invs, rhss)]
        kdts, x2s, egls = [], [], []
        for (n, d), gcol, kf, uw in zip(chains, gcols, kfs, uws):
            r0 = c - 1 if d == 0 else 0
            gl0 = gcol[r0:r0 + 1]
            gl1 = gcol[c + r0:c + r0 + 1]
            gl = jnp.where(top_col, gl0, gl1)
            kdts.append((kf * jnp.exp(gl - gcol)).T.astype(BF16))
            u, wv = uw[:, :dv], uw[:, dv:]
            x2s.append(jnp.concatenate([jnp.where(top, u, 0.0), jnp.where(top, 0.0, u),
                                        jnp.where(top, wv, 0.0), jnp.where(top, 0.0, wv)], axis=1).astype(BF16))
            egls.append(jnp.concatenate([jnp.broadcast_to(jnp.exp(gl0), (1, dv)),
                                         jnp.broadcast_to(jnp.exp(gl1), (1, dv))], axis=1))
        wcs = [_dot(kdt, x2) for kdt, x2 in zip(kdts, x2s)]
        for (n, d), p_mat, uw, wc, egl in zip(chains, p_mats, uws, wcs, egls):
            e = entry(*units[n], d)
            wm_ref[e] = wc[:, :2 * dv].astype(BF16)
            cm_ref[e] = wc[:, 2 * dv:]
            egl_ref[e] = egl
            p_ref[e] = p_mat.astype(BF16)
            u_ref[e] = uw[:, :dv].astype(BF16)
            wv_ref[e] = uw[:, dv:]

    def phase_b(n, carry):
        lines = [(hl, d) for hl in range(heads) for d in (0, 1)]
        es = [entry(hl, n if d == 0 else n_chunks - 1 - n, d) for hl, d in lines]
        ss = [s_ref[2 * hl + d] for hl, d in lines]
        sbs = [s.astype(BF16) for s in ss]
        for e, sb in zip(es, sbs):
            sh_ref[e] = sb
        wms = [wm_ref[e] for e in es]
        wss = [jnp.concatenate([_dot(wm[:, :dv], sb[:, :dv]), _dot(wm[:, dv:], sb[:, dv:])], axis=1)
               for wm, sb in zip(wms, sbs)]
        for (hl, d), e, s, ws in zip(lines, es, ss, wss):
            s_ref[2 * hl + d] = egl_ref[e] * s - ws + cm_ref[e]
        return carry

    def phase_c(chunks):
        units = [(hl, ch) for hl in range(heads) for ch in chunks]
        chains = [(n, d) for n in range(len(units)) for d in (0, 1)]
        es = [entry(*units[n], d) for n, d in chains]
        rs = []
        for (n, d), e in zip(chains, es):
            hl, ch = units[n]
            lhs = jnp.concatenate([q_ref[rows_of(ch), qk_lanes(hl)], u_ref[e]], axis=0)
            rs.append(_dot(lhs, sh_ref[e]))
        v_news, q_ss = [], []
        for e, r in zip(es, rs):
            q_ss.append(jnp.concatenate([r[:c, :dv], r[:c, dv:]], axis=0))
            u_s = jnp.concatenate([r[c:2 * c, :dv], r[2 * c:, dv:]], axis=0)
            v_news.append((wv_ref[e] - u_s).astype(BF16))
        pvs = [_dot(p_ref[e], v_new) for e, v_new in zip(es, v_news)]
        colvs = gate_columns(units)
        outs = [jnp.exp(colvs[n][:, 2 * d:2 * d + 1]) * q_s + pv for (n, d), q_s, pv in zip(chains, q_ss, pvs)]
        for n, (hl, ch) in enumerate(units):
            o = outs[2 * n] + outs[2 * n + 1]
            rows = rows_of(ch)
            for hh in range(2):
                y = _rms(o[hh * c:(hh + 1) * c], nw_ref[...])
                og_ref[rows, v_lanes(hl, hh)] = (y * _silu(z_ref[rows, v_lanes(hl, hh)])).astype(og_ref.dtype)

    def over_groups(phase):
        def body(g, carry):
            phase([g * group + i for i in range(group)])
            return carry
        lax.fori_loop(0, n_chunks // group, body, 0)

    if n_chunks <= DN_UNROLLED_SCAN_CHUNKS:
        ready = set()
        for n in range(n_chunks):
            needed = sorted({n, n_chunks - 1 - n} - ready)
            if needed:
                phase_a(needed)
                ready.update(needed)
            phase_b(n, 0)
        phase_c(list(range(n_chunks)))
    else:
        over_groups(phase_a)
        for n in range(n_chunks):
            phase_b(n, 0)
        phase_c(list(range(n_chunks)))

    if has_state_out:
        for hl in range(heads):
            for hh in range(2):
                sf_ref[2 * hl + hh] = s_ref[2 * hl][:, hh * dv:(hh + 1) * dv]
                sb_ref[2 * hl + hh] = s_ref[2 * hl + 1][:, hh * dv:(hh + 1) * dv]


def dn_core(qk, v, z, gates_row, norm_w, layer, *, seq_len, n_seq, row0, heads, group, init_states=None):
    n_chunks = seq_len // DN_CHUNK
    rb0 = row0 // seq_len
    qkw = heads * DN_DK
    pair = 2 * DN_DV
    vw = heads * pair
    pc = 2 * DN_CHUNK
    n_e = 2 * n_chunks * heads
    assert n_chunks % group == 0 and DN_HEADS_K % heads == 0
    has_init = init_states is not None
    in_specs = [
        pl.BlockSpec((seq_len, qkw), lambda s, h: (rb0 + s, h)),
        pl.BlockSpec((seq_len, qkw), lambda s, h: (rb0 + s, DN_HEADS_K // heads + h)),
        pl.BlockSpec((seq_len, vw), lambda s, h: (rb0 + s, h)),
        pl.BlockSpec((seq_len, vw), lambda s, h: (rb0 + s, h)),
        pl.BlockSpec((heads, n_chunks, SUBLANES, 2 * DN_CHUNK), lambda s, h: (h, rb0 + s, 0, 0)),
        pl.BlockSpec((None, 1, DN_DV), lambda s, h: (layer, 0, 0)),
    ]
    args = [qk, qk, v, z, gates_row, norm_w]
    og_shape = jax.ShapeDtypeStruct((n_seq * seq_len, DN_V_DIM), BF16)
    og_spec = pl.BlockSpec((seq_len, vw), lambda s, h: (s, h))
    if has_init:
        st_spec = pl.BlockSpec((None, None, 2 * heads, DN_DK, DN_DV), lambda s, h: (s, layer, h, 0, 0))
        in_specs += [st_spec, st_spec]
        args += [init_states[0], init_states[1]]
        out_specs = og_spec
        out_shape = og_shape
    else:
        so_spec = pl.BlockSpec((None, None, 2 * heads, DN_DK, DN_DV), lambda s, h: (s, 0, h, 0, 0))
        so_shape = jax.ShapeDtypeStruct((n_seq, 1, DN_HEADS_V, DN_DK, DN_DV), F32)
        out_specs = [og_spec, so_spec, so_spec]
        out_shape = [og_shape, so_shape, so_shape]
    return pl.pallas_call(
        functools.partial(_dn_core_kernel, n_chunks=n_chunks, group=group, heads=heads, has_init=has_init,
                          has_state_out=not has_init),
        grid=(n_seq, DN_HEADS_K // heads),
        in_specs=in_specs,
        out_specs=out_specs,
        out_shape=out_shape,
        scratch_shapes=[
            pltpu.VMEM((n_e, DN_DK, pair), BF16),
            pltpu.VMEM((n_e, DN_DK, pair), F32),
            pltpu.VMEM((n_e, 1, pair), F32),
            pltpu.VMEM((n_e, pc, pc), BF16),
            pltpu.VMEM((n_e, pc, DN_DK), BF16),
            pltpu.VMEM((n_e, pc, DN_DV), F32),
            pltpu.VMEM((n_e, DN_DK, pair), BF16),
            pltpu.VMEM((2 * heads, DN_DK, pair), F32),
        ],
        compiler_params=_params("arbitrary", "arbitrary"),
        name="dn_core_lat" if has_init else "dn_core_ctx",
    )(*args)


def _cm_gate_kernel(u_ref, v_ref, lw_ref, lb_ref, ws_ref, bs_ref, o_ref):
    for n in range(u_ref.shape[0] // CM_CHUNK):
        rows = slice(n * CM_CHUNK, (n + 1) * CM_CHUNK)
        v = v_ref[rows, :].astype(F32)
        mu = jnp.mean(v, axis=-1, keepdims=True)
        vc = v - mu
        vn = vc * lax.rsqrt(jnp.mean(vc * vc, axis=-1, keepdims=True) + EPS) * lw_ref[...] + lb_ref[...]
        vn = vn.astype(BF16)
        for g in range(CM_GROUPS):
            lanes = slice(g * CM_GDIM, (g + 1) * CM_GDIM)
            sp = jnp.dot(ws_ref[g].astype(BF16), vn[:, lanes], preferred_element_type=F32) + bs_ref[g]
            o_ref[rows, lanes] = (u_ref[rows, lanes].astype(F32) * sp).astype(o_ref.dtype)


def cm_gate(zz, ln_w, ln_b, w_s, b_s_col, layer, chunks_per_step=2):
    m = zz.shape[0]
    rows = chunks_per_step * CM_CHUNK
    row = pl.BlockSpec((None, 1, CM_DIM), lambda i: (layer, 0, 0))
    return pl.pallas_call(
        _cm_gate_kernel,
        grid=(m // rows,),
        in_specs=[
            pl.BlockSpec((rows, CM_DIM), lambda i: (i, 0)),
            pl.BlockSpec((rows, CM_DIM), lambda i: (i, 1)),
            row,
            row,
            pl.BlockSpec((None, CM_GROUPS, CM_CHUNK, CM_CHUNK), lambda i: (layer, 0, 0, 0)),
            pl.BlockSpec((None, CM_GROUPS, CM_CHUNK, 1), lambda i: (layer, 0, 0, 0)),
        ],
        out_specs=pl.BlockSpec((rows, CM_DIM), lambda i: (i, 0)),
        out_shape=jax.ShapeDtypeStruct((m, CM_DIM), BF16),
        compiler_params=_params("arbitrary"),
        name="cm_gate",
    )(zz, zz, ln_w, ln_b, w_s, b_s_col)


def _ffn_kernel(*refs, tail):
    x_ref, nw_ref, sh_ref, sc_ref, g_ref, w1_ref, w2_ref, tw_ref = refs[:8]
    if tail == "next_h":
        tsh_ref, tsc_ref, o_ref, hn_ref, h_ref = refs[8:]
    else:
        o_ref, h_ref = refs[8:]
    kk = pl.program_id(1)

    @pl.when(kk == 0)
    def _():
        y = _rms(x_ref[...], nw_ref[...])
        h_ref[...] = (y * (1.0 + sc_ref[...]) + sh_ref[...]).astype(BF16)
        o_ref[...] = jnp.zeros_like(o_ref)

    a = jnp.dot(h_ref[...], w1_ref[...].astype(BF16), preferred_element_type=F32)
    a = jnp.square(jnp.maximum(a, 0.0)).astype(BF16)
    o_ref[...] += jnp.dot(a, w2_ref[...].astype(BF16), preferred_element_type=F32)

    @pl.when(kk == pl.num_programs(1) - 1)
    def _():
        r = x_ref[...] + g_ref[...] * o_ref[...]
        if tail == "final_norm":
            r = _rms(r, tw_ref[...])
        else:
            hn_ref[...] = (_rms(r, tw_ref[...]) * (1.0 + tsc_ref[...]) + tsh_ref[...]).astype(BF16)
        o_ref[...] = r


def ffn(x, norm_w, layer, shift, scale, gate, w1, w2, *, tail, tail_w, tail_mod=None, row0=0, n_rows=None,
        tm=1024, tk=512):
    d = x.shape[1]
    n_rows = x.shape[0] if n_rows is None else n_rows
    ff = w1.shape[2]
    t0 = row0 // tm
    mod_spec = pl.BlockSpec((None, 1, d), lambda i, k: (_group_of_row((t0 + i) * tm), 0, 0))
    row_spec = pl.BlockSpec((tm, d), lambda i, k: (i, 0), pipeline_mode=pl.Buffered(1))
    in_specs = [
        pl.BlockSpec((tm, d), lambda i, k: (t0 + i, 0), pipeline_mode=pl.Buffered(1)),
        pl.BlockSpec((None, 1, d), lambda i, k: (layer, 0, 0)),
        mod_spec,
        mod_spec,
        mod_spec,
        pl.BlockSpec((None, d, tk), lambda i, k: (layer, 0, k)),
        pl.BlockSpec((None, tk, d), lambda i, k: (layer, k, 0)),
        pl.BlockSpec((1, d), lambda i, k: (0, 0)),
    ]
    args = [x, norm_w.reshape(norm_w.shape[0], 1, d), shift, scale, gate, w1, w2, tail_w.reshape(1, d)]
    out_specs, out_shape = row_spec, jax.ShapeDtypeStruct((n_rows, d), F32)
    if tail == "next_h":
        in_specs += [mod_spec, mod_spec]
        args += list(tail_mod)
        out_specs = [row_spec, row_spec]
        out_shape = [out_shape, jax.ShapeDtypeStruct((n_rows, d), BF16)]
    return pl.pallas_call(
        functools.partial(_ffn_kernel, tail=tail),
        grid=(n_rows // tm, ff // tk),
        in_specs=in_specs,
        out_specs=out_specs,
        out_shape=out_shape,
        scratch_shapes=[pltpu.VMEM((tm, d), BF16)],
        compiler_params=_params("arbitrary", "arbitrary"),
        name="ffn_" + tail,
    )(*args)


def _grid_pos_embed(n_tokens):
    rows = n_tokens // GRID_W
    r = np.repeat(np.arange(rows), GRID_W).astype(np.float64)
    col = np.tile(np.arange(GRID_W), rows).astype(np.float64)
    quarter = D_MODEL // 4
    freq = 1.0 / (10000.0 ** (np.arange(quarter, dtype=np.float64) / quarter))
    ar = r[:, None] * freq[None, :]
    ac = col[:, None] * freq[None, :]
    return np.concatenate([np.sin(ar), np.cos(ar), np.sin(ac), np.cos(ac)], axis=-1)


def _deltanet_layer(x, h, j, gate, state_f, state_b, dn_w_in, dn_conv_w, dn_a_log, dn_dt_bias, dn_norm_w, dn_w_out):
    tm = tn = 1024

    def project(col0, n_cols, name, epilogue=_ep_plain, extras=(), out_dtype=F32, tn=tn, row_pieces=1):
        return matmul_ws(h, dn_w_in, j, n_cols, tm=tm, tn=tn, epilogue=epilogue, extras=extras,
                         out_dtype=out_dtype, col_block0=col0 // tn, row_pieces=row_pieces, name=name)

    def conv_w_for(col0):
        return (dn_conv_w, pl.BlockSpec((None, DN_CONV, tn), lambda jj, i: (j, 0, col0 // tn + jj)))

    qk = project(0, 2 * DN_K_DIM, "dn_in_qk", extras=(conv_w_for(0),), out_dtype=BF16, row_pieces=tm // SEQ,
                 epilogue=functools.partial(_ep_conv_silu, normalize=True, n_q_tiles=DN_K_DIM // tn))
    v = project(2 * DN_K_DIM, DN_V_DIM, "dn_in_v", extras=(conv_w_for(2 * DN_K_DIM),), row_pieces=tm // SEQ,
                epilogue=functools.partial(_ep_conv_silu, normalize=False, n_q_tiles=0))
    z = project(DN_QKV_DIM, DN_V_DIM, "dn_in_z")
    ab = project(DN_QKV_DIM + DN_V_DIM, DN_GATE_COLS, "dn_in_gates", tn=DN_GATE_COLS)

    zeros = jnp.zeros((DN_HEADS_V,), F32)
    a_log_row = jnp.concatenate([dn_a_log[j, 0], zeros, dn_a_log[j, 1], zeros])[None, :]
    dt_row = jnp.concatenate([dn_dt_bias[j, 0], zeros, dn_dt_bias[j, 1], zeros])[None, :]
    gates = dn_gates(ab, a_log_row, dt_row)
    m = gates.shape[0]
    g5 = gates.reshape(m // DN_CHUNK, DN_CHUNK, 4, DN_HEADS_K, 2)
    gates_row = g5.transpose(3, 0, 2, 4, 1).reshape(DN_HEADS_K, m // DN_CHUNK, 4, 2 * DN_CHUNK)
    gates_row = jnp.pad(gates_row, ((0, 0), (0, 0), (0, SUBLANES - 4), (0, 0)))
    norm_w = dn_norm_w.reshape(dn_norm_w.shape[0], 1, DN_DV)

    og_ctx, s_f, s_b = dn_core(qk, v, z, gates_row, norm_w, j, seq_len=SEQ, n_seq=BATCH, row0=0,
                               heads=4, group=2)
    og_lat = dn_core(qk, v, z, gates_row, norm_w, j, seq_len=DEC_SEQ, n_seq=DEC_BATCH, row0=N_CTX_TOK,
                     heads=2, group=4, init_states=(state_f, state_b))
    x = matmul_resid_gate([og_ctx, og_lat], dn_w_out, j, x, gate, tm=512, tn=512, name="dn_out")
    return x, s_f, s_b


def _chunk_mlp_layer(x, h, j, gate, cm_w_in, cm_b_in, cm_ln_w, cm_ln_b, cm_w_s, cm_b_s, cm_w_out):
    n_in = 2 * CM_DIM
    tn = 1024
    bias = (cm_b_in.reshape(cm_b_in.shape[0], 1, n_in), pl.BlockSpec((None, 1, tn), lambda jj, i: (j, 0, jj)))
    zz = matmul_ws(h, cm_w_in, j, n_in, tm=1024, tn=tn, epilogue=_ep_bias_gelu, extras=(bias,), out_dtype=BF16,
                   name="cm_in")
    n_b = cm_ln_w.shape[0]
    uv = cm_gate(zz, cm_ln_w.reshape(n_b, 1, CM_DIM), cm_ln_b.reshape(n_b, 1, CM_DIM), cm_w_s,
                 cm_b_s[..., None], j)
    return matmul_resid_gate(uv, cm_w_out, j, x, gate, tm=1024, tn=512, name="cm_out")


def kernel(x_prompt, x_sample, state_dn_fwd, state_dn_bwd, c, c_ctx, norm_mix_w, norm_mlp_w, w_ada, b_ada, dn_w_in, dn_conv_w, dn_A_log, dn_dt_bias, dn_norm_w, dn_w_out, cm_w_in, cm_b_in, cm_ln_w, cm_ln_b, cm_w_s, cm_b_s, cm_w_out, w_ff1, w_ff2, final_norm_w):
    cond = jnp.concatenate([c_ctx[None, :], c, jnp.zeros((N_COND - 1 - DEC_BATCH, D_MODEL), F32)], axis=0)
    mod = adaln_all(cond, w_ada, b_ada)
    mod = mod.reshape(DEPTH, N_COND, N_MOD, D_MODEL).transpose(0, 2, 1, 3)[:, :, :, None, :]
    mods = [[mod[i, t] for t in range(N_MOD)] for i in range(DEPTH)]

    x, h = embed_norm_modulate(x_prompt.reshape(N_CTX_TOK, D_MODEL), x_sample.reshape(N_LAT_TOK, D_MODEL),
                               jnp.asarray(_grid_pos_embed(DEC_SEQ), x_sample.dtype), norm_mix_w, 0,
                               mods[0][0], mods[0][1])
    new_fwd, new_bwd = [], []
    for i in range(DEPTH):
        j = i // N_MIXERS
        _, _, gate, shift2, scale2, gate2 = mods[i]
        if i % N_MIXERS == 0:
            x, s_f, s_b = _deltanet_layer(x, h, j, gate, state_dn_fwd, state_dn_bwd, dn_w_in, dn_conv_w,
                                          dn_A_log, dn_dt_bias, dn_norm_w, dn_w_out)
            new_fwd.append(s_f)
            new_bwd.append(s_b)
        else:
            x = _chunk_mlp_layer(x, h, j, gate, cm_w_in, cm_b_in, cm_ln_w, cm_ln_b, cm_w_s, cm_b_s, cm_w_out)
        mlp = functools.partial(ffn, x, norm_mlp_w, i, shift2, scale2, gate2, w_ff1, w_ff2)
        if i < DEPTH - 1:
            x, h = mlp(tail="next_h", tail_w=norm_mix_w[i + 1], tail_mod=mods[i + 1][:2])
        else:
            y_ctx = mlp(tail="final_norm", tail_w=final_norm_w, row0=0, n_rows=N_CTX_TOK)
            y_lat = mlp(tail="final_norm", tail_w=final_norm_w, row0=N_CTX_TOK, n_rows=N_LAT_TOK)

    y_prompt = y_ctx.reshape(BATCH, SEQ, D_MODEL)
    y_sample = y_lat.reshape(DEC_BATCH, DEC_SEQ, D_MODEL)
    return (y_prompt, y_sample, jnp.concatenate(new_fwd, axis=1), jnp.concatenate(new_bwd, axis=1))
```

```python
import functools
import math

import jax
import jax.numpy as jnp
import numpy as np
from jax import lax
from jax.experimental import pallas as pl
from jax.experimental.pallas import tpu as pltpu

F32 = jnp.float32
BF16 = jnp.bfloat16

D_MODEL = 2048
BATCH = 16
SEQ = 256
DEPTH = 2
DEC_BATCH = 2
DEC_SEQ = 1024
GRID_W = 64
N_MIXERS = 2
DN_DK = 128
DN_DV = 128
DN_HEADS_K = D_MODEL // DN_DK
DN_HEADS_V = 2 * DN_HEADS_K
DN_K_DIM = DN_HEADS_K * DN_DK
DN_V_DIM = DN_HEADS_V * DN_DV
DN_QKV_DIM = 2 * DN_K_DIM + DN_V_DIM
DN_GATE_COLS = 4 * DN_HEADS_V
DN_CONV = 5
DN_CHUNK = 64
DN_UNROLLED_SCAN_CHUNKS = 4
CM_DIM = 2 * D_MODEL
CM_CHUNK = 128
CM_GROUPS = 16
CM_GDIM = CM_DIM // CM_GROUPS
FF_DIM = 4 * D_MODEL
N_MOD = 6
EPS = 1e-6

N_CTX_TOK = BATCH * SEQ
N_LAT_TOK = DEC_BATCH * DEC_SEQ
N_TOK = N_CTX_TOK + N_LAT_TOK
N_COND = 8
SUBLANES = 8
CONV_HALO = SUBLANES
MASKED_LOG_DECAY = -1e30
V7X_VMEM_BYTES = 64 * 1024 * 1024
VMEM_LIMIT_BYTES = V7X_VMEM_BYTES * 7 // 8

ROW_TILE = 1024
PROJ_COL_TILE = 1024
OUT_COL_TILE = 512
DN_OUT_ROW_TILE = 512
FFN_K_TILE = 512
EPILOGUE_ROW_PIECES = 8
ADALN_COL_TILE = 1024
EMBED_ROW_TILE = 512
GATES_CHUNKS_PER_STEP = 8
CM_GATE_CHUNKS_PER_STEP = 2


def _group_of_row(row0):
    return jnp.where(row0 < N_CTX_TOK, 0, 1 + (row0 - N_CTX_TOK) // DEC_SEQ)


def _params(*sem):
    return pltpu.CompilerParams(dimension_semantics=sem, vmem_limit_bytes=VMEM_LIMIT_BYTES)


def _rms(x, w):
    return x * lax.rsqrt(jnp.mean(x * x, axis=-1, keepdims=True) + EPS) * w


def _silu(x):
    half = 0.5 * x
    return half + half * jnp.tanh(half)


def _gelu_tanh(x):
    return 0.5 * x * (1.0 + jnp.tanh(math.sqrt(2.0 / math.pi) * (x + 0.044715 * (x * x * x))))


def _adaln_kernel(c_ref, w_ref, b_ref, o_ref):
    x = _silu(c_ref[...]).astype(BF16)
    acc = jnp.dot(x, w_ref[...].astype(BF16), preferred_element_type=F32)
    o_ref[...] = acc + b_ref[...]


def adaln_all(cond, w_ada, b_ada, tn=ADALN_COL_TILE):
    depth, d, n = w_ada.shape
    return pl.pallas_call(
        _adaln_kernel,
        grid=(depth, n // tn),
        in_specs=[
            pl.BlockSpec((N_COND, d), lambda l, j: (0, 0)),
            pl.BlockSpec((None, d, tn), lambda l, j: (l, 0, j)),
            pl.BlockSpec((None, 1, tn), lambda l, j: (l, 0, j)),
        ],
        out_specs=pl.BlockSpec((None, N_COND, tn), lambda l, j: (l, 0, j)),
        out_shape=jax.ShapeDtypeStruct((depth, N_COND, n), F32),
        compiler_params=_params("arbitrary", "arbitrary"),
        name="adaln",
    )(cond, w_ada, b_ada.reshape(depth, 1, n))


def _embed_kernel(xc_ref, xl_ref, pe_ref, w_ref, sh_ref, sc_ref, x_ref, h_ref, *, ctx_tiles):
    def emit(x):
        x_ref[...] = x
        h_ref[...] = (_rms(x, w_ref[...]) * (1.0 + sc_ref[...]) + sh_ref[...]).astype(BF16)

    i = pl.program_id(0)
    pl.when(i < ctx_tiles)(lambda: emit(xc_ref[...]))
    pl.when(i >= ctx_tiles)(lambda: emit(xl_ref[...] + pe_ref[...]))


def embed_norm_modulate(x_ctx, x_lat, pos_table, norm_w, layer, shift, scale, tm=EMBED_ROW_TILE):
    d = x_ctx.shape[1]
    ctx_tiles = x_ctx.shape[0] // tm
    lat_tiles = x_lat.shape[0] // tm
    pos_tiles = pos_table.shape[0] // tm
    m = x_ctx.shape[0] + x_lat.shape[0]
    mod_spec = pl.BlockSpec((None, 1, d), lambda i: (_group_of_row(i * tm), 0, 0))
    row_spec = pl.BlockSpec((tm, d), lambda i: (i, 0))
    return pl.pallas_call(
        functools.partial(_embed_kernel, ctx_tiles=ctx_tiles),
        grid=(ctx_tiles + lat_tiles,),
        in_specs=[
            pl.BlockSpec((tm, d), lambda i: (jnp.minimum(i, ctx_tiles - 1), 0)),
            pl.BlockSpec((tm, d), lambda i: (jnp.clip(i - ctx_tiles, 0, lat_tiles - 1), 0)),
            pl.BlockSpec((tm, d), lambda i: (jnp.maximum(i - ctx_tiles, 0) % pos_tiles, 0)),
            pl.BlockSpec((None, 1, d), lambda i: (layer, 0, 0)),
            mod_spec,
            mod_spec,
        ],
        out_specs=[row_spec, row_spec],
        out_shape=[jax.ShapeDtypeStruct((m, d), F32), jax.ShapeDtypeStruct((m, d), BF16)],
        compiler_params=_params("arbitrary"),
        name="embed_norm_modulate",
    )(x_ctx, x_lat, pos_table, norm_w.reshape(norm_w.shape[0], 1, d), shift, scale)


def _mm_kernel(*refs, part_tiles, n_extra, epilogue, row_pieces):
    n_parts = len(part_tiles)
    x_refs = refs[:n_parts]
    w_ref = refs[n_parts]
    extra = refs[n_parts + 1:n_parts + 1 + n_extra]
    o_ref = refs[n_parts + 1 + n_extra]
    wb_ref = refs[n_parts + 2 + n_extra]
    i = pl.program_id(1)

    @pl.when(i == 0)
    def _():
        wb_ref[...] = w_ref[...].astype(BF16)

    def run(x_ref):
        extra_vals = [r[...] for r in extra]
        if row_pieces == 1:
            epilogue(jnp.dot(x_ref[...], wb_ref[...], preferred_element_type=F32), o_ref, *extra_vals)
            return
        rows = x_ref.shape[0] // row_pieces
        accs = []
        for s in range(row_pieces):
            accs.append(jnp.dot(x_ref[s * rows:(s + 1) * rows, :], wb_ref[...], preferred_element_type=F32))
            if s >= 1:
                epilogue(accs, s - 1, o_ref, *extra_vals)
        epilogue(accs, row_pieces - 1, o_ref, *extra_vals)

    if n_parts == 1:
        run(x_refs[0])
    else:
        first = 0
        for x_ref, tiles in zip(x_refs, part_tiles):
            pl.when(jnp.logical_and(i >= first, i < first + tiles))(functools.partial(run, x_ref))
            first += tiles


def matmul_ws(x, w, layer, n_out, *, tm, tn, epilogue, extras=(), out_dtype=F32, col_block0=0, row_pieces=1,
              name="matmul"):
    parts = list(x) if isinstance(x, (list, tuple)) else [x]
    k = parts[0].shape[1]
    part_tiles = [part.shape[0] // tm for part in parts]
    m = sum(part.shape[0] for part in parts)
    in_specs = []
    first = 0
    for tiles in part_tiles:
        in_specs.append(pl.BlockSpec(
            (tm, k), lambda j, i, first=first, tiles=tiles: (jnp.clip(i - first, 0, tiles - 1), 0)))
        first += tiles
    in_specs.append(pl.BlockSpec((None, k, tn), lambda j, i: (layer, 0, j + col_block0)))
    in_specs += [spec for _, spec in extras]
    return pl.pallas_call(
        functools.partial(_mm_kernel, part_tiles=tuple(part_tiles), n_extra=len(extras), epilogue=epilogue,
                          row_pieces=row_pieces),
        grid=(n_out // tn, m // tm),
        in_specs=in_specs,
        out_specs=pl.BlockSpec((tm, tn), lambda j, i: (i, j)),
        out_shape=jax.ShapeDtypeStruct((m, n_out), out_dtype),
        scratch_shapes=[pltpu.VMEM((k, tn), BF16)],
        compiler_params=_params("arbitrary", "arbitrary"),
        name=name,
    )(*parts, w, *[a for a, _ in extras])


def _ep_plain(acc, o_ref):
    o_ref[...] = acc.astype(o_ref.dtype)


def _ep_bias_gelu(accs, s, o_ref, b):
    rows = accs[s].shape[0]
    o_ref[s * rows:(s + 1) * rows, :] = _gelu_tanh(accs[s] + b).astype(o_ref.dtype)


def _ep_resid_gate(acc, o_ref, resid, gate):
    o_ref[...] = (resid + gate * acc).astype(o_ref.dtype)


def _ep_conv_silu(accs, s, o_ref, w, *, normalize, n_q_tiles):
    p_rows = accs[s].shape[0]
    pieces = o_ref.shape[0] // p_rows
    tn = o_ref.shape[1]
    assert o_ref.shape[0] == DEC_SEQ and SEQ % p_rows == 0
    tile_is_context = pl.program_id(1) * o_ref.shape[0] < N_CTX_TOK
    pad = DN_CONV // 2
    n_ext = p_rows + 2 * CONV_HALO
    zeros = jnp.zeros((CONV_HALO, tn), F32)

    def halo(rows_of_neighbour, boundary_row, at_tile_end):
        if at_tile_end:
            return zeros
        if boundary_row % SEQ == 0:
            return jnp.where(tile_is_context, 0.0, rows_of_neighbour)
        return rows_of_neighbour

    prev = halo(None if s == 0 else accs[s - 1][p_rows - CONV_HALO:], s * p_rows, s == 0)
    nxt = halo(None if s == pieces - 1 else accs[s + 1][:CONV_HALO], (s + 1) * p_rows, s == pieces - 1)
    ext = jnp.concatenate([prev, accs[s], nxt], axis=0)
    y = None
    for t in range(DN_CONV):
        shifted = ext if t == pad else pltpu.roll(ext, (pad - t) % n_ext, 0)
        term = shifted[CONV_HALO:CONV_HALO + p_rows] * w[t:t + 1, :]
        y = term if y is None else y + term
    y = _silu(y)
    rows = slice(s * p_rows, (s + 1) * p_rows)
    if normalize:
        scale = jnp.where(pl.program_id(0) < n_q_tiles, DN_DK ** -0.5, 1.0)
        for hh in range(tn // DN_DK):
            lanes = slice(hh * DN_DK, (hh + 1) * DN_DK)
            sl = y[:, lanes]
            inv = lax.rsqrt(jnp.sum(sl * sl, axis=-1, keepdims=True) + EPS) * scale
            o_ref[rows, lanes] = (sl * inv).astype(o_ref.dtype)
    else:
        o_ref[rows, :] = y.astype(o_ref.dtype)


def matmul_resid_gate(x, w, layer, resid, gate, *, tm, tn, name):
    n_out = resid.shape[1]
    extras = (
        (resid, pl.BlockSpec((tm, tn), lambda j, i: (i, j))),
        (gate, pl.BlockSpec((None, 1, tn), lambda j, i: (_group_of_row(i * tm), 0, j))),
    )
    return matmul_ws(x, w, layer, n_out, tm=tm, tn=tn, epilogue=_ep_resid_gate, extras=extras, name=name)


def _split3(x):
    hi = x.astype(BF16)
    r1 = x - hi.astype(F32)
    mid = r1.astype(BF16)
    lo = (r1 - mid.astype(F32)).astype(BF16)
    return hi, mid, lo


def _gates_kernel(ab_ref, alog_ref, dtb_ref, o_ref):
    c = DN_CHUNK
    ii = lax.broadcasted_iota(jnp.int32, (c, c), 0)
    jj = lax.broadcasted_iota(jnp.int32, (c, c), 1)
    tril = (ii >= jj).astype(BF16)
    triu = (ii <= jj).astype(BF16)
    lane = lax.broadcasted_iota(jnp.int32, (c, ab_ref.shape[1]), 1)
    kind = lane // DN_HEADS_V
    for n in range(ab_ref.shape[0] // c):
        ab = ab_ref[n * c:(n + 1) * c, :]
        log_g = -jnp.exp(alog_ref[...]) * jax.nn.softplus(ab + dtb_ref[...])
        beta = jax.nn.sigmoid(ab)
        parts = _split3(log_g)
        cum_f = sum(jnp.dot(tril, part, preferred_element_type=F32) for part in parts)
        cum_b = sum(jnp.dot(triu, part, preferred_element_type=F32) for part in parts)
        o_ref[n * c:(n + 1) * c, :] = jnp.where(kind == 0, cum_f, jnp.where(kind == 2, cum_b, beta))


def dn_gates(ab, a_log_row, dt_bias_row, chunks_per_step=GATES_CHUNKS_PER_STEP):
    m, n = ab.shape
    rows = chunks_per_step * DN_CHUNK
    return pl.pallas_call(
        _gates_kernel,
        grid=(m // rows,),
        in_specs=[
            pl.BlockSpec((rows, n), lambda i: (i, 0)),
            pl.BlockSpec((1, n), lambda i: (0, 0)),
            pl.BlockSpec((1, n), lambda i: (0, 0)),
        ],
        out_specs=pl.BlockSpec((rows, n), lambda i: (i, 0)),
        out_shape=jax.ShapeDtypeStruct((m, n), F32),
        compiler_params=_params("arbitrary"),
        name="dn_gates",
    )(ab, a_log_row, dt_bias_row)


def _dot(a, b):
    return jnp.dot(a, b, preferred_element_type=F32)


def _inv_unit_triangular_many(a_list):
    n = a_list[0].shape[0]
    ii = lax.broadcasted_iota(jnp.int32, (n, n), 0)
    jj = lax.broadcasted_iota(jnp.int32, (n, n), 1)
    eye = jnp.where(ii == jj, 1.0, 0.0)
    xs = [eye - jnp.where((ii >> 1) == (jj >> 1), a, 0.0) for a in a_list]
    for level in range(1, int(math.log2(DN_CHUNK))):
        joins = jnp.logical_and((ii >> (level + 1)) == (jj >> (level + 1)), (ii >> level) != (jj >> level))
        ns = [jnp.where(joins, a, 0.0).astype(BF16) for a in a_list]
        xbs = [x.astype(BF16) for x in xs]
        ys = [_dot(xb, nn).astype(BF16) for xb, nn in zip(xbs, ns)]
        xs = [x - _dot(y, xb) for x, y, xb in zip(xs, ys, xbs)]
    return xs


def _dn_core_kernel(*refs, n_chunks, group, heads, has_init, has_state_out):
    q_ref, k_ref, v_ref, z_ref, gr_ref, nw_ref = refs[:6]
    pos = 6
    if has_init:
        s0f_ref, s0b_ref = refs[pos:pos + 2]
        pos += 2
    og_ref = refs[pos]
    pos += 1
    if has_state_out:
        sf_ref, sb_ref = refs[pos:pos + 2]
        pos += 2
    wm_ref, cm_ref, egl_ref, p_ref, u_ref, wv_ref, sh_ref, s_ref = refs[pos:pos + 8]

    c = DN_CHUNK
    pc = 2 * c
    dv = DN_DV
    if has_init:
        for hl in range(heads):
            s_ref[2 * hl] = jnp.concatenate([s0f_ref[2 * hl], s0f_ref[2 * hl + 1]], axis=1)
            s_ref[2 * hl + 1] = jnp.concatenate([s0b_ref[2 * hl], s0b_ref[2 * hl + 1]], axis=1)
    else:
        s_ref[...] = jnp.zeros_like(s_ref)

    ii = lax.broadcasted_iota(jnp.int32, (pc, pc), 0)
    jj = lax.broadcasted_iota(jnp.int32, (pc, pc), 1)
    chunk_shift = int(math.log2(c))
    same_head = (ii >> chunk_shift) == (jj >> chunk_shift)
    top = ii < c
    top_col = lax.broadcasted_iota(jnp.int32, (pc, 1), 0) < c
    nt_dims = (((1,), (1,)), ((), ()))

    def rows_of(chunk):
        start = chunk * c
        return pl.ds(start if isinstance(start, int) else pl.multiple_of(start, c), c)

    def gate_columns(units):
        return [gr_ref[hl, ch].T for hl, ch in units]

    def entry(hl, chunk, d):
        return (hl * n_chunks + chunk) * 2 + d

    def qk_lanes(hl):
        return slice(hl * DN_DK, (hl + 1) * DN_DK)

    def v_lanes(hl, hh):
        return slice((2 * hl + hh) * dv, (2 * hl + hh + 1) * dv)

    def phase_a(chunks):
        units = [(hl, ch) for hl in range(heads) for ch in chunks]
        chains = [(n, d) for n in range(len(units)) for d in (0, 1)]
        k2s, grams = [], []
        for hl, ch in units:
            kc = k_ref[rows_of(ch), qk_lanes(hl)]
            qc = q_ref[rows_of(ch), qk_lanes(hl)]
            k2 = jnp.concatenate([kc, kc], axis=0)
            k2s.append(k2)
            lhs = jnp.concatenate([kc, kc, qc, qc], axis=0)
            grams.append(lax.dot_general(lhs, k2, nt_dims, preferred_element_type=F32))
        colvs = gate_columns(units)
        a_mats, p_mats, gcols, bcols = [], [], [], []
        for n, d in chains:
            hl, ch = units[n]
            gcol, bcol = colvs[n][:, 2 * d:2 * d + 1], colvs[n][:, 2 * d + 1:2 * d + 2]
            grow = gr_ref[hl, ch][2 * d:2 * d + 1, :]
            incl = jnp.logical_and(same_head, (ii >= jj) if d == 0 else (ii <= jj))
            strict = jnp.logical_and(same_head, (ii > jj) if d == 0 else (ii < jj))
            decay = jnp.exp(jnp.where(incl, gcol - grow, MASKED_LOG_DECAY))
            a_mats.append(jnp.where(strict, bcol * grams[n][:pc] * decay, 0.0))
            p_mats.append(grams[n][pc:] * decay)
            gcols.append(gcol)
            bcols.append(bcol)
        t_invs = _inv_unit_triangular_many(a_mats)
        rhss, kfs = [], []
        for (n, d), gcol, bcol in zip(chains, gcols, bcols):
            hl, ch = units[n]
            kf = k2s[n].astype(F32)
            rows = rows_of(ch)
            vp = jnp.concatenate([v_ref[rows, v_lanes(hl, 0)], v_ref[rows, v_lanes(hl, 1)]], axis=0)
            rhss.append(jnp.concatenate([(bcol * jnp.exp(gcol)) * kf, bcol * vp], axis=1).astype(BF16))
            kfs.append(kf)
        uws = [_dot(t.astype(BF16), r) for t, r in zip(t_invs, rhss)]
        kdts, x2s, egls = [], [], []
        for (n, d), gcol, kf, uw in zip(chains, gcols, kfs, uws):
            r0 = c - 1 if d == 0 else 0
            gl0 = gcol[r0:r0 + 1]
            gl1 = gcol[c + r0:c + r0 + 1]
            gl = jnp.where(top_col, gl0, gl1)
            kdts.append((kf * jnp.exp(gl - gcol)).T.astype(BF16))
            u, wv = uw[:, :dv], uw[:, dv:]
            x2s.append(jnp.concatenate([jnp.where(top, u, 0.0), jnp.where(top, 0.0, u),
                                        jnp.where(top, wv, 0.0), jnp.where(top, 0.0, wv)], axis=1).astype(BF16))
            egls.append(jnp.concatenate([jnp.broadcast_to(jnp.exp(gl0), (1, dv)),
                                         jnp.broadcast_to(jnp.exp(gl1), (1, dv))], axis=1))
        wcs = [_dot(kdt, x2) for kdt, x2 in zip(kdts, x2s)]
        for (n, d), p_mat, uw, wc, egl in zip(chains, p_mats, uws, wcs, egls):
            e = entry(*units[n], d)
            wm_ref[e] = wc[:, :2 * dv].astype(BF16)
            cm_ref[e] = wc[:, 2 * dv:]
            egl_ref[e] = egl
            p_ref[e] = p_mat.astype(BF16)
            u_ref[e] = uw[:, :dv].astype(BF16)
            wv_ref[e] = uw[:, dv:]

    def phase_b(n, carry):
        lines = [(hl, d) for hl in range(heads) for d in (0, 1)]
        es = [entry(hl, n if d == 0 else n_chunks - 1 - n, d) for hl, d in lines]
        ss = [s_ref[2 * hl + d] for hl, d in lines]
        sbs = [s.astype(BF16) for s in ss]
        for e, sb in zip(es, sbs):
            sh_ref[e] = sb
        wms = [wm_ref[e] for e in es]
        wss = [jnp.concatenate([_dot(wm[:, :dv], sb[:, :dv]), _dot(wm[:, dv:], sb[:, dv:])], axis=1)
               for wm, sb in zip(wms, sbs)]
        for (hl, d), e, s, ws in zip(lines, es, ss, wss):
            s_ref[2 * hl + d] = egl_ref[e] * s - ws + cm_ref[e]
        return carry

    def phase_c(chunks):
        units = [(hl, ch) for hl in range(heads) for ch in chunks]
        chains = [(n, d) for n in range(len(units)) for d in (0, 1)]
        es = [entry(*units[n], d) for n, d in chains]
        rs = []
        for (n, d), e in zip(chains, es):
            hl, ch = units[n]
            lhs = jnp.concatenate([q_ref[rows_of(ch), qk_lanes(hl)], u_ref[e]], axis=0)
            rs.append(_dot(lhs, sh_ref[e]))
        v_news, q_ss = [], []
        for e, r in zip(es, rs):
            q_ss.append(jnp.concatenate([r[:c, :dv], r[:c, dv:]], axis=0))
            u_s = jnp.concatenate([r[c:2 * c, :dv], r[2 * c:, dv:]], axis=0)
            v_news.append((wv_ref[e] - u_s).astype(BF16))
        pvs = [_dot(p_ref[e], v_new) for e, v_new in zip(es, v_news)]
        colvs = gate_columns(units)
        outs = [jnp.exp(colvs[n][:, 2 * d:2 * d + 1]) * q_s + pv for (n, d), q_s, pv in zip(chains, q_ss, pvs)]
        for n, (hl, ch) in enumerate(units):
            o = outs[2 * n] + outs[2 * n + 1]
            rows = rows_of(ch)
            for hh in range(2):
                y = _rms(o[hh * c:(hh + 1) * c], nw_ref[...])
                og_ref[rows, v_lanes(hl, hh)] = (y * _silu(z_ref[rows, v_lanes(hl, hh)])).astype(og_ref.dtype)

    def over_groups(phase):
        def body(g, carry):
            phase([g * group + i for i in range(group)])
            return carry
        lax.fori_loop(0, n_chunks // group, body, 0)

    if n_chunks <= DN_UNROLLED_SCAN_CHUNKS:
        ready = set()
        for n in range(n_chunks):
            needed = sorted({n, n_chunks - 1 - n} - ready)
            if needed:
                phase_a(needed)
                ready.update(needed)
            phase_b(n, 0)
        phase_c(list(range(n_chunks)))
    else:
        over_groups(phase_a)
        for n in range(n_chunks):
            phase_b(n, 0)
        phase_c(list(range(n_chunks)))

    if has_state_out:
        for hl in range(heads):
            for hh in range(2):
                sf_ref[2 * hl + hh] = s_ref[2 * hl][:, hh * dv:(hh + 1) * dv]
                sb_ref[2 * hl + hh] = s_ref[2 * hl + 1][:, hh * dv:(hh + 1) * dv]


def dn_core(qk, v, z, gates_row, norm_w, layer, *, seq_len, n_seq, row0, heads, group=None, init_states=None):
    n_chunks = seq_len // DN_CHUNK
    rb0 = row0 // seq_len
    qkw = heads * DN_DK
    pair = 2 * DN_DV
    vw = heads * pair
    pc = 2 * DN_CHUNK
    n_e = 2 * n_chunks * heads
    assert DN_HEADS_K % heads == 0
    if n_chunks > DN_UNROLLED_SCAN_CHUNKS:
        assert group is not None and n_chunks % group == 0
    else:
        assert group is None
    has_init = init_states is not None
    in_specs = [
        pl.BlockSpec((seq_len, qkw), lambda s, h: (rb0 + s, h)),
        pl.BlockSpec((seq_len, qkw), lambda s, h: (rb0 + s, DN_HEADS_K // heads + h)),
        pl.BlockSpec((seq_len, vw), lambda s, h: (rb0 + s, h)),
        pl.BlockSpec((seq_len, vw), lambda s, h: (rb0 + s, h)),
        pl.BlockSpec((heads, n_chunks, SUBLANES, 2 * DN_CHUNK), lambda s, h: (h, rb0 + s, 0, 0)),
        pl.BlockSpec((None, 1, DN_DV), lambda s, h: (layer, 0, 0)),
    ]
    args = [qk, qk, v, z, gates_row, norm_w]
    og_shape = jax.ShapeDtypeStruct((n_seq * seq_len, DN_V_DIM), BF16)
    og_spec = pl.BlockSpec((seq_len, vw), lambda s, h: (s, h))
    if has_init:
        st_spec = pl.BlockSpec((None, None, 2 * heads, DN_DK, DN_DV), lambda s, h: (s, layer, h, 0, 0))
        in_specs += [st_spec, st_spec]
        args += [init_states[0], init_states[1]]
        out_specs = og_spec
        out_shape = og_shape
    else:
        so_spec = pl.BlockSpec((None, None, 2 * heads, DN_DK, DN_DV), lambda s, h: (s, 0, h, 0, 0))
        so_shape = jax.ShapeDtypeStruct((n_seq, 1, DN_HEADS_V, DN_DK, DN_DV), F32)
        out_specs = [og_spec, so_spec, so_spec]
        out_shape = [og_shape, so_shape, so_shape]
    return pl.pallas_call(
        functools.partial(_dn_core_kernel, n_chunks=n_chunks, group=group, heads=heads, has_init=has_init,
                          has_state_out=not has_init),
        grid=(n_seq, DN_HEADS_K // heads),
        in_specs=in_specs,
        out_specs=out_specs,
        out_shape=out_shape,
        scratch_shapes=[
            pltpu.VMEM((n_e, DN_DK, pair), BF16),
            pltpu.VMEM((n_e, DN_DK, pair), F32),
            pltpu.VMEM((n_e, 1, pair), F32),
            pltpu.VMEM((n_e, pc, pc), BF16),
            pltpu.VMEM((n_e, pc, DN_DK), BF16),
            pltpu.VMEM((n_e, pc, DN_DV), F32),
            pltpu.VMEM((n_e, DN_DK, pair), BF16),
            pltpu.VMEM((2 * heads, DN_DK, pair), F32),
        ],
        compiler_params=_params("arbitrary", "arbitrary"),
        name="dn_core_lat" if has_init else "dn_core_ctx",
    )(*args)


def _cm_gate_kernel(u_ref, v_ref, lw_ref, lb_ref, ws_ref, bs_ref, o_ref):
    for n in range(u_ref.shape[0] // CM_CHUNK):
        rows = slice(n * CM_CHUNK, (n + 1) * CM_CHUNK)
        v = v_ref[rows, :].astype(F32)
        mu = jnp.mean(v, axis=-1, keepdims=True)
        vc = v - mu
        vn = vc * lax.rsqrt(jnp.mean(vc * vc, axis=-1, keepdims=True) + EPS) * lw_ref[...] + lb_ref[...]
        vn = vn.astype(BF16)
        for g in range(CM_GROUPS):
            lanes = slice(g * CM_GDIM, (g + 1) * CM_GDIM)
            sp = jnp.dot(ws_ref[g].astype(BF16), vn[:, lanes], preferred_element_type=F32) + bs_ref[g]
            o_ref[rows, lanes] = (u_ref[rows, lanes].astype(F32) * sp).astype(o_ref.dtype)


def cm_gate(zz, ln_w, ln_b, w_s, b_s_col, layer, chunks_per_step=CM_GATE_CHUNKS_PER_STEP):
    m = zz.shape[0]
    rows = chunks_per_step * CM_CHUNK
    row = pl.BlockSpec((None, 1, CM_DIM), lambda i: (layer, 0, 0))
    return pl.pallas_call(
        _cm_gate_kernel,
        grid=(m // rows,),
        in_specs=[
            pl.BlockSpec((rows, CM_DIM), lambda i: (i, 0)),
            pl.BlockSpec((rows, CM_DIM), lambda i: (i, 1)),
            row,
            row,
            pl.BlockSpec((None, CM_GROUPS, CM_CHUNK, CM_CHUNK), lambda i: (layer, 0, 0, 0)),
            pl.BlockSpec((None, CM_GROUPS, CM_CHUNK, 1), lambda i: (layer, 0, 0, 0)),
        ],
        out_specs=pl.BlockSpec((rows, CM_DIM), lambda i: (i, 0)),
        out_shape=jax.ShapeDtypeStruct((m, CM_DIM), BF16),
        compiler_params=_params("arbitrary"),
        name="cm_gate",
    )(zz, zz, ln_w, ln_b, w_s, b_s_col)


def _ffn_kernel(*refs, tail):
    x_ref, nw_ref, sh_ref, sc_ref, g_ref, w1_ref, w2_ref, tw_ref = refs[:8]
    if tail == "next_h":
        tsh_ref, tsc_ref, o_ref, hn_ref, h_ref = refs[8:]
    else:
        o_ref, h_ref = refs[8:]
    kk = pl.program_id(1)

    @pl.when(kk == 0)
    def _():
        y = _rms(x_ref[...], nw_ref[...])
        h_ref[...] = (y * (1.0 + sc_ref[...]) + sh_ref[...]).astype(BF16)
        o_ref[...] = jnp.zeros_like(o_ref)

    a = jnp.dot(h_ref[...], w1_ref[...].astype(BF16), preferred_element_type=F32)
    a = jnp.square(jnp.maximum(a, 0.0)).astype(BF16)
    o_ref[...] += jnp.dot(a, w2_ref[...].astype(BF16), preferred_element_type=F32)

    @pl.when(kk == pl.num_programs(1) - 1)
    def _():
        r = x_ref[...] + g_ref[...] * o_ref[...]
        if tail == "final_norm":
            r = _rms(r, tw_ref[...])
        else:
            hn_ref[...] = (_rms(r, tw_ref[...]) * (1.0 + tsc_ref[...]) + tsh_ref[...]).astype(BF16)
        o_ref[...] = r


def ffn(x, norm_w, layer, shift, scale, gate, w1, w2, *, tail, tail_w, tail_mod=None, row0=0, n_rows=None,
        tm=ROW_TILE, tk=FFN_K_TILE):
    d = x.shape[1]
    n_rows = x.shape[0] if n_rows is None else n_rows
    ff = w1.shape[2]
    t0 = row0 // tm
    mod_spec = pl.BlockSpec((None, 1, d), lambda i, k: (_group_of_row((t0 + i) * tm), 0, 0))
    row_spec = pl.BlockSpec((tm, d), lambda i, k: (i, 0), pipeline_mode=pl.Buffered(1))
    in_specs = [
        pl.BlockSpec((tm, d), lambda i, k: (t0 + i, 0), pipeline_mode=pl.Buffered(1)),
        pl.BlockSpec((None, 1, d), lambda i, k: (layer, 0, 0)),
        mod_spec,
        mod_spec,
        mod_spec,
        pl.BlockSpec((None, d, tk), lambda i, k: (layer, 0, k)),
        pl.BlockSpec((None, tk, d), lambda i, k: (layer, k, 0)),
        pl.BlockSpec((1, d), lambda i, k: (0, 0)),
    ]
    args = [x, norm_w.reshape(norm_w.shape[0], 1, d), shift, scale, gate, w1, w2, tail_w.reshape(1, d)]
    out_specs, out_shape = row_spec, jax.ShapeDtypeStruct((n_rows, d), F32)
    if tail == "next_h":
        in_specs += [mod_spec, mod_spec]
        args += list(tail_mod)
        out_specs = [row_spec, row_spec]
        out_shape = [out_shape, jax.ShapeDtypeStruct((n_rows, d), BF16)]
    return pl.pallas_call(
        functools.partial(_ffn_kernel, tail=tail),
        grid=(n_rows // tm, ff // tk),
        in_specs=in_specs,
        out_specs=out_specs,
        out_shape=out_shape,
        scratch_shapes=[pltpu.VMEM((tm, d), BF16)],
        compiler_params=_params("arbitrary", "arbitrary"),
        name="ffn_" + tail,
    )(*args)


def _grid_pos_embed(n_tokens):
    rows = n_tokens // GRID_W
    r = np.repeat(np.arange(rows), GRID_W).astype(np.float64)
    col = np.tile(np.arange(GRID_W), rows).astype(np.float64)
    quarter = D_MODEL // 4
    freq = 1.0 / (10000.0 ** (np.arange(quarter, dtype=np.float64) / quarter))
    ar = r[:, None] * freq[None, :]
    ac = col[:, None] * freq[None, :]
    return np.concatenate([np.sin(ar), np.cos(ar), np.sin(ac), np.cos(ac)], axis=-1)


def _deltanet_layer(x, h, j, gate, state_f, state_b, dn_w_in, dn_conv_w, dn_a_log, dn_dt_bias, dn_norm_w, dn_w_out):
    tm, tn = ROW_TILE, PROJ_COL_TILE

    def project(col0, n_cols, name, epilogue=_ep_plain, extras=(), out_dtype=F32, tn=tn, row_pieces=1):
        return matmul_ws(h, dn_w_in, j, n_cols, tm=tm, tn=tn, epilogue=epilogue, extras=extras,
                         out_dtype=out_dtype, col_block0=col0 // tn, row_pieces=row_pieces, name=name)

    def conv_w_for(col0):
        return (dn_conv_w, pl.BlockSpec((None, DN_CONV, tn), lambda jj, i: (j, 0, col0 // tn + jj)))

    qk = project(0, 2 * DN_K_DIM, "dn_in_qk", extras=(conv_w_for(0),), out_dtype=BF16, row_pieces=EPILOGUE_ROW_PIECES,
                 epilogue=functools.partial(_ep_conv_silu, normalize=True, n_q_tiles=DN_K_DIM // tn))
    v = project(2 * DN_K_DIM, DN_V_DIM, "dn_in_v", extras=(conv_w_for(2 * DN_K_DIM),), row_pieces=EPILOGUE_ROW_PIECES,
                epilogue=functools.partial(_ep_conv_silu, normalize=False, n_q_tiles=0))
    z = project(DN_QKV_DIM, DN_V_DIM, "dn_in_z")
    ab = project(DN_QKV_DIM + DN_V_DIM, DN_GATE_COLS, "dn_in_gates", tn=DN_GATE_COLS)

    zeros = jnp.zeros((DN_HEADS_V,), F32)
    a_log_row = jnp.concatenate([dn_a_log[j, 0], zeros, dn_a_log[j, 1], zeros])[None, :]
    dt_row = jnp.concatenate([dn_dt_bias[j, 0], zeros, dn_dt_bias[j, 1], zeros])[None, :]
    gates = dn_gates(ab, a_log_row, dt_row)
    m = gates.shape[0]
    g5 = gates.reshape(m // DN_CHUNK, DN_CHUNK, 4, DN_HEADS_K, 2)
    gates_row = g5.transpose(3, 0, 2, 4, 1).reshape(DN_HEADS_K, m // DN_CHUNK, 4, 2 * DN_CHUNK)
    gates_row = jnp.pad(gates_row, ((0, 0), (0, 0), (0, SUBLANES - 4), (0, 0)))
    norm_w = dn_norm_w.reshape(dn_norm_w.shape[0], 1, DN_DV)

    og_ctx, s_f, s_b = dn_core(qk, v, z, gates_row, norm_w, j, seq_len=SEQ, n_seq=BATCH, row0=0,
                               heads=4)
    og_lat = dn_core(qk, v, z, gates_row, norm_w, j, seq_len=DEC_SEQ, n_seq=DEC_BATCH, row0=N_CTX_TOK,
                     heads=2, group=4, init_states=(state_f, state_b))
    x = matmul_resid_gate([og_ctx, og_lat], dn_w_out, j, x, gate, tm=DN_OUT_ROW_TILE, tn=OUT_COL_TILE,
                          name="dn_out")
    return x, s_f, s_b


def _chunk_mlp_layer(x, h, j, gate, cm_w_in, cm_b_in, cm_ln_w, cm_ln_b, cm_w_s, cm_b_s, cm_w_out):
    n_in = 2 * CM_DIM
    tn = PROJ_COL_TILE
    bias = (cm_b_in.reshape(cm_b_in.shape[0], 1, n_in), pl.BlockSpec((None, 1, tn), lambda jj, i: (j, 0, jj)))
    zz = matmul_ws(h, cm_w_in, j, n_in, tm=ROW_TILE, tn=tn, epilogue=_ep_bias_gelu, extras=(bias,),
                   out_dtype=BF16, row_pieces=EPILOGUE_ROW_PIECES, name="cm_in")
    n_b = cm_ln_w.shape[0]
    uv = cm_gate(zz, cm_ln_w.reshape(n_b, 1, CM_DIM), cm_ln_b.reshape(n_b, 1, CM_DIM), cm_w_s,
                 cm_b_s[..., None], j)
    return matmul_resid_gate(uv, cm_w_out, j, x, gate, tm=ROW_TILE, tn=OUT_COL_TILE, name="cm_out")


def kernel(x_prompt, x_sample, state_dn_fwd, state_dn_bwd, c, c_ctx, norm_mix_w, norm_mlp_w, w_ada, b_ada, dn_w_in, dn_conv_w, dn_A_log, dn_dt_bias, dn_norm_w, dn_w_out, cm_w_in, cm_b_in, cm_ln_w, cm_ln_b, cm_w_s, cm_b_s, cm_w_out, w_ff1, w_ff2, final_norm_w):
    cond = jnp.concatenate([c_ctx[None, :], c, jnp.zeros((N_COND - 1 - DEC_BATCH, D_MODEL), F32)], axis=0)
    mod = adaln_all(cond, w_ada, b_ada)
    mod = mod.reshape(DEPTH, N_COND, N_MOD, D_MODEL).transpose(0, 2, 1, 3)[:, :, :, None, :]
    mods = [[mod[i, t] for t in range(N_MOD)] for i in range(DEPTH)]

    x, h = embed_norm_modulate(x_prompt.reshape(N_CTX_TOK, D_MODEL), x_sample.reshape(N_LAT_TOK, D_MODEL),
                               jnp.asarray(_grid_pos_embed(DEC_SEQ), x_sample.dtype), norm_mix_w, 0,
                               mods[0][0], mods[0][1])
    new_fwd, new_bwd = [], []
    for i in range(DEPTH):
        j = i // N_MIXERS
        _, _, gate, shift2, scale2, gate2 = mods[i]
        if i % N_MIXERS == 0:
            x, s_f, s_b = _deltanet_layer(x, h, j, gate, state_dn_fwd, state_dn_bwd, dn_w_in, dn_conv_w,
                                          dn_A_log, dn_dt_bias, dn_norm_w, dn_w_out)
            new_fwd.append(s_f)
            new_bwd.append(s_b)
        else:
            x = _chunk_mlp_layer(x, h, j, gate, cm_w_in, cm_b_in, cm_ln_w, cm_ln_b, cm_w_s, cm_b_s, cm_w_out)
        mlp = functools.partial(ffn, x, norm_mlp_w, i, shift2, scale2, gate2, w_ff1, w_ff2)
        if i < DEPTH - 1:
            x, h = mlp(tail="next_h", tail_w=norm_mix_w[i + 1], tail_mod=mods[i + 1][:2])
        else:
            y_ctx = mlp(tail="final_norm", tail_w=final_norm_w, row0=0, n_rows=N_CTX_TOK)
            y_lat = mlp(tail="final_norm", tail_w=final_norm_w, row0=N_CTX_TOK, n_rows=N_LAT_TOK)

    y_prompt = y_ctx.reshape(BATCH, SEQ, D_MODEL)
    y_sample = y_lat.reshape(DEC_BATCH, DEC_SEQ, D_MODEL)
    return (y_prompt, y_sample, jnp.concatenate(new_fwd, axis=1), jnp.concatenate(new_bwd, axis=1))
```

```python
import functools
import math

import jax
import jax.numpy as jnp
import numpy as np
from jax import lax
from jax.experimental import pallas as pl
from jax.experimental.pallas import tpu as pltpu

F32 = jnp.float32
BF16 = jnp.bfloat16

D_MODEL = 2048
BATCH = 16
SEQ = 256
DEPTH = 2
DEC_BATCH = 2
DEC_SEQ = 1024
GRID_W = 64
N_MIXERS = 2
DN_DK = 128
DN_DV = 128
DN_HEADS_K = D_MODEL // DN_DK
DN_HEADS_V = 2 * DN_HEADS_K
DN_K_DIM = DN_HEADS_K * DN_DK
DN_V_DIM = DN_HEADS_V * DN_DV
DN_QKV_DIM = 2 * DN_K_DIM + DN_V_DIM
DN_GATE_COLS = 4 * DN_HEADS_V
DN_CONV = 5
DN_CHUNK = 64
DN_UNROLLED_SCAN_CHUNKS = 4
CM_DIM = 2 * D_MODEL
CM_CHUNK = 128
CM_GROUPS = 16
CM_GDIM = CM_DIM // CM_GROUPS
FF_DIM = 4 * D_MODEL
N_MOD = 6
EPS = 1e-6

N_CTX_TOK = BATCH * SEQ
N_LAT_TOK = DEC_BATCH * DEC_SEQ
N_TOK = N_CTX_TOK + N_LAT_TOK
N_COND = 8
SUBLANES = 8
CONV_HALO = SUBLANES
MASKED_LOG_DECAY = -1e30
V7X_VMEM_BYTES = 64 * 1024 * 1024
VMEM_LIMIT_BYTES = V7X_VMEM_BYTES * 7 // 8

ROW_TILE = 1024
PROJ_COL_TILE = 1024
OUT_COL_TILE = 512
DN_OUT_ROW_TILE = 512
FFN_K_TILE = 512
EPILOGUE_ROW_PIECES = 4
ADALN_COL_TILE = 1024
EMBED_ROW_TILE = 512
GATES_CHUNKS_PER_STEP = 8
CM_GATE_CHUNKS_PER_STEP = 2


def _group_of_row(row0):
    return jnp.where(row0 < N_CTX_TOK, 0, 1 + (row0 - N_CTX_TOK) // DEC_SEQ)


def _params(*sem):
    return pltpu.CompilerParams(dimension_semantics=sem, vmem_limit_bytes=VMEM_LIMIT_BYTES)


def _rms(x, w):
    return x * lax.rsqrt(jnp.mean(x * x, axis=-1, keepdims=True) + EPS) * w


def _silu(x):
    half = 0.5 * x
    return half + half * jnp.tanh(half)


def _gelu_tanh(x):
    return 0.5 * x * (1.0 + jnp.tanh(math.sqrt(2.0 / math.pi) * (x + 0.044715 * (x * x * x))))


def _adaln_kernel(c_ref, w_ref, b_ref, o_ref):
    x = _silu(c_ref[...]).astype(BF16)
    acc = jnp.dot(x, w_ref[...].astype(BF16), preferred_element_type=F32)
    o_ref[...] = acc + b_ref[...]


def adaln_all(cond, w_ada, b_ada, tn=ADALN_COL_TILE):
    depth, d, n = w_ada.shape
    return pl.pallas_call(
        _adaln_kernel,
        grid=(depth, n // tn),
        in_specs=[
            pl.BlockSpec((N_COND, d), lambda l, j: (0, 0)),
            pl.BlockSpec((None, d, tn), lambda l, j: (l, 0, j)),
            pl.BlockSpec((None, 1, tn), lambda l, j: (l, 0, j)),
        ],
        out_specs=pl.BlockSpec((None, N_COND, tn), lambda l, j: (l, 0, j)),
        out_shape=jax.ShapeDtypeStruct((depth, N_COND, n), F32),
        compiler_params=_params("arbitrary", "arbitrary"),
        name="adaln",
    )(cond, w_ada, b_ada.reshape(depth, 1, n))


def _embed_kernel(xc_ref, xl_ref, pe_ref, w_ref, sh_ref, sc_ref, x_ref, h_ref, *, ctx_tiles):
    def emit(x):
        x_ref[...] = x
        h_ref[...] = (_rms(x, w_ref[...]) * (1.0 + sc_ref[...]) + sh_ref[...]).astype(BF16)

    i = pl.program_id(0)
    pl.when(i < ctx_tiles)(lambda: emit(xc_ref[...]))
    pl.when(i >= ctx_tiles)(lambda: emit(xl_ref[...] + pe_ref[...]))


def embed_norm_modulate(x_ctx, x_lat, pos_table, norm_w, layer, shift, scale, tm=EMBED_ROW_TILE):
    d = x_ctx.shape[1]
    ctx_tiles = x_ctx.shape[0] // tm
    lat_tiles = x_lat.shape[0] // tm
    pos_tiles = pos_table.shape[0] // tm
    m = x_ctx.shape[0] + x_lat.shape[0]
    mod_spec = pl.BlockSpec((None, 1, d), lambda i: (_group_of_row(i * tm), 0, 0))
    row_spec = pl.BlockSpec((tm, d), lambda i: (i, 0))
    return pl.pallas_call(
        functools.partial(_embed_kernel, ctx_tiles=ctx_tiles),
        grid=(ctx_tiles + lat_tiles,),
        in_specs=[
            pl.BlockSpec((tm, d), lambda i: (jnp.minimum(i, ctx_tiles - 1), 0)),
            pl.BlockSpec((tm, d), lambda i: (jnp.clip(i - ctx_tiles, 0, lat_tiles - 1), 0)),
            pl.BlockSpec((tm, d), lambda i: (jnp.maximum(i - ctx_tiles, 0) % pos_tiles, 0)),
            pl.BlockSpec((None, 1, d), lambda i: (layer, 0, 0)),
            mod_spec,
            mod_spec,
        ],
        out_specs=[row_spec, row_spec],
        out_shape=[jax.ShapeDtypeStruct((m, d), F32), jax.ShapeDtypeStruct((m, d), BF16)],
        compiler_params=_params("arbitrary"),
        name="embed_norm_modulate",
    )(x_ctx, x_lat, pos_table, norm_w.reshape(norm_w.shape[0], 1, d), shift, scale)


def _mm_kernel(*refs, part_tiles, n_extra, epilogue, row_pieces):
    n_parts = len(part_tiles)
    x_refs = refs[:n_parts]
    w_ref = refs[n_parts]
    extra = refs[n_parts + 1:n_parts + 1 + n_extra]
    o_ref = refs[n_parts + 1 + n_extra]
    wb_ref = refs[n_parts + 2 + n_extra]
    i = pl.program_id(1)

    @pl.when(i == 0)
    def _():
        wb_ref[...] = w_ref[...].astype(BF16)

    def run(x_ref):
        extra_vals = [r[...] for r in extra]
        if row_pieces == 1:
            epilogue(jnp.dot(x_ref[...], wb_ref[...], preferred_element_type=F32), o_ref, *extra_vals)
            return
        rows = x_ref.shape[0] // row_pieces
        accs = []
        for s in range(row_pieces):
            accs.append(jnp.dot(x_ref[s * rows:(s + 1) * rows, :], wb_ref[...], preferred_element_type=F32))
            if s >= 1:
                epilogue(accs, s - 1, o_ref, *extra_vals)
        epilogue(accs, row_pieces - 1, o_ref, *extra_vals)

    if n_parts == 1:
        run(x_refs[0])
    else:
        first = 0
        for x_ref, tiles in zip(x_refs, part_tiles):
            pl.when(jnp.logical_and(i >= first, i < first + tiles))(functools.partial(run, x_ref))
            first += tiles


def matmul_ws(x, w, layer, n_out, *, tm, tn, epilogue, extras=(), out_dtype=F32, col_block0=0, row_pieces=1,
              name="matmul"):
    parts = list(x) if isinstance(x, (list, tuple)) else [x]
    k = parts[0].shape[1]
    part_tiles = [part.shape[0] // tm for part in parts]
    m = sum(part.shape[0] for part in parts)
    in_specs = []
    first = 0
    for tiles in part_tiles:
        in_specs.append(pl.BlockSpec(
            (tm, k), lambda j, i, first=first, tiles=tiles: (jnp.clip(i - first, 0, tiles - 1), 0)))
        first += tiles
    in_specs.append(pl.BlockSpec((None, k, tn), lambda j, i: (layer, 0, j + col_block0)))
    in_specs += [spec for _, spec in extras]
    return pl.pallas_call(
        functools.partial(_mm_kernel, part_tiles=tuple(part_tiles), n_extra=len(extras), epilogue=epilogue,
                          row_pieces=row_pieces),
        grid=(n_out // tn, m // tm),
        in_specs=in_specs,
        out_specs=pl.BlockSpec((tm, tn), lambda j, i: (i, j)),
        out_shape=jax.ShapeDtypeStruct((m, n_out), out_dtype),
        scratch_shapes=[pltpu.VMEM((k, tn), BF16)],
        compiler_params=_params("arbitrary", "arbitrary"),
        name=name,
    )(*parts, w, *[a for a, _ in extras])


def _ep_plain(acc, o_ref):
    o_ref[...] = acc.astype(o_ref.dtype)


def _ep_bias_gelu(accs, s, o_ref, b):
    rows = accs[s].shape[0]
    o_ref[s * rows:(s + 1) * rows, :] = _gelu_tanh(accs[s] + b).astype(o_ref.dtype)


def _ep_resid_gate(acc, o_ref, resid, gate):
    o_ref[...] = (resid + gate * acc).astype(o_ref.dtype)


def _ep_conv_silu(accs, s, o_ref, w, *, normalize, n_q_tiles):
    p_rows = accs[s].shape[0]
    pieces = o_ref.shape[0] // p_rows
    tn = o_ref.shape[1]
    assert o_ref.shape[0] == DEC_SEQ and SEQ % p_rows == 0
    tile_is_context = pl.program_id(1) * o_ref.shape[0] < N_CTX_TOK
    pad = DN_CONV // 2
    n_ext = p_rows + 2 * CONV_HALO
    zeros = jnp.zeros((CONV_HALO, tn), F32)

    def halo(rows_of_neighbour, boundary_row, at_tile_end):
        if at_tile_end:
            return zeros
        if boundary_row % SEQ == 0:
            return jnp.where(tile_is_context, 0.0, rows_of_neighbour)
        return rows_of_neighbour

    prev = halo(None if s == 0 else accs[s - 1][p_rows - CONV_HALO:], s * p_rows, s == 0)
    nxt = halo(None if s == pieces - 1 else accs[s + 1][:CONV_HALO], (s + 1) * p_rows, s == pieces - 1)
    ext = jnp.concatenate([prev, accs[s], nxt], axis=0)
    y = None
    for t in range(DN_CONV):
        shifted = ext if t == pad else pltpu.roll(ext, (pad - t) % n_ext, 0)
        term = shifted[CONV_HALO:CONV_HALO + p_rows] * w[t:t + 1, :]
        y = term if y is None else y + term
    y = _silu(y)
    rows = slice(s * p_rows, (s + 1) * p_rows)
    if normalize:
        scale = jnp.where(pl.program_id(0) < n_q_tiles, DN_DK ** -0.5, 1.0)
        for hh in range(tn // DN_DK):
            lanes = slice(hh * DN_DK, (hh + 1) * DN_DK)
            sl = y[:, lanes]
            inv = lax.rsqrt(jnp.sum(sl * sl, axis=-1, keepdims=True) + EPS) * scale
            o_ref[rows, lanes] = (sl * inv).astype(o_ref.dtype)
    else:
        o_ref[rows, :] = y.astype(o_ref.dtype)


def matmul_resid_gate(x, w, layer, resid, gate, *, tm, tn, name):
    n_out = resid.shape[1]
    extras = (
        (resid, pl.BlockSpec((tm, tn), lambda j, i: (i, j))),
        (gate, pl.BlockSpec((None, 1, tn), lambda j, i: (_group_of_row(i * tm), 0, j))),
    )
    return matmul_ws(x, w, layer, n_out, tm=tm, tn=tn, epilogue=_ep_resid_gate, extras=extras, name=name)


def _split3(x):
    hi = x.astype(BF16)
    r1 = x - hi.astype(F32)
    mid = r1.astype(BF16)
    lo = (r1 - mid.astype(F32)).astype(BF16)
    return hi, mid, lo


def _gates_kernel(ab_ref, alog_ref, dtb_ref, o_ref):
    c = DN_CHUNK
    ii = lax.broadcasted_iota(jnp.int32, (c, c), 0)
    jj = lax.broadcasted_iota(jnp.int32, (c, c), 1)
    tril = (ii >= jj).astype(BF16)
    triu = (ii <= jj).astype(BF16)
    lane = lax.broadcasted_iota(jnp.int32, (c, ab_ref.shape[1]), 1)
    kind = lane // DN_HEADS_V
    for n in range(ab_ref.shape[0] // c):
        ab = ab_ref[n * c:(n + 1) * c, :]
        log_g = -jnp.exp(alog_ref[...]) * jax.nn.softplus(ab + dtb_ref[...])
        beta = jax.nn.sigmoid(ab)
        parts = _split3(log_g)
        cum_f = sum(jnp.dot(tril, part, preferred_element_type=F32) for part in parts)
        cum_b = sum(jnp.dot(triu, part, preferred_element_type=F32) for part in parts)
        o_ref[n * c:(n + 1) * c, :] = jnp.where(kind == 0, cum_f, jnp.where(kind == 2, cum_b, beta))


def dn_gates(ab, a_log_row, dt_bias_row, chunks_per_step=GATES_CHUNKS_PER_STEP):
    m, n = ab.shape
    rows = chunks_per_step * DN_CHUNK
    return pl.pallas_call(
        _gates_kernel,
        grid=(m // rows,),
        in_specs=[
            pl.BlockSpec((rows, n), lambda i: (i, 0)),
            pl.BlockSpec((1, n), lambda i: (0, 0)),
            pl.BlockSpec((1, n), lambda i: (0, 0)),
        ],
        out_specs=pl.BlockSpec((rows, n), lambda i: (i, 0)),
        out_shape=jax.ShapeDtypeStruct((m, n), F32),
        compiler_params=_params("arbitrary"),
        name="dn_gates",
    )(ab, a_log_row, dt_bias_row)


def _dot(a, b):
    return jnp.dot(a, b, preferred_element_type=F32)


def _inv_unit_triangular_many(a_list):
    n = a_list[0].shape[0]
    ii = lax.broadcasted_iota(jnp.int32, (n, n), 0)
    jj = lax.broadcasted_iota(jnp.int32, (n, n), 1)
    eye = jnp.where(ii == jj, 1.0, 0.0)
    xs = [eye - jnp.where((ii >> 1) == (jj >> 1), a, 0.0) for a in a_list]
    for level in range(1, int(math.log2(DN_CHUNK))):
        joins = jnp.logical_and((ii >> (level + 1)) == (jj >> (level + 1)), (ii >> level) != (jj >> level))
        ns = [jnp.where(joins, a, 0.0).astype(BF16) for a in a_list]
        xbs = [x.astype(BF16) for x in xs]
        ys = [_dot(xb, nn).astype(BF16) for xb, nn in zip(xbs, ns)]
        xs = [x - _dot(y, xb) for x, y, xb in zip(xs, ys, xbs)]
    return xs


def _dn_core_kernel(*refs, n_chunks, group, heads, has_init, has_state_out):
    q_ref, k_ref, v_ref, z_ref, gr_ref, nw_ref = refs[:6]
    pos = 6
    if has_init:
        s0f_ref, s0b_ref = refs[pos:pos + 2]
        pos += 2
    og_ref = refs[pos]
    pos += 1
    if has_state_out:
        sf_ref, sb_ref = refs[pos:pos + 2]
        pos += 2
    wm_ref, cm_ref, egl_ref, p_ref, u_ref, wv_ref, sh_ref, s_ref = refs[pos:pos + 8]

    c = DN_CHUNK
    pc = 2 * c
    dv = DN_DV
    if has_init:
        for hl in range(heads):
            s_ref[2 * hl] = jnp.concatenate([s0f_ref[2 * hl], s0f_ref[2 * hl + 1]], axis=1)
            s_ref[2 * hl + 1] = jnp.concatenate([s0b_ref[2 * hl], s0b_ref[2 * hl + 1]], axis=1)
    else:
        s_ref[...] = jnp.zeros_like(s_ref)

    ii = lax.broadcasted_iota(jnp.int32, (pc, pc), 0)
    jj = lax.broadcasted_iota(jnp.int32, (pc, pc), 1)
    chunk_shift = int(math.log2(c))
    same_head = (ii >> chunk_shift) == (jj >> chunk_shift)
    top = ii < c
    top_col = lax.broadcasted_iota(jnp.int32, (pc, 1), 0) < c
    nt_dims = (((1,), (1,)), ((), ()))

    def rows_of(chunk):
        start = chunk * c
        return pl.ds(start if isinstance(start, int) else pl.multiple_of(start, c), c)

    def gate_columns(units):
        return [gr_ref[hl, ch].T for hl, ch in units]

    def entry(hl, chunk, d):
        return (hl * n_chunks + chunk) * 2 + d

    def qk_lanes(hl):
        return slice(hl * DN_DK, (hl + 1) * DN_DK)

    def v_lanes(hl, hh):
        return slice((2 * hl + hh) * dv, (2 * hl + hh + 1) * dv)

    def phase_a(chunks):
        units = [(hl, ch) for hl in range(heads) for ch in chunks]
        chains = [(n, d) for n in range(len(units)) for d in (0, 1)]
        k2s, grams = [], []
        for hl, ch in units:
            kc = k_ref[rows_of(ch), qk_lanes(hl)]
            qc = q_ref[rows_of(ch), qk_lanes(hl)]
            k2 = jnp.concatenate([kc, kc], axis=0)
            k2s.append(k2)
            lhs = jnp.concatenate([kc, kc, qc, qc], axis=0)
            grams.append(lax.dot_general(lhs, k2, nt_dims, preferred_element_type=F32))
        colvs = gate_columns(units)
        a_mats, p_mats, gcols, bcols = [], [], [], []
        for n, d in chains:
            hl, ch = units[n]
            gcol, bcol = colvs[n][:, 2 * d:2 * d + 1], colvs[n][:, 2 * d + 1:2 * d + 2]
            grow = gr_ref[hl, ch][2 * d:2 * d + 1, :]
            incl = jnp.logical_and(same_head, (ii >= jj) if d == 0 else (ii <= jj))
            strict = jnp.logical_and(same_head, (ii > jj) if d == 0 else (ii < jj))
            decay = jnp.exp(jnp.where(incl, gcol - grow, MASKED_LOG_DECAY))
            a_mats.append(jnp.where(strict, bcol * grams[n][:pc] * decay, 0.0))
            p_mats.append(grams[n][pc:] * decay)
            gcols.append(gcol)
            bcols.append(bcol)
        t_invs = _inv_unit_triangular_many(a_mats)
        rhss, kfs = [], []
        for (n, d), gcol, bcol in zip(chains, gcols, bcols):
            hl, ch = units[n]
            kf = k2s[n].astype(F32)
            rows = rows_of(ch)
            vp = jnp.concatenate([v_ref[rows, v_lanes(hl, 0)], v_ref[rows, v_lanes(hl, 1)]], axis=0)
            rhss.append(jnp.concatenate([(bcol * jnp.exp(gcol)) * kf, bcol * vp], axis=1).astype(BF16))
            kfs.append(kf)
        uws = [_dot(t.astype(BF16), r) for t, r in zip(t_invs, rhss)]
        kdts, x2s, egls = [], [], []
        for (n, d), gcol, kf, uw in zip(chains, gcols, kfs, uws):
            r0 = c - 1 if d == 0 else 0
            gl0 = gcol[r0:r0 + 1]
            gl1 = gcol[c + r0:c + r0 + 1]
            gl = jnp.where(top_col, gl0, gl1)
            kdts.append((kf * jnp.exp(gl - gcol)).T.astype(BF16))
            u, wv = uw[:, :dv], uw[:, dv:]
            x2s.append(jnp.concatenate([jnp.where(top, u, 0.0), jnp.where(top, 0.0, u),
                                        jnp.where(top, wv, 0.0), jnp.where(top, 0.0, wv)], axis=1).astype(BF16))
            egls.append(jnp.concatenate([jnp.broadcast_to(jnp.exp(gl0), (1, dv)),
                                         jnp.broadcast_to(jnp.exp(gl1), (1, dv))], axis=1))
        wcs = [_dot(kdt, x2) for kdt, x2 in zip(kdts, x2s)]
        for (n, d), p_mat, uw, wc, egl in zip(chains, p_mats, uws, wcs, egls):
            e = entry(*units[n], d)
            wm_ref[e] = wc[:, :2 * dv].astype(BF16)
            cm_ref[e] = wc[:, 2 * dv:]
            egl_ref[e] = egl
            p_ref[e] = p_mat.astype(BF16)
            u_ref[e] = uw[:, :dv].astype(BF16)
            wv_ref[e] = uw[:, dv:]

    def phase_b(n, carry):
        lines = [(hl, d) for hl in range(heads) for d in (0, 1)]
        es = [entry(hl, n if d == 0 else n_chunks - 1 - n, d) for hl, d in lines]
        ss = [s_ref[2 * hl + d] for hl, d in lines]
        sbs = [s.astype(BF16) for s in ss]
        for e, sb in zip(es, sbs):
            sh_ref[e] = sb
        wms = [wm_ref[e] for e in es]
        wss = [jnp.concatenate([_dot(wm[:, :dv], sb[:, :dv]), _dot(wm[:, dv:], sb[:, dv:])], axis=1)
               for wm, sb in zip(wms, sbs)]
        for (hl, d), e, s, ws in zip(lines, es, ss, wss):
            s_ref[2 * hl + d] = egl_ref[e] * s - ws + cm_ref[e]
        return carry

    def phase_c(chunks):
        units = [(hl, ch) for hl in range(heads) for ch in chunks]
        chains = [(n, d) for n in range(len(units)) for d in (0, 1)]
        es = [entry(*units[n], d) for n, d in chains]
        rs = []
        for (n, d), e in zip(chains, es):
            hl, ch = units[n]
            lhs = jnp.concatenate([q_ref[rows_of(ch), qk_lanes(hl)], u_ref[e]], axis=0)
            rs.append(_dot(lhs, sh_ref[e]))
        v_news, q_ss = [], []
        for e, r in zip(es, rs):
            q_ss.append(jnp.concatenate([r[:c, :dv], r[:c, dv:]], axis=0))
            u_s = jnp.concatenate([r[c:2 * c, :dv], r[2 * c:, dv:]], axis=0)
            v_news.append((wv_ref[e] - u_s).astype(BF16))
        pvs = [_dot(p_ref[e], v_new) for e, v_new in zip(es, v_news)]
        colvs = gate_columns(units)
        outs = [jnp.exp(colvs[n][:, 2 * d:2 * d + 1]) * q_s + pv for (n, d), q_s, pv in zip(chains, q_ss, pvs)]
        for n, (hl, ch) in enumerate(units):
            o = outs[2 * n] + outs[2 * n + 1]
            rows = rows_of(ch)
            for hh in range(2):
                y = _rms(o[hh * c:(hh + 1) * c], nw_ref[...])
                og_ref[rows, v_lanes(hl, hh)] = (y * _silu(z_ref[rows, v_lanes(hl, hh)])).astype(og_ref.dtype)

    def over_groups(phase):
        def body(g, carry):
            phase([g * group + i for i in range(group)])
            return carry
        lax.fori_loop(0, n_chunks // group, body, 0)

    if n_chunks <= DN_UNROLLED_SCAN_CHUNKS:
        ready = set()
        for n in range(n_chunks):
            needed = sorted({n, n_chunks - 1 - n} - ready)
            if needed:
                phase_a(needed)
                ready.update(needed)
            phase_b(n, 0)
        phase_c(list(range(n_chunks)))
    else:
        over_groups(phase_a)
        for n in range(n_chunks):
            phase_b(n, 0)
        phase_c(list(range(n_chunks)))

    if has_state_out:
        for hl in range(heads):
            for hh in range(2):
                sf_ref[2 * hl + hh] = s_ref[2 * hl][:, hh * dv:(hh + 1) * dv]
                sb_ref[2 * hl + hh] = s_ref[2 * hl + 1][:, hh * dv:(hh + 1) * dv]


def dn_core(qk, v, z, gates_row, norm_w, layer, *, seq_len, n_seq, row0, heads, group=None, init_states=None):
    n_chunks = seq_len // DN_CHUNK
    rb0 = row0 // seq_len
    qkw = heads * DN_DK
    pair = 2 * DN_DV
    vw = heads * pair
    pc = 2 * DN_CHUNK
    n_e = 2 * n_chunks * heads
    assert DN_HEADS_K % heads == 0
    if n_chunks > DN_UNROLLED_SCAN_CHUNKS:
        assert group is not None and n_chunks % group == 0
    else:
        assert group is None
    has_init = init_states is not None
    in_specs = [
        pl.BlockSpec((seq_len, qkw), lambda s, h: (rb0 + s, h)),
        pl.BlockSpec((seq_len, qkw), lambda s, h: (rb0 + s, DN_HEADS_K // heads + h)),
        pl.BlockSpec((seq_len, vw), lambda s, h: (rb0 + s, h)),
        pl.BlockSpec((seq_len, vw), lambda s, h: (rb0 + s, h)),
        pl.BlockSpec((heads, n_chunks, SUBLANES, 2 * DN_CHUNK), lambda s, h: (h, rb0 + s, 0, 0)),
        pl.BlockSpec((None, 1, DN_DV), lambda s, h: (layer, 0, 0)),
    ]
    args = [qk, qk, v, z, gates_row, norm_w]
    og_shape = jax.ShapeDtypeStruct((n_seq * seq_len, DN_V_DIM), BF16)
    og_spec = pl.BlockSpec((seq_len, vw), lambda s, h: (s, h))
    if has_init:
        st_spec = pl.BlockSpec((None, None, 2 * heads, DN_DK, DN_DV), lambda s, h: (s, layer, h, 0, 0))
        in_specs += [st_spec, st_spec]
        args += [init_states[0], init_states[1]]
        out_specs = og_spec
        out_shape = og_shape
    else:
        so_spec = pl.BlockSpec((None, None, 2 * heads, DN_DK, DN_DV), lambda s, h: (s, 0, h, 0, 0))
        so_shape = jax.ShapeDtypeStruct((n_seq, 1, DN_HEADS_V, DN_DK, DN_DV), F32)
        out_specs = [og_spec, so_spec, so_spec]
        out_shape = [og_shape, so_shape, so_shape]
    return pl.pallas_call(
        functools.partial(_dn_core_kernel, n_chunks=n_chunks, group=group, heads=heads, has_init=has_init,
                          has_state_out=not has_init),
        grid=(n_seq, DN_HEADS_K // heads),
        in_specs=in_specs,
        out_specs=out_specs,
        out_shape=out_shape,
        scratch_shapes=[
            pltpu.VMEM((n_e, DN_DK, pair), BF16),
            pltpu.VMEM((n_e, DN_DK, pair), F32),
            pltpu.VMEM((n_e, 1, pair), F32),
            pltpu.VMEM((n_e, pc, pc), BF16),
            pltpu.VMEM((n_e, pc, DN_DK), BF16),
            pltpu.VMEM((n_e, pc, DN_DV), F32),
            pltpu.VMEM((n_e, DN_DK, pair), BF16),
            pltpu.VMEM((2 * heads, DN_DK, pair), F32),
        ],
        compiler_params=_params("arbitrary", "arbitrary"),
        name="dn_core_lat" if has_init else "dn_core_ctx",
    )(*args)


def _cm_gate_kernel(u_ref, v_ref, lw_ref, lb_ref, ws_ref, bs_ref, o_ref):
    for n in range(u_ref.shape[0] // CM_CHUNK):
        rows = slice(n * CM_CHUNK, (n + 1) * CM_CHUNK)
        v = v_ref[rows, :].astype(F32)
        mu = jnp.mean(v, axis=-1, keepdims=True)
        vc = v - mu
        vn = vc * lax.rsqrt(jnp.mean(vc * vc, axis=-1, keepdims=True) + EPS) * lw_ref[...] + lb_ref[...]
        vn = vn.astype(BF16)
        for g in range(CM_GROUPS):
            lanes = slice(g * CM_GDIM, (g + 1) * CM_GDIM)
            sp = jnp.dot(ws_ref[g].astype(BF16), vn[:, lanes], preferred_element_type=F32) + bs_ref[g]
            o_ref[rows, lanes] = (u_ref[rows, lanes].astype(F32) * sp).astype(o_ref.dtype)


def cm_gate(zz, ln_w, ln_b, w_s, b_s_col, layer, chunks_per_step=CM_GATE_CHUNKS_PER_STEP):
    m = zz.shape[0]
    rows = chunks_per_step * CM_CHUNK
    row = pl.BlockSpec((None, 1, CM_DIM), lambda i: (layer, 0, 0))
    return pl.pallas_call(
        _cm_gate_kernel,
        grid=(m // rows,),
        in_specs=[
            pl.BlockSpec((rows, CM_DIM), lambda i: (i, 0)),
            pl.BlockSpec((rows, CM_DIM), lambda i: (i, 1)),
            row,
            row,
            pl.BlockSpec((None, CM_GROUPS, CM_CHUNK, CM_CHUNK), lambda i: (layer, 0, 0, 0)),
            pl.BlockSpec((None, CM_GROUPS, CM_CHUNK, 1), lambda i: (layer, 0, 0, 0)),
        ],
        out_specs=pl.BlockSpec((rows, CM_DIM), lambda i: (i, 0)),
        out_shape=jax.ShapeDtypeStruct((m, CM_DIM), BF16),
        compiler_params=_params("arbitrary"),
        name="cm_gate",
    )(zz, zz, ln_w, ln_b, w_s, b_s_col)


def _ffn_kernel(*refs, tail):
    x_ref, nw_ref, sh_ref, sc_ref, g_ref, w1_ref, w2_ref, tw_ref = refs[:8]
    if tail == "next_h":
        tsh_ref, tsc_ref, o_ref, hn_ref, h_ref = refs[8:]
    else:
        o_ref, h_ref = refs[8:]
    kk = pl.program_id(1)

    @pl.when(kk == 0)
    def _():
        y = _rms(x_ref[...], nw_ref[...])
        h_ref[...] = (y * (1.0 + sc_ref[...]) + sh_ref[...]).astype(BF16)
        o_ref[...] = jnp.zeros_like(o_ref)

    a = jnp.dot(h_ref[...], w1_ref[...].astype(BF16), preferred_element_type=F32)
    a = jnp.square(jnp.maximum(a, 0.0)).astype(BF16)
    o_ref[...] += jnp.dot(a, w2_ref[...].astype(BF16), preferred_element_type=F32)

    @pl.when(kk == pl.num_programs(1) - 1)
    def _():
        r = x_ref[...] + g_ref[...] * o_ref[...]
        if tail == "final_norm":
            r = _rms(r, tw_ref[...])
        else:
            hn_ref[...] = (_rms(r, tw_ref[...]) * (1.0 + tsc_ref[...]) + tsh_ref[...]).astype(BF16)
        o_ref[...] = r


def ffn(x, norm_w, layer, shift, scale, gate, w1, w2, *, tail, tail_w, tail_mod=None, row0=0, n_rows=None,
        tm=ROW_TILE, tk=FFN_K_TILE):
    d = x.shape[1]
    n_rows = x.shape[0] if n_rows is None else n_rows
    ff = w1.shape[2]
    t0 = row0 // tm
    mod_spec = pl.BlockSpec((None, 1, d), lambda i, k: (_group_of_row((t0 + i) * tm), 0, 0))
    row_spec = pl.BlockSpec((tm, d), lambda i, k: (i, 0), pipeline_mode=pl.Buffered(1))
    in_specs = [
        pl.BlockSpec((tm, d), lambda i, k: (t0 + i, 0), pipeline_mode=pl.Buffered(1)),
        pl.BlockSpec((None, 1, d), lambda i, k: (layer, 0, 0)),
        mod_spec,
        mod_spec,
        mod_spec,
        pl.BlockSpec((None, d, tk), lambda i, k: (layer, 0, k)),
        pl.BlockSpec((None, tk, d), lambda i, k: (layer, k, 0)),
        pl.BlockSpec((1, d), lambda i, k: (0, 0)),
    ]
    args = [x, norm_w.reshape(norm_w.shape[0], 1, d), shift, scale, gate, w1, w2, tail_w.reshape(1, d)]
    out_specs, out_shape = row_spec, jax.ShapeDtypeStruct((n_rows, d), F32)
    if tail == "next_h":
        in_specs += [mod_spec, mod_spec]
        args += list(tail_mod)
        out_specs = [row_spec, row_spec]
        out_shape = [out_shape, jax.ShapeDtypeStruct((n_rows, d), BF16)]
    return pl.pallas_call(
        functools.partial(_ffn_kernel, tail=tail),
        grid=(n_rows // tm, ff // tk),
        in_specs=in_specs,
        out_specs=out_specs,
        out_shape=out_shape,
        scratch_shapes=[pltpu.VMEM((tm, d), BF16)],
        compiler_params=_params("arbitrary", "arbitrary"),
        name="ffn_" + tail,
    )(*args)


def _grid_pos_embed(n_tokens):
    rows = n_tokens // GRID_W
    r = np.repeat(np.arange(rows), GRID_W).astype(np.float64)
    col = np.tile(np.arange(GRID_W), rows).astype(np.float64)
    quarter = D_MODEL // 4
    freq = 1.0 / (10000.0 ** (np.arange(quarter, dtype=np.float64) / quarter))
    ar = r[:, None] * freq[None, :]
    ac = col[:, None] * freq[None, :]
    return np.concatenate([np.sin(ar), np.cos(ar), np.sin(ac), np.cos(ac)], axis=-1)


def _deltanet_layer(x, h, j, gate, state_f, state_b, dn_w_in, dn_conv_w, dn_a_log, dn_dt_bias, dn_norm_w, dn_w_out):
    tm, tn = ROW_TILE, PROJ_COL_TILE

    def project(col0, n_cols, name, epilogue=_ep_plain, extras=(), out_dtype=F32, tn=tn, row_pieces=1):
        return matmul_ws(h, dn_w_in, j, n_cols, tm=tm, tn=tn, epilogue=epilogue, extras=extras,
                         out_dtype=out_dtype, col_block0=col0 // tn, row_pieces=row_pieces, name=name)

    def conv_w_for(col0):
        return (dn_conv_w, pl.BlockSpec((None, DN_CONV, tn), lambda jj, i: (j, 0, col0 // tn + jj)))

    qk = project(0, 2 * DN_K_DIM, "dn_in_qk", extras=(conv_w_for(0),), out_dtype=BF16, row_pieces=EPILOGUE_ROW_PIECES,
                 epilogue=functools.partial(_ep_conv_silu, normalize=True, n_q_tiles=DN_K_DIM // tn))
    v = project(2 * DN_K_DIM, DN_V_DIM, "dn_in_v", extras=(conv_w_for(2 * DN_K_DIM),), row_pieces=EPILOGUE_ROW_PIECES,
                epilogue=functools.partial(_ep_conv_silu, normalize=False, n_q_tiles=0))
    z = project(DN_QKV_DIM, DN_V_DIM, "dn_in_z")
    ab = project(DN_QKV_DIM + DN_V_DIM, DN_GATE_COLS, "dn_in_gates", tn=DN_GATE_COLS)

    zeros = jnp.zeros((DN_HEADS_V,), F32)
    a_log_row = jnp.concatenate([dn_a_log[j, 0], zeros, dn_a_log[j, 1], zeros])[None, :]
    dt_row = jnp.concatenate([dn_dt_bias[j, 0], zeros, dn_dt_bias[j, 1], zeros])[None, :]
    gates = dn_gates(ab, a_log_row, dt_row)
    m = gates.shape[0]
    g5 = gates.reshape(m // DN_CHUNK, DN_CHUNK, 4, DN_HEADS_K, 2)
    gates_row = g5.transpose(3, 0, 2, 4, 1).reshape(DN_HEADS_K, m // DN_CHUNK, 4, 2 * DN_CHUNK)
    gates_row = jnp.pad(gates_row, ((0, 0), (0, 0), (0, SUBLANES - 4), (0, 0)))
    norm_w = dn_norm_w.reshape(dn_norm_w.shape[0], 1, DN_DV)

    og_ctx, s_f, s_b = dn_core(qk, v, z, gates_row, norm_w, j, seq_len=SEQ, n_seq=BATCH, row0=0,
                               heads=4)
    og_lat = dn_core(qk, v, z, gates_row, norm_w, j, seq_len=DEC_SEQ, n_seq=DEC_BATCH, row0=N_CTX_TOK,
                     heads=2, group=4, init_states=(state_f, state_b))
    x = matmul_resid_gate([og_ctx, og_lat], dn_w_out, j, x, gate, tm=DN_OUT_ROW_TILE, tn=OUT_COL_TILE,
                          name="dn_out")
    return x, s_f, s_b


def _chunk_mlp_layer(x, h, j, gate, cm_w_in, cm_b_in, cm_ln_w, cm_ln_b, cm_w_s, cm_b_s, cm_w_out):
    n_in = 2 * CM_DIM
    tn = PROJ_COL_TILE
    bias = (cm_b_in.reshape(cm_b_in.shape[0], 1, n_in), pl.BlockSpec((None, 1, tn), lambda jj, i: (j, 0, jj)))
    zz = matmul_ws(h, cm_w_in, j, n_in, tm=ROW_TILE, tn=tn, epilogue=_ep_bias_gelu, extras=(bias,),
                   out_dtype=BF16, row_pieces=EPILOGUE_ROW_PIECES, name="cm_in")
    n_b = cm_ln_w.shape[0]
    uv = cm_gate(zz, cm_ln_w.reshape(n_b, 1, CM_DIM), cm_ln_b.reshape(n_b, 1, CM_DIM), cm_w_s,
                 cm_b_s[..., None], j)
    return matmul_resid_gate(uv, cm_w_out, j, x, gate, tm=ROW_TILE, tn=OUT_COL_TILE, name="cm_out")


def kernel(x_prompt, x_sample, state_dn_fwd, state_dn_bwd, c, c_ctx, norm_mix_w, norm_mlp_w, w_ada, b_ada, dn_w_in, dn_conv_w, dn_A_log, dn_dt_bias, dn_norm_w, dn_w_out, cm_w_in, cm_b_in, cm_ln_w, cm_ln_b, cm_w_s, cm_b_s, cm_w_out, w_ff1, w_ff2, final_norm_w):
    cond = jnp.concatenate([c_ctx[None, :], c, jnp.zeros((N_COND - 1 - DEC_BATCH, D_MODEL), F32)], axis=0)
    mod = adaln_all(cond, w_ada, b_ada)
    mod = mod.reshape(DEPTH, N_COND, N_MOD, D_MODEL).transpose(0, 2, 1, 3)[:, :, :, None, :]
    mods = [[mod[i, t] for t in range(N_MOD)] for i in range(DEPTH)]

    x, h = embed_norm_modulate(x_prompt.reshape(N_CTX_TOK, D_MODEL), x_sample.reshape(N_LAT_TOK, D_MODEL),
                               jnp.asarray(_grid_pos_embed(DEC_SEQ), x_sample.dtype), norm_mix_w, 0,
                               mods[0][0], mods[0][1])
    new_fwd, new_bwd = [], []
    for i in range(DEPTH):
        j = i // N_MIXERS
        _, _, gate, shift2, scale2, gate2 = mods[i]
        if i % N_MIXERS == 0:
            x, s_f, s_b = _deltanet_layer(x, h, j, gate, state_dn_fwd, state_dn_bwd, dn_w_in, dn_conv_w,
                                          dn_A_log, dn_dt_bias, dn_norm_w, dn_w_out)
            new_fwd.append(s_f)
            new_bwd.append(s_b)
        else:
            x = _chunk_mlp_layer(x, h, j, gate, cm_w_in, cm_b_in, cm_ln_w, cm_ln_b, cm_w_s, cm_b_s, cm_w_out)
        mlp = functools.partial(ffn, x, norm_mlp_w, i, shift2, scale2, gate2, w_ff1, w_ff2)
        if i < DEPTH - 1:
            x, h = mlp(tail="next_h", tail_w=norm_mix_w[i + 1], tail_mod=mods[i + 1][:2])
        else:
            y_ctx = mlp(tail="final_norm", tail_w=final_norm_w, row0=0, n_rows=N_CTX_TOK)
            y_lat = mlp(tail="final_norm", tail_w=final_norm_w, row0=N_CTX_TOK, n_rows=N_LAT_TOK)

    y_prompt = y_ctx.reshape(BATCH, SEQ, D_MODEL)
    y_sample = y_lat.reshape(DEC_BATCH, DEC_SEQ, D_MODEL)
    return (y_prompt, y_sample, jnp.concatenate(new_fwd, axis=1), jnp.concatenate(new_bwd, axis=1))
```

```python
import functools
import math

import jax
import jax.numpy as jnp
import numpy as np
from jax import lax
from jax.experimental import pallas as pl
from jax.experimental.pallas import tpu as pltpu

F32 = jnp.float32
BF16 = jnp.bfloat16

D_MODEL = 2048
BATCH = 16
SEQ = 256
DEPTH = 2
DEC_BATCH = 2
DEC_SEQ = 1024
GRID_W = 64
N_MIXERS = 2
DN_DK = 128
DN_DV = 128
DN_HEADS_K = D_MODEL // DN_DK
DN_HEADS_V = 2 * DN_HEADS_K
DN_K_DIM = DN_HEADS_K * DN_DK
DN_V_DIM = DN_HEADS_V * DN_DV
DN_QKV_DIM = 2 * DN_K_DIM + DN_V_DIM
DN_GATE_COLS = 4 * DN_HEADS_V
DN_CONV = 5
DN_CHUNK = 64
DN_UNROLLED_SCAN_CHUNKS = 4
CM_DIM = 2 * D_MODEL
CM_CHUNK = 128
CM_GROUPS = 16
CM_GDIM = CM_DIM // CM_GROUPS
FF_DIM = 4 * D_MODEL
N_MOD = 6
EPS = 1e-6

N_CTX_TOK = BATCH * SEQ
N_LAT_TOK = DEC_BATCH * DEC_SEQ
N_TOK = N_CTX_TOK + N_LAT_TOK
N_COND = 8
SUBLANES = 8
CONV_HALO = SUBLANES
MASKED_LOG_DECAY = -1e30
V7X_VMEM_BYTES = 64 * 1024 * 1024
VMEM_LIMIT_BYTES = V7X_VMEM_BYTES * 7 // 8

ROW_TILE = 1024
PROJ_COL_TILE = 1024
OUT_COL_TILE = 512
DN_OUT_ROW_TILE = 512
FFN_K_TILE = 512
CONV_ROW_PIECES = 1
ADALN_COL_TILE = 1024
EMBED_ROW_TILE = 512
GATES_CHUNKS_PER_STEP = 8
CM_GATE_CHUNKS_PER_STEP = 2


def _group_of_row(row0):
    return jnp.where(row0 < N_CTX_TOK, 0, 1 + (row0 - N_CTX_TOK) // DEC_SEQ)


def _params(*sem):
    return pltpu.CompilerParams(dimension_semantics=sem, vmem_limit_bytes=VMEM_LIMIT_BYTES)


def _rms(x, w):
    return x * lax.rsqrt(jnp.mean(x * x, axis=-1, keepdims=True) + EPS) * w


def _silu(x):
    half = 0.5 * x
    return half + half * jnp.tanh(half)


def _gelu_tanh(x):
    return 0.5 * x * (1.0 + jnp.tanh(math.sqrt(2.0 / math.pi) * (x + 0.044715 * (x * x * x))))


def _adaln_kernel(c_ref, w_ref, b_ref, o_ref):
    x = _silu(c_ref[...]).astype(BF16)
    acc = jnp.dot(x, w_ref[...].astype(BF16), preferred_element_type=F32)
    o_ref[...] = acc + b_ref[...]


def adaln_all(cond, w_ada, b_ada, tn=ADALN_COL_TILE):
    depth, d, n = w_ada.shape
    return pl.pallas_call(
        _adaln_kernel,
        grid=(depth, n // tn),
        in_specs=[
            pl.BlockSpec((N_COND, d), lambda l, j: (0, 0)),
            pl.BlockSpec((None, d, tn), lambda l, j: (l, 0, j)),
            pl.BlockSpec((None, 1, tn), lambda l, j: (l, 0, j)),
        ],
        out_specs=pl.BlockSpec((None, N_COND, tn), lambda l, j: (l, 0, j)),
        out_shape=jax.ShapeDtypeStruct((depth, N_COND, n), F32),
        compiler_params=_params("arbitrary", "arbitrary"),
        name="adaln",
    )(cond, w_ada, b_ada.reshape(depth, 1, n))


def _embed_kernel(xc_ref, xl_ref, pe_ref, w_ref, sh_ref, sc_ref, x_ref, h_ref, *, ctx_tiles):
    def emit(x):
        x_ref[...] = x
        h_ref[...] = (_rms(x, w_ref[...]) * (1.0 + sc_ref[...]) + sh_ref[...]).astype(BF16)

    i = pl.program_id(0)
    pl.when(i < ctx_tiles)(lambda: emit(xc_ref[...]))
    pl.when(i >= ctx_tiles)(lambda: emit(xl_ref[...] + pe_ref[...]))


def embed_norm_modulate(x_ctx, x_lat, pos_table, norm_w, layer, shift, scale, tm=EMBED_ROW_TILE):
    d = x_ctx.shape[1]
    ctx_tiles = x_ctx.shape[0] // tm
    lat_tiles = x_lat.shape[0] // tm
    pos_tiles = pos_table.shape[0] // tm
    m = x_ctx.shape[0] + x_lat.shape[0]
    mod_spec = pl.BlockSpec((None, 1, d), lambda i: (_group_of_row(i * tm), 0, 0))
    row_spec = pl.BlockSpec((tm, d), lambda i: (i, 0))
    return pl.pallas_call(
        functools.partial(_embed_kernel, ctx_tiles=ctx_tiles),
        grid=(ctx_tiles + lat_tiles,),
        in_specs=[
            pl.BlockSpec((tm, d), lambda i: (jnp.minimum(i, ctx_tiles - 1), 0)),
            pl.BlockSpec((tm, d), lambda i: (jnp.clip(i - ctx_tiles, 0, lat_tiles - 1), 0)),
            pl.BlockSpec((tm, d), lambda i: (jnp.maximum(i - ctx_tiles, 0) % pos_tiles, 0)),
            pl.BlockSpec((None, 1, d), lambda i: (layer, 0, 0)),
            mod_spec,
            mod_spec,
        ],
        out_specs=[row_spec, row_spec],
        out_shape=[jax.ShapeDtypeStruct((m, d), F32), jax.ShapeDtypeStruct((m, d), BF16)],
        compiler_params=_params("arbitrary"),
        name="embed_norm_modulate",
    )(x_ctx, x_lat, pos_table, norm_w.reshape(norm_w.shape[0], 1, d), shift, scale)


def _mm_kernel(*refs, part_tiles, n_extra, epilogue, row_pieces):
    n_parts = len(part_tiles)
    x_refs = refs[:n_parts]
    w_ref = refs[n_parts]
    extra = refs[n_parts + 1:n_parts + 1 + n_extra]
    o_ref = refs[n_parts + 1 + n_extra]
    wb_ref = refs[n_parts + 2 + n_extra]
    i = pl.program_id(1)

    @pl.when(i == 0)
    def _():
        wb_ref[...] = w_ref[...].astype(BF16)

    def run(x_ref):
        extra_vals = [r[...] for r in extra]
        rows = x_ref.shape[0] // row_pieces
        accs = []
        for s in range(row_pieces):
            accs.append(jnp.dot(x_ref[s * rows:(s + 1) * rows, :], wb_ref[...], preferred_element_type=F32))
            if s >= 1:
                epilogue(accs, s - 1, o_ref, *extra_vals)
        epilogue(accs, row_pieces - 1, o_ref, *extra_vals)

    if n_parts == 1:
        run(x_refs[0])
    else:
        first = 0
        for x_ref, tiles in zip(x_refs, part_tiles):
            pl.when(jnp.logical_and(i >= first, i < first + tiles))(functools.partial(run, x_ref))
            first += tiles


def matmul_ws(x, w, layer, n_out, *, tm, tn, epilogue, extras=(), out_dtype=F32, col_block0=0, row_pieces=1,
              name="matmul"):
    parts = list(x) if isinstance(x, (list, tuple)) else [x]
    k = parts[0].shape[1]
    part_tiles = [part.shape[0] // tm for part in parts]
    m = sum(part.shape[0] for part in parts)
    in_specs = []
    first = 0
    for tiles in part_tiles:
        in_specs.append(pl.BlockSpec(
            (tm, k), lambda j, i, first=first, tiles=tiles: (jnp.clip(i - first, 0, tiles - 1), 0)))
        first += tiles
    in_specs.append(pl.BlockSpec((None, k, tn), lambda j, i: (layer, 0, j + col_block0)))
    in_specs += [spec for _, spec in extras]
    return pl.pallas_call(
        functools.partial(_mm_kernel, part_tiles=tuple(part_tiles), n_extra=len(extras), epilogue=epilogue,
                          row_pieces=row_pieces),
        grid=(n_out // tn, m // tm),
        in_specs=in_specs,
        out_specs=pl.BlockSpec((tm, tn), lambda j, i: (i, j)),
        out_shape=jax.ShapeDtypeStruct((m, n_out), out_dtype),
        scratch_shapes=[pltpu.VMEM((k, tn), BF16)],
        compiler_params=_params("arbitrary", "arbitrary"),
        name=name,
    )(*parts, w, *[a for a, _ in extras])


def _piece_rows(accs, s):
    rows = accs[s].shape[0]
    return slice(s * rows, (s + 1) * rows)


def _ep_plain(accs, s, o_ref):
    o_ref[_piece_rows(accs, s), :] = accs[s].astype(o_ref.dtype)


def _ep_bias_gelu(accs, s, o_ref, b):
    o_ref[_piece_rows(accs, s), :] = _gelu_tanh(accs[s] + b).astype(o_ref.dtype)


def _ep_resid_gate(accs, s, o_ref, resid, gate):
    rows = _piece_rows(accs, s)
    o_ref[rows, :] = (resid[rows, :] + gate * accs[s]).astype(o_ref.dtype)


def _ep_conv_silu(accs, s, o_ref, w, *, normalize, n_q_tiles):
    p_rows = accs[s].shape[0]
    tile_rows, tn = o_ref.shape
    blk = min(p_rows, SEQ)
    assert tile_rows == DEC_SEQ and p_rows % blk == 0 and SEQ % blk == 0
    tile_is_context = pl.program_id(1) * tile_rows < N_CTX_TOK
    pad = DN_CONV // 2
    n_ext = blk + 2 * CONV_HALO
    scale = jnp.where(pl.program_id(0) < n_q_tiles, DN_DK ** -0.5, 1.0)

    def rows_at(g, n):
        return accs[g // p_rows][g % p_rows:g % p_rows + n]

    def halo(boundary, g):
        if boundary in (0, tile_rows):
            return jnp.zeros((CONV_HALO, tn), F32)
        if boundary % SEQ == 0:
            return jnp.where(tile_is_context, 0.0, rows_at(g, CONV_HALO))
        return rows_at(g, CONV_HALO)

    for b in range(p_rows // blk):
        g0 = s * p_rows + b * blk
        ext = jnp.concatenate([halo(g0, g0 - CONV_HALO), rows_at(g0, blk), halo(g0 + blk, g0 + blk)], axis=0)
        y = None
        for t in range(DN_CONV):
            shifted = ext if t == pad else pltpu.roll(ext, (pad - t) % n_ext, 0)
            term = shifted[CONV_HALO:CONV_HALO + blk] * w[t:t + 1, :]
            y = term if y is None else y + term
        y = _silu(y)
        rows = slice(g0, g0 + blk)
        if normalize:
            for hh in range(tn // DN_DK):
                lanes = slice(hh * DN_DK, (hh + 1) * DN_DK)
                sl = y[:, lanes]
                inv = lax.rsqrt(jnp.sum(sl * sl, axis=-1, keepdims=True) + EPS) * scale
                o_ref[rows, lanes] = (sl * inv).astype(o_ref.dtype)
        else:
            o_ref[rows, :] = y.astype(o_ref.dtype)


def matmul_resid_gate(x, w, layer, resid, gate, *, tm, tn, name):
    n_out = resid.shape[1]
    extras = (
        (resid, pl.BlockSpec((tm, tn), lambda j, i: (i, j))),
        (gate, pl.BlockSpec((None, 1, tn), lambda j, i: (_group_of_row(i * tm), 0, j))),
    )
    return matmul_ws(x, w, layer, n_out, tm=tm, tn=tn, epilogue=_ep_resid_gate, extras=extras, name=name)


def _split3(x):
    hi = x.astype(BF16)
    r1 = x - hi.astype(F32)
    mid = r1.astype(BF16)
    lo = (r1 - mid.astype(F32)).astype(BF16)
    return hi, mid, lo


def _gates_kernel(ab_ref, alog_ref, dtb_ref, o_ref):
    c = DN_CHUNK
    ii = lax.broadcasted_iota(jnp.int32, (c, c), 0)
    jj = lax.broadcasted_iota(jnp.int32, (c, c), 1)
    tril = (ii >= jj).astype(BF16)
    triu = (ii <= jj).astype(BF16)
    lane = lax.broadcasted_iota(jnp.int32, (c, ab_ref.shape[1]), 1)
    kind = lane // DN_HEADS_V
    for n in range(ab_ref.shape[0] // c):
        ab = ab_ref[n * c:(n + 1) * c, :]
        log_g = -jnp.exp(alog_ref[...]) * jax.nn.softplus(ab + dtb_ref[...])
        beta = jax.nn.sigmoid(ab)
        parts = _split3(log_g)
        cum_f = sum(jnp.dot(tril, part, preferred_element_type=F32) for part in parts)
        cum_b = sum(jnp.dot(triu, part, preferred_element_type=F32) for part in parts)
        o_ref[n * c:(n + 1) * c, :] = jnp.where(kind == 0, cum_f, jnp.where(kind == 2, cum_b, beta))


def dn_gates(ab, a_log_row, dt_bias_row, chunks_per_step=GATES_CHUNKS_PER_STEP):
    m, n = ab.shape
    rows = chunks_per_step * DN_CHUNK
    return pl.pallas_call(
        _gates_kernel,
        grid=(m // rows,),
        in_specs=[
            pl.BlockSpec((rows, n), lambda i: (i, 0)),
            pl.BlockSpec((1, n), lambda i: (0, 0)),
            pl.BlockSpec((1, n), lambda i: (0, 0)),
        ],
        out_specs=pl.BlockSpec((rows, n), lambda i: (i, 0)),
        out_shape=jax.ShapeDtypeStruct((m, n), F32),
        compiler_params=_params("arbitrary"),
        name="dn_gates",
    )(ab, a_log_row, dt_bias_row)


def _dot(a, b):
    return jnp.dot(a, b, preferred_element_type=F32)


def _inv_unit_triangular_many(a_list):
    n = a_list[0].shape[0]
    ii = lax.broadcasted_iota(jnp.int32, (n, n), 0)
    jj = lax.broadcasted_iota(jnp.int32, (n, n), 1)
    eye = jnp.where(ii == jj, 1.0, 0.0)
    xs = [eye - jnp.where((ii >> 1) == (jj >> 1), a, 0.0) for a in a_list]
    for level in range(1, int(math.log2(DN_CHUNK))):
        joins = jnp.logical_and((ii >> (level + 1)) == (jj >> (level + 1)), (ii >> level) != (jj >> level))
        ns = [jnp.where(joins, a, 0.0).astype(BF16) for a in a_list]
        xbs = [x.astype(BF16) for x in xs]
        ys = [_dot(xb, nn).astype(BF16) for xb, nn in zip(xbs, ns)]
        xs = [x - _dot(y, xb) for x, y, xb in zip(xs, ys, xbs)]
    return xs


def _dn_core_kernel(*refs, n_chunks, group, heads, has_init, has_state_out):
    q_ref, k_ref, v_ref, z_ref, gr_ref, nw_ref = refs[:6]
    pos = 6
    if has_init:
        s0f_ref, s0b_ref = refs[pos:pos + 2]
        pos += 2
    og_ref = refs[pos]
    pos += 1
    if has_state_out:
        sf_ref, sb_ref = refs[pos:pos + 2]
        pos += 2
    wm_ref, cm_ref, egl_ref, p_ref, u_ref, wv_ref, sh_ref, s_ref = refs[pos:pos + 8]

    c = DN_CHUNK
    pc = 2 * c
    dv = DN_DV
    if has_init:
        for hl in range(heads):
            s_ref[2 * hl] = jnp.concatenate([s0f_ref[2 * hl], s0f_ref[2 * hl + 1]], axis=1)
            s_ref[2 * hl + 1] = jnp.concatenate([s0b_ref[2 * hl], s0b_ref[2 * hl + 1]], axis=1)
    else:
        s_ref[...] = jnp.zeros_like(s_ref)

    ii = lax.broadcasted_iota(jnp.int32, (pc, pc), 0)
    jj = lax.broadcasted_iota(jnp.int32, (pc, pc), 1)
    chunk_shift = int(math.log2(c))
    same_head = (ii >> chunk_shift) == (jj >> chunk_shift)
    top = ii < c
    top_col = lax.broadcasted_iota(jnp.int32, (pc, 1), 0) < c
    nt_dims = (((1,), (1,)), ((), ()))

    def rows_of(chunk):
        start = chunk * c
        return pl.ds(start if isinstance(start, int) else pl.multiple_of(start, c), c)

    def gate_columns(units):
        return [gr_ref[hl, ch].T for hl, ch in units]

    def entry(hl, chunk, d):
        return (hl * n_chunks + chunk) * 2 + d

    def qk_lanes(hl):
        return slice(hl * DN_DK, (hl + 1) * DN_DK)

    def v_lanes(hl, hh):
        return slice((2 * hl + hh) * dv, (2 * hl + hh + 1) * dv)

    def phase_a(chunks):
        units = [(hl, ch) for hl in range(heads) for ch in chunks]
        chains = [(n, d) for n in range(len(units)) for d in (0, 1)]
        k2s, grams = [], []
        for hl, ch in units:
            kc = k_ref[rows_of(ch), qk_lanes(hl)]
            qc = q_ref[rows_of(ch), qk_lanes(hl)]
            k2 = jnp.concatenate([kc, kc], axis=0)
            k2s.append(k2)
            lhs = jnp.concatenate([kc, kc, qc, qc], axis=0)
            grams.append(lax.dot_general(lhs, k2, nt_dims, preferred_element_type=F32))
        colvs = gate_columns(units)
        a_mats, p_mats, gcols, bcols = [], [], [], []
        for n, d in chains:
            hl, ch = units[n]
            gcol, bcol = colvs[n][:, 2 * d:2 * d + 1], colvs[n][:, 2 * d + 1:2 * d + 2]
            grow = gr_ref[hl, ch][2 * d:2 * d + 1, :]
            incl = jnp.logical_and(same_head, (ii >= jj) if d == 0 else (ii <= jj))
            strict = jnp.logical_and(same_head, (ii > jj) if d == 0 else (ii < jj))
            decay = jnp.exp(jnp.where(incl, gcol - grow, MASKED_LOG_DECAY))
            a_mats.append(jnp.where(strict, bcol * grams[n][:pc] * decay, 0.0))
            p_mats.append(grams[n][pc:] * decay)
            gcols.append(gcol)
            bcols.append(bcol)
        t_invs = _inv_unit_triangular_many(a_mats)
        rhss, kfs = [], []
        for (n, d), gcol, bcol in zip(chains, gcols, bcols):
            hl, ch = units[n]
            kf = k2s[n].astype(F32)
            rows = rows_of(ch)
            vp = jnp.concatenate([v_ref[rows, v_lanes(hl, 0)], v_ref[rows, v_lanes(hl, 1)]], axis=0)
            rhss.append(jnp.concatenate([(bcol * jnp.exp(gcol)) * kf, bcol * vp], axis=1).astype(BF16))
            kfs.append(kf)
        uws = [_dot(t.astype(BF16), r) for t, r in zip(t_invs, rhss)]
        kdts, x2s, egls = [], [], []
        for (n, d), gcol, kf, uw in zip(chains, gcols, kfs, uws):
            r0 = c - 1 if d == 0 else 0
            gl0 = gcol[r0:r0 + 1]
            gl1 = gcol[c + r0:c + r0 + 1]
            gl = jnp.where(top_col, gl0, gl1)
            kdts.append((kf * jnp.exp(gl - gcol)).T.astype(BF16))
            u, wv = uw[:, :dv], uw[:, dv:]
            x2s.append(jnp.concatenate([jnp.where(top, u, 0.0), jnp.where(top, 0.0, u),
                                        jnp.where(top, wv, 0.0), jnp.where(top, 0.0, wv)], axis=1).astype(BF16))
            egls.append(jnp.concatenate([jnp.broadcast_to(jnp.exp(gl0), (1, dv)),
                                         jnp.broadcast_to(jnp.exp(gl1), (1, dv))], axis=1))
        wcs = [_dot(kdt, x2) for kdt, x2 in zip(kdts, x2s)]
        for (n, d), p_mat, uw, wc, egl in zip(chains, p_mats, uws, wcs, egls):
            e = entry(*units[n], d)
            wm_ref[e] = wc[:, :2 * dv].astype(BF16)
            cm_ref[e] = wc[:, 2 * dv:]
            egl_ref[e] = egl
            p_ref[e] = p_mat.astype(BF16)
            u_ref[e] = uw[:, :dv].astype(BF16)
            wv_ref[e] = uw[:, dv:]

    def phase_b(n, carry):
        lines = [(hl, d) for hl in range(heads) for d in (0, 1)]
        es = [entry(hl, n if d == 0 else n_chunks - 1 - n, d) for hl, d in lines]
        ss = [s_ref[2 * hl + d] for hl, d in lines]
        sbs = [s.astype(BF16) for s in ss]
        for e, sb in zip(es, sbs):
            sh_ref[e] = sb
        wms = [wm_ref[e] for e in es]
        wss = [jnp.concatenate([_dot(wm[:, :dv], sb[:, :dv]), _dot(wm[:, dv:], sb[:, dv:])], axis=1)
               for wm, sb in zip(wms, sbs)]
        for (hl, d), e, s, ws in zip(lines, es, ss, wss):
            s_ref[2 * hl + d] = egl_ref[e] * s - ws + cm_ref[e]
        return carry

    def phase_c(chunks):
        units = [(hl, ch) for hl in range(heads) for ch in chunks]
        chains = [(n, d) for n in range(len(units)) for d in (0, 1)]
        es = [entry(*units[n], d) for n, d in chains]
        rs = []
        for (n, d), e in zip(chains, es):
            hl, ch = units[n]
            lhs = jnp.concatenate([q_ref[rows_of(ch), qk_lanes(hl)], u_ref[e]], axis=0)
            rs.append(_dot(lhs, sh_ref[e]))
        v_news, q_ss = [], []
        for e, r in zip(es, rs):
            q_ss.append(jnp.concatenate([r[:c, :dv], r[:c, dv:]], axis=0))
            u_s = jnp.concatenate([r[c:2 * c, :dv], r[2 * c:, dv:]], axis=0)
            v_news.append((wv_ref[e] - u_s).astype(BF16))
        pvs = [_dot(p_ref[e], v_new) for e, v_new in zip(es, v_news)]
        colvs = gate_columns(units)
        outs = [jnp.exp(colvs[n][:, 2 * d:2 * d + 1]) * q_s + pv for (n, d), q_s, pv in zip(chains, q_ss, pvs)]
        for n, (hl, ch) in enumerate(units):
            o = outs[2 * n] + outs[2 * n + 1]
            rows = rows_of(ch)
            for hh in range(2):
                y = _rms(o[hh * c:(hh + 1) * c], nw_ref[...])
                og_ref[rows, v_lanes(hl, hh)] = (y * _silu(z_ref[rows, v_lanes(hl, hh)])).astype(og_ref.dtype)

    def over_groups(phase):
        def body(g, carry):
            phase([g * group + i for i in range(group)])
            return carry
        lax.fori_loop(0, n_chunks // group, body, 0)

    if n_chunks <= DN_UNROLLED_SCAN_CHUNKS:
        ready = set()
        for n in range(n_chunks):
            needed = sorted({n, n_chunks - 1 - n} - ready)
            if needed:
                phase_a(needed)
                ready.update(needed)
            phase_b(n, 0)
        phase_c(list(range(n_chunks)))
    else:
        over_groups(phase_a)
        for n in range(n_chunks):
            phase_b(n, 0)
        phase_c(list(range(n_chunks)))

    if has_state_out:
        for hl in range(heads):
            for hh in range(2):
                sf_ref[2 * hl + hh] = s_ref[2 * hl][:, hh * dv:(hh + 1) * dv]
                sb_ref[2 * hl + hh] = s_ref[2 * hl + 1][:, hh * dv:(hh + 1) * dv]


def dn_core(qk, v, z, gates_row, norm_w, layer, *, seq_len, n_seq, row0, heads, group=None, init_states=None):
    n_chunks = seq_len // DN_CHUNK
    rb0 = row0 // seq_len
    qkw = heads * DN_DK
    pair = 2 * DN_DV
    vw = heads * pair
    pc = 2 * DN_CHUNK
    n_e = 2 * n_chunks * heads
    assert DN_HEADS_K % heads == 0
    if n_chunks > DN_UNROLLED_SCAN_CHUNKS:
        assert group is not None and n_chunks % group == 0
    else:
        assert group is None
    has_init = init_states is not None
    in_specs = [
        pl.BlockSpec((seq_len, qkw), lambda s, h: (rb0 + s, h)),
        pl.BlockSpec((seq_len, qkw), lambda s, h: (rb0 + s, DN_HEADS_K // heads + h)),
        pl.BlockSpec((seq_len, vw), lambda s, h: (rb0 + s, h)),
        pl.BlockSpec((seq_len, vw), lambda s, h: (rb0 + s, h)),
        pl.BlockSpec((heads, n_chunks, SUBLANES, 2 * DN_CHUNK), lambda s, h: (h, rb0 + s, 0, 0)),
        pl.BlockSpec((None, 1, DN_DV), lambda s, h: (layer, 0, 0)),
    ]
    args = [qk, qk, v, z, gates_row, norm_w]
    og_shape = jax.ShapeDtypeStruct((n_seq * seq_len, DN_V_DIM), BF16)
    og_spec = pl.BlockSpec((seq_len, vw), lambda s, h: (s, h))
    if has_init:
        st_spec = pl.BlockSpec((None, None, 2 * heads, DN_DK, DN_DV), lambda s, h: (s, layer, h, 0, 0))
        in_specs += [st_spec, st_spec]
        args += [init_states[0], init_states[1]]
        out_specs = og_spec
        out_shape = og_shape
    else:
        so_spec = pl.BlockSpec((None, None, 2 * heads, DN_DK, DN_DV), lambda s, h: (s, 0, h, 0, 0))
        so_shape = jax.ShapeDtypeStruct((n_seq, 1, DN_HEADS_V, DN_DK, DN_DV), F32)
        out_specs = [og_spec, so_spec, so_spec]
        out_shape = [og_shape, so_shape, so_shape]
    return pl.pallas_call(
        functools.partial(_dn_core_kernel, n_chunks=n_chunks, group=group, heads=heads, has_init=has_init,
                          has_state_out=not has_init),
        grid=(n_seq, DN_HEADS_K // heads),
        in_specs=in_specs,
        out_specs=out_specs,
        out_shape=out_shape,
        scratch_shapes=[
            pltpu.VMEM((n_e, DN_DK, pair), BF16),
            pltpu.VMEM((n_e, DN_DK, pair), F32),
            pltpu.VMEM((n_e, 1, pair), F32),
            pltpu.VMEM((n_e, pc, pc), BF16),
            pltpu.VMEM((n_e, pc, DN_DK), BF16),
            pltpu.VMEM((n_e, pc, DN_DV), F32),
            pltpu.VMEM((n_e, DN_DK, pair), BF16),
            pltpu.VMEM((2 * heads, DN_DK, pair), F32),
        ],
        compiler_params=_params("arbitrary", "arbitrary"),
        name="dn_core_lat" if has_init else "dn_core_ctx",
    )(*args)


def _cm_gate_kernel(u_ref, v_ref, lw_ref, lb_ref, ws_ref, bs_ref, o_ref):
    for n in range(u_ref.shape[0] // CM_CHUNK):
        rows = slice(n * CM_CHUNK, (n + 1) * CM_CHUNK)
        v = v_ref[rows, :].astype(F32)
        mu = jnp.mean(v, axis=-1, keepdims=True)
        vc = v - mu
        vn = vc * lax.rsqrt(jnp.mean(vc * vc, axis=-1, keepdims=True) + EPS) * lw_ref[...] + lb_ref[...]
        vn = vn.astype(BF16)
        for g in range(CM_GROUPS):
            lanes = slice(g * CM_GDIM, (g + 1) * CM_GDIM)
            sp = jnp.dot(ws_ref[g].astype(BF16), vn[:, lanes], preferred_element_type=F32) + bs_ref[g]
            o_ref[rows, lanes] = (u_ref[rows, lanes].astype(F32) * sp).astype(o_ref.dtype)


def cm_gate(zz, ln_w, ln_b, w_s, b_s_col, layer, chunks_per_step=CM_GATE_CHUNKS_PER_STEP):
    m = zz.shape[0]
    rows = chunks_per_step * CM_CHUNK
    row = pl.BlockSpec((None, 1, CM_DIM), lambda i: (layer, 0, 0))
    return pl.pallas_call(
        _cm_gate_kernel,
        grid=(m // rows,),
        in_specs=[
            pl.BlockSpec((rows, CM_DIM), lambda i: (i, 0)),
            pl.BlockSpec((rows, CM_DIM), lambda i: (i, 1)),
            row,
            row,
            pl.BlockSpec((None, CM_GROUPS, CM_CHUNK, CM_CHUNK), lambda i: (layer, 0, 0, 0)),
            pl.BlockSpec((None, CM_GROUPS, CM_CHUNK, 1), lambda i: (layer, 0, 0, 0)),
        ],
        out_specs=pl.BlockSpec((rows, CM_DIM), lambda i: (i, 0)),
        out_shape=jax.ShapeDtypeStruct((m, CM_DIM), BF16),
        compiler_params=_params("arbitrary"),
        name="cm_gate",
    )(zz, zz, ln_w, ln_b, w_s, b_s_col)


def _ffn_kernel(*refs, tail):
    x_ref, nw_ref, sh_ref, sc_ref, g_ref, w1_ref, w2_ref, tw_ref = refs[:8]
    if tail == "next_h":
        tsh_ref, tsc_ref, o_ref, hn_ref, h_ref = refs[8:]
    else:
        o_ref, h_ref = refs[8:]
    kk = pl.program_id(1)

    @pl.when(kk == 0)
    def _():
        y = _rms(x_ref[...], nw_ref[...])
        h_ref[...] = (y * (1.0 + sc_ref[...]) + sh_ref[...]).astype(BF16)
        o_ref[...] = jnp.zeros_like(o_ref)

    a = jnp.dot(h_ref[...], w1_ref[...].astype(BF16), preferred_element_type=F32)
    a = jnp.square(jnp.maximum(a, 0.0)).astype(BF16)
    o_ref[...] += jnp.dot(a, w2_ref[...].astype(BF16), preferred_element_type=F32)

    @pl.when(kk == pl.num_programs(1) - 1)
    def _():
        r = x_ref[...] + g_ref[...] * o_ref[...]
        if tail == "final_norm":
            r = _rms(r, tw_ref[...])
        else:
            hn_ref[...] = (_rms(r, tw_ref[...]) * (1.0 + tsc_ref[...]) + tsh_ref[...]).astype(BF16)
        o_ref[...] = r


def ffn(x, norm_w, layer, shift, scale, gate, w1, w2, *, tail, tail_w, tail_mod=None, row0=0, n_rows=None,
        tm=ROW_TILE, tk=FFN_K_TILE):
    d = x.shape[1]
    n_rows = x.shape[0] if n_rows is None else n_rows
    ff = w1.shape[2]
    t0 = row0 // tm
    mod_spec = pl.BlockSpec((None, 1, d), lambda i, k: (_group_of_row((t0 + i) * tm), 0, 0))
    row_spec = pl.BlockSpec((tm, d), lambda i, k: (i, 0), pipeline_mode=pl.Buffered(1))
    in_specs = [
        pl.BlockSpec((tm, d), lambda i, k: (t0 + i, 0), pipeline_mode=pl.Buffered(1)),
        pl.BlockSpec((None, 1, d), lambda i, k: (layer, 0, 0)),
        mod_spec,
        mod_spec,
        mod_spec,
        pl.BlockSpec((None, d, tk), lambda i, k: (layer, 0, k)),
        pl.BlockSpec((None, tk, d), lambda i, k: (layer, k, 0)),
        pl.BlockSpec((1, d), lambda i, k: (0, 0)),
    ]
    args = [x, norm_w.reshape(norm_w.shape[0], 1, d), shift, scale, gate, w1, w2, tail_w.reshape(1, d)]
    out_specs, out_shape = row_spec, jax.ShapeDtypeStruct((n_rows, d), F32)
    if tail == "next_h":
        in_specs += [mod_spec, mod_spec]
        args += list(tail_mod)
        out_specs = [row_spec, row_spec]
        out_shape = [out_shape, jax.ShapeDtypeStruct((n_rows, d), BF16)]
    return pl.pallas_call(
        functools.partial(_ffn_kernel, tail=tail),
        grid=(n_rows // tm, ff // tk),
        in_specs=in_specs,
        out_specs=out_specs,
        out_shape=out_shape,
        scratch_shapes=[pltpu.VMEM((tm, d), BF16)],
        compiler_params=_params("arbitrary", "arbitrary"),
        name="ffn_" + tail,
    )(*args)


def _grid_pos_embed(n_tokens):
    rows = n_tokens // GRID_W
    r = np.repeat(np.arange(rows), GRID_W).astype(np.float64)
    col = np.tile(np.arange(GRID_W), rows).astype(np.float64)
    quarter = D_MODEL // 4
    freq = 1.0 / (10000.0 ** (np.arange(quarter, dtype=np.float64) / quarter))
    ar = r[:, None] * freq[None, :]
    ac = col[:, None] * freq[None, :]
    return np.concatenate([np.sin(ar), np.cos(ar), np.sin(ac), np.cos(ac)], axis=-1)


def _deltanet_layer(x, h, j, gate, state_f, state_b, dn_w_in, dn_conv_w, dn_a_log, dn_dt_bias, dn_norm_w, dn_w_out):
    tm, tn = ROW_TILE, PROJ_COL_TILE

    def project(col0, n_cols, name, epilogue=_ep_plain, extras=(), out_dtype=F32, tn=tn, row_pieces=1):
        return matmul_ws(h, dn_w_in, j, n_cols, tm=tm, tn=tn, epilogue=epilogue, extras=extras,
                         out_dtype=out_dtype, col_block0=col0 // tn, row_pieces=row_pieces, name=name)

    def conv_w_for(col0):
        return (dn_conv_w, pl.BlockSpec((None, DN_CONV, tn), lambda jj, i: (j, 0, col0 // tn + jj)))

    qk = project(0, 2 * DN_K_DIM, "dn_in_qk", extras=(conv_w_for(0),), out_dtype=BF16, row_pieces=CONV_ROW_PIECES,
                 epilogue=functools.partial(_ep_conv_silu, normalize=True, n_q_tiles=DN_K_DIM // tn))
    v = project(2 * DN_K_DIM, DN_V_DIM, "dn_in_v", extras=(conv_w_for(2 * DN_K_DIM),), row_pieces=CONV_ROW_PIECES,
                epilogue=functools.partial(_ep_conv_silu, normalize=False, n_q_tiles=0))
    z = project(DN_QKV_DIM, DN_V_DIM, "dn_in_z")
    ab = project(DN_QKV_DIM + DN_V_DIM, DN_GATE_COLS, "dn_in_gates", tn=DN_GATE_COLS)

    zeros = jnp.zeros((DN_HEADS_V,), F32)
    a_log_row = jnp.concatenate([dn_a_log[j, 0], zeros, dn_a_log[j, 1], zeros])[None, :]
    dt_row = jnp.concatenate([dn_dt_bias[j, 0], zeros, dn_dt_bias[j, 1], zeros])[None, :]
    gates = dn_gates(ab, a_log_row, dt_row)
    m = gates.shape[0]
    g5 = gates.reshape(m // DN_CHUNK, DN_CHUNK, 4, DN_HEADS_K, 2)
    gates_row = g5.transpose(3, 0, 2, 4, 1).reshape(DN_HEADS_K, m // DN_CHUNK, 4, 2 * DN_CHUNK)
    gates_row = jnp.pad(gates_row, ((0, 0), (0, 0), (0, SUBLANES - 4), (0, 0)))
    norm_w = dn_norm_w.reshape(dn_norm_w.shape[0], 1, DN_DV)

    og_ctx, s_f, s_b = dn_core(qk, v, z, gates_row, norm_w, j, seq_len=SEQ, n_seq=BATCH, row0=0,
                               heads=4)
    og_lat = dn_core(qk, v, z, gates_row, norm_w, j, seq_len=DEC_SEQ, n_seq=DEC_BATCH, row0=N_CTX_TOK,
                     heads=2, group=4, init_states=(state_f, state_b))
    x = matmul_resid_gate([og_ctx, og_lat], dn_w_out, j, x, gate, tm=DN_OUT_ROW_TILE, tn=OUT_COL_TILE,
                          name="dn_out")
    return x, s_f, s_b


def _chunk_mlp_layer(x, h, j, gate, cm_w_in, cm_b_in, cm_ln_w, cm_ln_b, cm_w_s, cm_b_s, cm_w_out):
    n_in = 2 * CM_DIM
    tn = PROJ_COL_TILE
    bias = (cm_b_in.reshape(cm_b_in.shape[0], 1, n_in), pl.BlockSpec((None, 1, tn), lambda jj, i: (j, 0, jj)))
    zz = matmul_ws(h, cm_w_in, j, n_in, tm=ROW_TILE, tn=tn, epilogue=_ep_bias_gelu, extras=(bias,),
                   out_dtype=BF16, name="cm_in")
    n_b = cm_ln_w.shape[0]
    uv = cm_gate(zz, cm_ln_w.reshape(n_b, 1, CM_DIM), cm_ln_b.reshape(n_b, 1, CM_DIM), cm_w_s,
                 cm_b_s[..., None], j)
    return matmul_resid_gate(uv, cm_w_out, j, x, gate, tm=ROW_TILE, tn=OUT_COL_TILE, name="cm_out")


def kernel(x_prompt, x_sample, state_dn_fwd, state_dn_bwd, c, c_ctx, norm_mix_w, norm_mlp_w, w_ada, b_ada, dn_w_in, dn_conv_w, dn_A_log, dn_dt_bias, dn_norm_w, dn_w_out, cm_w_in, cm_b_in, cm_ln_w, cm_ln_b, cm_w_s, cm_b_s, cm_w_out, w_ff1, w_ff2, final_norm_w):
    cond = jnp.concatenate([c_ctx[None, :], c, jnp.zeros((N_COND - 1 - DEC_BATCH, D_MODEL), F32)], axis=0)
    mod = adaln_all(cond, w_ada, b_ada)
    mod = mod.reshape(DEPTH, N_COND, N_MOD, D_MODEL).transpose(0, 2, 1, 3)[:, :, :, None, :]
    mods = [[mod[i, t] for t in range(N_MOD)] for i in range(DEPTH)]

    x, h = embed_norm_modulate(x_prompt.reshape(N_CTX_TOK, D_MODEL), x_sample.reshape(N_LAT_TOK, D_MODEL),
                               jnp.asarray(_grid_pos_embed(DEC_SEQ), x_sample.dtype), norm_mix_w, 0,
                               mods[0][0], mods[0][1])
    new_fwd, new_bwd = [], []
    for i in range(DEPTH):
        j = i // N_MIXERS
        _, _, gate, shift2, scale2, gate2 = mods[i]
        if i % N_MIXERS == 0:
            x, s_f, s_b = _deltanet_layer(x, h, j, gate, state_dn_fwd, state_dn_bwd, dn_w_in, dn_conv_w,
                                          dn_A_log, dn_dt_bias, dn_norm_w, dn_w_out)
            new_fwd.append(s_f)
            new_bwd.append(s_b)
        else:
            x = _chunk_mlp_layer(x, h, j, gate, cm_w_in, cm_b_in, cm_ln_w, cm_ln_b, cm_w_s, cm_b_s, cm_w_out)
        mlp = functools.partial(ffn, x, norm_mlp_w, i, shift2, scale2, gate2, w_ff1, w_ff2)
        if i < DEPTH - 1:
            x, h = mlp(tail="next_h", tail_w=norm_mix_w[i + 1], tail_mod=mods[i + 1][:2])
        else:
            y_ctx = mlp(tail="final_norm", tail_w=final_norm_w, row0=0, n_rows=N_CTX_TOK)
            y_lat = mlp(tail="final_norm", tail_w=final_norm_w, row0=N_CTX_TOK, n_rows=N_LAT_TOK)

    y_prompt = y_ctx.reshape(BATCH, SEQ, D_MODEL)
    y_sample = y_lat.reshape(DEC_BATCH, DEC_SEQ, D_MODEL)
    return (y_prompt, y_sample, jnp.concatenate(new_fwd, axis=1), jnp.concatenate(new_bwd, axis=1))
```

```python
import functools
import math

import jax
import jax.numpy as jnp
import numpy as np
from jax import lax
from jax.experimental import pallas as pl
from jax.experimental.pallas import tpu as pltpu

F32 = jnp.float32
BF16 = jnp.bfloat16

D_MODEL = 2048
BATCH = 16
SEQ = 256
DEPTH = 2
DEC_BATCH = 2
DEC_SEQ = 1024
GRID_W = 64
N_MIXERS = 2
DN_DK = 128
DN_DV = 128
DN_HEADS_K = D_MODEL // DN_DK
DN_HEADS_V = 2 * DN_HEADS_K
DN_K_DIM = DN_HEADS_K * DN_DK
DN_V_DIM = DN_HEADS_V * DN_DV
DN_QKV_DIM = 2 * DN_K_DIM + DN_V_DIM
DN_GATE_COLS = 4 * DN_HEADS_V
DN_CONV = 5
DN_CHUNK = 64
CM_DIM = 2 * D_MODEL
CM_CHUNK = 128
CM_GROUPS = 16
CM_GDIM = CM_DIM // CM_GROUPS
FF_DIM = 4 * D_MODEL
N_MOD = 6
EPS = 1e-6

N_CTX_TOK = BATCH * SEQ
N_LAT_TOK = DEC_BATCH * DEC_SEQ
N_TOK = N_CTX_TOK + N_LAT_TOK
N_COND = 8
SUBLANES = 8
CONV_HALO = SUBLANES
MASKED_LOG_DECAY = -1e30
V7X_VMEM_BYTES = 64 * 1024 * 1024
VMEM_LIMIT_BYTES = V7X_VMEM_BYTES * 7 // 8

ROW_TILE = 1024
PROJ_COL_TILE = 1024
OUT_COL_TILE = 512
DN_OUT_ROW_TILE = 512
FFN_K_TILE = 512
CONV_ROW_PIECES = 1
ADALN_COL_TILE = 1024
EMBED_ROW_TILE = 512
GATES_CHUNKS_PER_STEP = 8
CM_GATE_CHUNKS_PER_STEP = 2


def _group_of_row(row0):
    return jnp.where(row0 < N_CTX_TOK, 0, 1 + (row0 - N_CTX_TOK) // DEC_SEQ)


def _params(*sem):
    return pltpu.CompilerParams(dimension_semantics=sem, vmem_limit_bytes=VMEM_LIMIT_BYTES)


def _rms(x, w):
    return x * lax.rsqrt(jnp.mean(x * x, axis=-1, keepdims=True) + EPS) * w


def _silu(x):
    half = 0.5 * x
    return half + half * jnp.tanh(half)


def _gelu_tanh(x):
    return 0.5 * x * (1.0 + jnp.tanh(math.sqrt(2.0 / math.pi) * (x + 0.044715 * (x * x * x))))


def _adaln_kernel(c_ref, w_ref, b_ref, o_ref):
    x = _silu(c_ref[...]).astype(BF16)
    acc = jnp.dot(x, w_ref[...].astype(BF16), preferred_element_type=F32)
    o_ref[...] = acc + b_ref[...]


def adaln_all(cond, w_ada, b_ada, tn=ADALN_COL_TILE):
    depth, d, n = w_ada.shape
    return pl.pallas_call(
        _adaln_kernel,
        grid=(depth, n // tn),
        in_specs=[
            pl.BlockSpec((N_COND, d), lambda l, j: (0, 0)),
            pl.BlockSpec((None, d, tn), lambda l, j: (l, 0, j)),
            pl.BlockSpec((None, 1, tn), lambda l, j: (l, 0, j)),
        ],
        out_specs=pl.BlockSpec((None, N_COND, tn), lambda l, j: (l, 0, j)),
        out_shape=jax.ShapeDtypeStruct((depth, N_COND, n), F32),
        compiler_params=_params("arbitrary", "arbitrary"),
        name="adaln",
    )(cond, w_ada, b_ada.reshape(depth, 1, n))


def _embed_kernel(xc_ref, xl_ref, pe_ref, w_ref, sh_ref, sc_ref, x_ref, h_ref, *, ctx_tiles):
    def emit(x):
        x_ref[...] = x
        h_ref[...] = (_rms(x, w_ref[...]) * (1.0 + sc_ref[...]) + sh_ref[...]).astype(BF16)

    i = pl.program_id(0)
    pl.when(i < ctx_tiles)(lambda: emit(xc_ref[...]))
    pl.when(i >= ctx_tiles)(lambda: emit(xl_ref[...] + pe_ref[...]))


def embed_norm_modulate(x_ctx, x_lat, pos_table, norm_w, layer, shift, scale, tm=EMBED_ROW_TILE):
    d = x_ctx.shape[1]
    ctx_tiles = x_ctx.shape[0] // tm
    lat_tiles = x_lat.shape[0] // tm
    pos_tiles = pos_table.shape[0] // tm
    m = x_ctx.shape[0] + x_lat.shape[0]
    mod_spec = pl.BlockSpec((None, 1, d), lambda i: (_group_of_row(i * tm), 0, 0))
    row_spec = pl.BlockSpec((tm, d), lambda i: (i, 0))
    return pl.pallas_call(
        functools.partial(_embed_kernel, ctx_tiles=ctx_tiles),
        grid=(ctx_tiles + lat_tiles,),
        in_specs=[
            pl.BlockSpec((tm, d), lambda i: (jnp.minimum(i, ctx_tiles - 1), 0)),
            pl.BlockSpec((tm, d), lambda i: (jnp.clip(i - ctx_tiles, 0, lat_tiles - 1), 0)),
            pl.BlockSpec((tm, d), lambda i: (jnp.maximum(i - ctx_tiles, 0) % pos_tiles, 0)),
            pl.BlockSpec((None, 1, d), lambda i: (layer, 0, 0)),
            mod_spec,
            mod_spec,
        ],
        out_specs=[row_spec, row_spec],
        out_shape=[jax.ShapeDtypeStruct((m, d), F32), jax.ShapeDtypeStruct((m, d), BF16)],
        compiler_params=_params("arbitrary"),
        name="embed_norm_modulate",
    )(x_ctx, x_lat, pos_table, norm_w.reshape(norm_w.shape[0], 1, d), shift, scale)


def _mm_kernel(*refs, part_tiles, n_extra, epilogue, row_pieces):
    n_parts = len(part_tiles)
    x_refs = refs[:n_parts]
    w_ref = refs[n_parts]
    extra = refs[n_parts + 1:n_parts + 1 + n_extra]
    o_ref = refs[n_parts + 1 + n_extra]
    wb_ref = refs[n_parts + 2 + n_extra]
    i = pl.program_id(1)

    @pl.when(i == 0)
    def _():
        wb_ref[...] = w_ref[...].astype(BF16)

    def run(x_ref):
        extra_vals = [r[...] for r in extra]
        rows = x_ref.shape[0] // row_pieces
        accs = []
        for s in range(row_pieces):
            accs.append(jnp.dot(x_ref[s * rows:(s + 1) * rows, :], wb_ref[...], preferred_element_type=F32))
            if s >= 1:
                epilogue(accs, s - 1, o_ref, *extra_vals)
        epilogue(accs, row_pieces - 1, o_ref, *extra_vals)

    if n_parts == 1:
        run(x_refs[0])
    else:
        first = 0
        for x_ref, tiles in zip(x_refs, part_tiles):
            pl.when(jnp.logical_and(i >= first, i < first + tiles))(functools.partial(run, x_ref))
            first += tiles


def matmul_ws(x, w, layer, n_out, *, tm, tn, epilogue, extras=(), out_dtype=F32, col_block0=0, row_pieces=1,
              name="matmul"):
    parts = list(x) if isinstance(x, (list, tuple)) else [x]
    k = parts[0].shape[1]
    part_tiles = [part.shape[0] // tm for part in parts]
    m = sum(part.shape[0] for part in parts)
    in_specs = []
    first = 0
    for tiles in part_tiles:
        in_specs.append(pl.BlockSpec(
            (tm, k), lambda j, i, first=first, tiles=tiles: (jnp.clip(i - first, 0, tiles - 1), 0)))
        first += tiles
    in_specs.append(pl.BlockSpec((None, k, tn), lambda j, i: (layer, 0, j + col_block0)))
    in_specs += [spec for _, spec in extras]
    return pl.pallas_call(
        functools.partial(_mm_kernel, part_tiles=tuple(part_tiles), n_extra=len(extras), epilogue=epilogue,
                          row_pieces=row_pieces),
        grid=(n_out // tn, m // tm),
        in_specs=in_specs,
        out_specs=pl.BlockSpec((tm, tn), lambda j, i: (i, j)),
        out_shape=jax.ShapeDtypeStruct((m, n_out), out_dtype),
        scratch_shapes=[pltpu.VMEM((k, tn), BF16)],
        compiler_params=_params("arbitrary", "arbitrary"),
        name=name,
    )(*parts, w, *[a for a, _ in extras])


def _piece_rows(accs, s):
    rows = accs[s].shape[0]
    return slice(s * rows, (s + 1) * rows)


def _ep_plain(accs, s, o_ref):
    o_ref[_piece_rows(accs, s), :] = accs[s].astype(o_ref.dtype)


def _ep_bias_gelu(accs, s, o_ref, b):
    o_ref[_piece_rows(accs, s), :] = _gelu_tanh(accs[s] + b).astype(o_ref.dtype)


def _ep_resid_gate(accs, s, o_ref, resid, gate):
    rows = _piece_rows(accs, s)
    o_ref[rows, :] = (resid[rows, :] + gate * accs[s]).astype(o_ref.dtype)


def _ep_conv_silu(accs, s, o_ref, w, *, normalize, n_q_tiles):
    p_rows = accs[s].shape[0]
    tile_rows, tn = o_ref.shape
    blk = min(p_rows, SEQ)
    assert tile_rows == DEC_SEQ and p_rows % blk == 0 and SEQ % blk == 0
    tile_is_context = pl.program_id(1) * tile_rows < N_CTX_TOK
    pad = DN_CONV // 2
    n_ext = blk + 2 * CONV_HALO
    scale = jnp.where(pl.program_id(0) < n_q_tiles, DN_DK ** -0.5, 1.0)

    def rows_at(g, n):
        return accs[g // p_rows][g % p_rows:g % p_rows + n]

    def halo(boundary, g):
        if boundary in (0, tile_rows):
            return jnp.zeros((CONV_HALO, tn), F32)
        if boundary % SEQ == 0:
            return jnp.where(tile_is_context, 0.0, rows_at(g, CONV_HALO))
        return rows_at(g, CONV_HALO)

    for b in range(p_rows // blk):
        g0 = s * p_rows + b * blk
        ext = jnp.concatenate([halo(g0, g0 - CONV_HALO), rows_at(g0, blk), halo(g0 + blk, g0 + blk)], axis=0)
        y = None
        for t in range(DN_CONV):
            shifted = ext if t == pad else pltpu.roll(ext, (pad - t) % n_ext, 0)
            term = shifted[CONV_HALO:CONV_HALO + blk] * w[t:t + 1, :]
            y = term if y is None else y + term
        y = _silu(y)
        rows = slice(g0, g0 + blk)
        if normalize:
            for hh in range(tn // DN_DK):
                lanes = slice(hh * DN_DK, (hh + 1) * DN_DK)
                sl = y[:, lanes]
                inv = lax.rsqrt(jnp.sum(sl * sl, axis=-1, keepdims=True) + EPS) * scale
                o_ref[rows, lanes] = (sl * inv).astype(o_ref.dtype)
        else:
            o_ref[rows, :] = y.astype(o_ref.dtype)


def matmul_resid_gate(x, w, layer, resid, gate, *, tm, tn, name):
    n_out = resid.shape[1]
    extras = (
        (resid, pl.BlockSpec((tm, tn), lambda j, i: (i, j))),
        (gate, pl.BlockSpec((None, 1, tn), lambda j, i: (_group_of_row(i * tm), 0, j))),
    )
    return matmul_ws(x, w, layer, n_out, tm=tm, tn=tn, epilogue=_ep_resid_gate, extras=extras, name=name)


def _split3(x):
    hi = x.astype(BF16)
    r1 = x - hi.astype(F32)
    mid = r1.astype(BF16)
    lo = (r1 - mid.astype(F32)).astype(BF16)
    return hi, mid, lo


def _gates_kernel(ab_ref, alog_ref, dtb_ref, o_ref):
    c = DN_CHUNK
    ii = lax.broadcasted_iota(jnp.int32, (c, c), 0)
    jj = lax.broadcasted_iota(jnp.int32, (c, c), 1)
    tril = (ii >= jj).astype(BF16)
    triu = (ii <= jj).astype(BF16)
    lane = lax.broadcasted_iota(jnp.int32, (c, ab_ref.shape[1]), 1)
    kind = lane // DN_HEADS_V
    for n in range(ab_ref.shape[0] // c):
        ab = ab_ref[n * c:(n + 1) * c, :]
        log_g = -jnp.exp(alog_ref[...]) * jax.nn.softplus(ab + dtb_ref[...])
        beta = jax.nn.sigmoid(ab)
        parts = _split3(log_g)
        cum_f = sum(jnp.dot(tril, part, preferred_element_type=F32) for part in parts)
        cum_b = sum(jnp.dot(triu, part, preferred_element_type=F32) for part in parts)
        o_ref[n * c:(n + 1) * c, :] = jnp.where(kind == 0, cum_f, jnp.where(kind == 2, cum_b, beta))


def dn_gates(ab, a_log_row, dt_bias_row, chunks_per_step=GATES_CHUNKS_PER_STEP):
    m, n = ab.shape
    rows = chunks_per_step * DN_CHUNK
    return pl.pallas_call(
        _gates_kernel,
        grid=(m // rows,),
        in_specs=[
            pl.BlockSpec((rows, n), lambda i: (i, 0)),
            pl.BlockSpec((1, n), lambda i: (0, 0)),
            pl.BlockSpec((1, n), lambda i: (0, 0)),
        ],
        out_specs=pl.BlockSpec((rows, n), lambda i: (i, 0)),
        out_shape=jax.ShapeDtypeStruct((m, n), F32),
        compiler_params=_params("arbitrary"),
        name="dn_gates",
    )(ab, a_log_row, dt_bias_row)


def _dot(a, b):
    return jnp.dot(a, b, preferred_element_type=F32)


def _inv_unit_triangular_many(a_list):
    n = a_list[0].shape[0]
    ii = lax.broadcasted_iota(jnp.int32, (n, n), 0)
    jj = lax.broadcasted_iota(jnp.int32, (n, n), 1)
    eye = jnp.where(ii == jj, 1.0, 0.0)
    xs = [eye - jnp.where((ii >> 1) == (jj >> 1), a, 0.0) for a in a_list]
    for level in range(1, int(math.log2(DN_CHUNK))):
        joins = jnp.logical_and((ii >> (level + 1)) == (jj >> (level + 1)), (ii >> level) != (jj >> level))
        ns = [jnp.where(joins, a, 0.0).astype(BF16) for a in a_list]
        xbs = [x.astype(BF16) for x in xs]
        ys = [_dot(xb, nn).astype(BF16) for xb, nn in zip(xbs, ns)]
        xs = [x - _dot(y, xb) for x, y, xb in zip(xs, ys, xbs)]
    return xs


def _dn_core_kernel(*refs, n_chunks, group, heads, has_init, has_state_out):
    q_ref, k_ref, v_ref, z_ref, gr_ref, nw_ref = refs[:6]
    pos = 6
    if has_init:
        s0f_ref, s0b_ref = refs[pos:pos + 2]
        pos += 2
    og_ref = refs[pos]
    pos += 1
    if has_state_out:
        sf_ref, sb_ref = refs[pos:pos + 2]
        pos += 2
    wm_ref, cm_ref, egl_ref, p_ref, u_ref, wv_ref, sh_ref, s_ref = refs[pos:pos + 8]

    c = DN_CHUNK
    pc = 2 * c
    dv = DN_DV
    if has_init:
        for hl in range(heads):
            s_ref[2 * hl] = jnp.concatenate([s0f_ref[2 * hl], s0f_ref[2 * hl + 1]], axis=1)
            s_ref[2 * hl + 1] = jnp.concatenate([s0b_ref[2 * hl], s0b_ref[2 * hl + 1]], axis=1)
    else:
        s_ref[...] = jnp.zeros_like(s_ref)

    ii = lax.broadcasted_iota(jnp.int32, (pc, pc), 0)
    jj = lax.broadcasted_iota(jnp.int32, (pc, pc), 1)
    chunk_shift = int(math.log2(c))
    same_head = (ii >> chunk_shift) == (jj >> chunk_shift)
    top = ii < c
    top_col = lax.broadcasted_iota(jnp.int32, (pc, 1), 0) < c
    nt_dims = (((1,), (1,)), ((), ()))

    def rows_of(chunk):
        return slice(chunk * c, (chunk + 1) * c)

    def gate_columns(units):
        return [gr_ref[hl, ch].T for hl, ch in units]

    def entry(hl, chunk, d):
        return (hl * n_chunks + chunk) * 2 + d

    def qk_lanes(hl):
        return slice(hl * DN_DK, (hl + 1) * DN_DK)

    def v_lanes(hl, hh):
        return slice((2 * hl + hh) * dv, (2 * hl + hh + 1) * dv)

    def phase_a(chunks):
        units = [(hl, ch) for hl in range(heads) for ch in chunks]
        chains = [(n, d) for n in range(len(units)) for d in (0, 1)]
        k2s, grams = [], []
        for hl, ch in units:
            kc = k_ref[rows_of(ch), qk_lanes(hl)]
            qc = q_ref[rows_of(ch), qk_lanes(hl)]
            k2 = jnp.concatenate([kc, kc], axis=0)
            k2s.append(k2)
            lhs = jnp.concatenate([kc, kc, qc, qc], axis=0)
            grams.append(lax.dot_general(lhs, k2, nt_dims, preferred_element_type=F32))
        colvs = gate_columns(units)
        a_mats, p_mats, gcols, bcols = [], [], [], []
        for n, d in chains:
            hl, ch = units[n]
            gcol, bcol = colvs[n][:, 2 * d:2 * d + 1], colvs[n][:, 2 * d + 1:2 * d + 2]
            grow = gr_ref[hl, ch][2 * d:2 * d + 1, :]
            incl = jnp.logical_and(same_head, (ii >= jj) if d == 0 else (ii <= jj))
            strict = jnp.logical_and(same_head, (ii > jj) if d == 0 else (ii < jj))
            decay = jnp.exp(jnp.where(incl, gcol - grow, MASKED_LOG_DECAY))
            a_mats.append(jnp.where(strict, bcol * grams[n][:pc] * decay, 0.0))
            p_mats.append(grams[n][pc:] * decay)
            gcols.append(gcol)
            bcols.append(bcol)
        t_invs = _inv_unit_triangular_many(a_mats)
        rhss, kfs = [], []
        for (n, d), gcol, bcol in zip(chains, gcols, bcols):
            hl, ch = units[n]
            kf = k2s[n].astype(F32)
            rows = rows_of(ch)
            vp = jnp.concatenate([v_ref[rows, v_lanes(hl, 0)], v_ref[rows, v_lanes(hl, 1)]], axis=0)
            rhss.append(jnp.concatenate([(bcol * jnp.exp(gcol)) * kf, bcol * vp], axis=1).astype(BF16))
            kfs.append(kf)
        uws = [_dot(t.astype(BF16), r) for t, r in zip(t_invs, rhss)]
        kdts, x2s, egls = [], [], []
        for (n, d), gcol, kf, uw in zip(chains, gcols, kfs, uws):
            r0 = c - 1 if d == 0 else 0
            gl0 = gcol[r0:r0 + 1]
            gl1 = gcol[c + r0:c + r0 + 1]
            gl = jnp.where(top_col, gl0, gl1)
            kdts.append((kf * jnp.exp(gl - gcol)).T.astype(BF16))
            u, wv = uw[:, :dv], uw[:, dv:]
            x2s.append(jnp.concatenate([jnp.where(top, u, 0.0), jnp.where(top, 0.0, u),
                                        jnp.where(top, wv, 0.0), jnp.where(top, 0.0, wv)], axis=1).astype(BF16))
            egls.append(jnp.concatenate([jnp.broadcast_to(jnp.exp(gl0), (1, dv)),
                                         jnp.broadcast_to(jnp.exp(gl1), (1, dv))], axis=1))
        wcs = [_dot(kdt, x2) for kdt, x2 in zip(kdts, x2s)]
        for (n, d), p_mat, uw, wc, egl in zip(chains, p_mats, uws, wcs, egls):
            e = entry(*units[n], d)
            wm_ref[e] = wc[:, :2 * dv].astype(BF16)
            cm_ref[e] = wc[:, 2 * dv:]
            egl_ref[e] = egl
            p_ref[e] = p_mat.astype(BF16)
            u_ref[e] = uw[:, :dv].astype(BF16)
            wv_ref[e] = uw[:, dv:]

    def phase_b(n, carry):
        lines = [(hl, d) for hl in range(heads) for d in (0, 1)]
        es = [entry(hl, n if d == 0 else n_chunks - 1 - n, d) for hl, d in lines]
        ss = [s_ref[2 * hl + d] for hl, d in lines]
        sbs = [s.astype(BF16) for s in ss]
        for e, sb in zip(es, sbs):
            sh_ref[e] = sb
        wms = [wm_ref[e] for e in es]
        wss = [jnp.concatenate([_dot(wm[:, :dv], sb[:, :dv]), _dot(wm[:, dv:], sb[:, dv:])], axis=1)
               for wm, sb in zip(wms, sbs)]
        for (hl, d), e, s, ws in zip(lines, es, ss, wss):
            s_ref[2 * hl + d] = egl_ref[e] * s - ws + cm_ref[e]
        return carry

    def phase_c(chunks):
        units = [(hl, ch) for hl in range(heads) for ch in chunks]
        chains = [(n, d) for n in range(len(units)) for d in (0, 1)]
        es = [entry(*units[n], d) for n, d in chains]
        rs = []
        for (n, d), e in zip(chains, es):
            hl, ch = units[n]
            lhs = jnp.concatenate([q_ref[rows_of(ch), qk_lanes(hl)], u_ref[e]], axis=0)
            rs.append(_dot(lhs, sh_ref[e]))
        v_news, q_ss = [], []
        for e, r in zip(es, rs):
            q_ss.append(jnp.concatenate([r[:c, :dv], r[:c, dv:]], axis=0))
            u_s = jnp.concatenate([r[c:2 * c, :dv], r[2 * c:, dv:]], axis=0)
            v_news.append((wv_ref[e] - u_s).astype(BF16))
        pvs = [_dot(p_ref[e], v_new) for e, v_new in zip(es, v_news)]
        colvs = gate_columns(units)
        outs = [jnp.exp(colvs[n][:, 2 * d:2 * d + 1]) * q_s + pv for (n, d), q_s, pv in zip(chains, q_ss, pvs)]
        for n, (hl, ch) in enumerate(units):
            o = outs[2 * n] + outs[2 * n + 1]
            rows = rows_of(ch)
            for hh in range(2):
                y = _rms(o[hh * c:(hh + 1) * c], nw_ref[...])
                og_ref[rows, v_lanes(hl, hh)] = (y * _silu(z_ref[rows, v_lanes(hl, hh)])).astype(og_ref.dtype)

    ready = set()
    for n in range(n_chunks):
        if n not in ready:
            batch = set(range(n, min(n + group, n_chunks)))
            batch |= set(range(max(n_chunks - n - group, 0), n_chunks - n))
            phase_a(sorted(batch - ready))
            ready |= batch
        phase_b(n, 0)
    phase_c(list(range(n_chunks)))

    if has_state_out:
        for hl in range(heads):
            for hh in range(2):
                sf_ref[2 * hl + hh] = s_ref[2 * hl][:, hh * dv:(hh + 1) * dv]
                sb_ref[2 * hl + hh] = s_ref[2 * hl + 1][:, hh * dv:(hh + 1) * dv]


def dn_core(qk, v, z, gates_row, norm_w, layer, *, seq_len, n_seq, row0, heads, group, init_states=None):
    n_chunks = seq_len // DN_CHUNK
    rb0 = row0 // seq_len
    qkw = heads * DN_DK
    pair = 2 * DN_DV
    vw = heads * pair
    pc = 2 * DN_CHUNK
    n_e = 2 * n_chunks * heads
    assert DN_HEADS_K % heads == 0 and n_chunks % (2 * group) == 0
    has_init = init_states is not None
    in_specs = [
        pl.BlockSpec((seq_len, qkw), lambda s, h: (rb0 + s, h)),
        pl.BlockSpec((seq_len, qkw), lambda s, h: (rb0 + s, DN_HEADS_K // heads + h)),
        pl.BlockSpec((seq_len, vw), lambda s, h: (rb0 + s, h)),
        pl.BlockSpec((seq_len, vw), lambda s, h: (rb0 + s, h)),
        pl.BlockSpec((heads, n_chunks, SUBLANES, 2 * DN_CHUNK), lambda s, h: (h, rb0 + s, 0, 0)),
        pl.BlockSpec((None, 1, DN_DV), lambda s, h: (layer, 0, 0)),
    ]
    args = [qk, qk, v, z, gates_row, norm_w]
    og_shape = jax.ShapeDtypeStruct((n_seq * seq_len, DN_V_DIM), BF16)
    og_spec = pl.BlockSpec((seq_len, vw), lambda s, h: (s, h))
    if has_init:
        st_spec = pl.BlockSpec((None, None, 2 * heads, DN_DK, DN_DV), lambda s, h: (s, layer, h, 0, 0))
        in_specs += [st_spec, st_spec]
        args += [init_states[0], init_states[1]]
        out_specs = og_spec
        out_shape = og_shape
    else:
        so_spec = pl.BlockSpec((None, None, 2 * heads, DN_DK, DN_DV), lambda s, h: (s, 0, h, 0, 0))
        so_shape = jax.ShapeDtypeStruct((n_seq, 1, DN_HEADS_V, DN_DK, DN_DV), F32)
        out_specs = [og_spec, so_spec, so_spec]
        out_shape = [og_shape, so_shape, so_shape]
    return pl.pallas_call(
        functools.partial(_dn_core_kernel, n_chunks=n_chunks, group=group, heads=heads, has_init=has_init,
                          has_state_out=not has_init),
        grid=(n_seq, DN_HEADS_K // heads),
        in_specs=in_specs,
        out_specs=out_specs,
        out_shape=out_shape,
        scratch_shapes=[
            pltpu.VMEM((n_e, DN_DK, pair), BF16),
            pltpu.VMEM((n_e, DN_DK, pair), F32),
            pltpu.VMEM((n_e, 1, pair), F32),
            pltpu.VMEM((n_e, pc, pc), BF16),
            pltpu.VMEM((n_e, pc, DN_DK), BF16),
            pltpu.VMEM((n_e, pc, DN_DV), F32),
            pltpu.VMEM((n_e, DN_DK, pair), BF16),
            pltpu.VMEM((2 * heads, DN_DK, pair), F32),
        ],
        compiler_params=_params("arbitrary", "arbitrary"),
        name="dn_core_lat" if has_init else "dn_core_ctx",
    )(*args)


def _cm_gate_kernel(u_ref, v_ref, lw_ref, lb_ref, ws_ref, bs_ref, o_ref):
    for n in range(u_ref.shape[0] // CM_CHUNK):
        rows = slice(n * CM_CHUNK, (n + 1) * CM_CHUNK)
        v = v_ref[rows, :].astype(F32)
        mu = jnp.mean(v, axis=-1, keepdims=True)
        vc = v - mu
        vn = vc * lax.rsqrt(jnp.mean(vc * vc, axis=-1, keepdims=True) + EPS) * lw_ref[...] + lb_ref[...]
        vn = vn.astype(BF16)
        for g in range(CM_GROUPS):
            lanes = slice(g * CM_GDIM, (g + 1) * CM_GDIM)
            sp = jnp.dot(ws_ref[g].astype(BF16), vn[:, lanes], preferred_element_type=F32) + bs_ref[g]
            o_ref[rows, lanes] = (u_ref[rows, lanes].astype(F32) * sp).astype(o_ref.dtype)


def cm_gate(zz, ln_w, ln_b, w_s, b_s_col, layer, chunks_per_step=CM_GATE_CHUNKS_PER_STEP):
    m = zz.shape[0]
    rows = chunks_per_step * CM_CHUNK
    row = pl.BlockSpec((None, 1, CM_DIM), lambda i: (layer, 0, 0))
    return pl.pallas_call(
        _cm_gate_kernel,
        grid=(m // rows,),
        in_specs=[
            pl.BlockSpec((rows, CM_DIM), lambda i: (i, 0)),
            pl.BlockSpec((rows, CM_DIM), lambda i: (i, 1)),
            row,
            row,
            pl.BlockSpec((None, CM_GROUPS, CM_CHUNK, CM_CHUNK), lambda i: (layer, 0, 0, 0)),
            pl.BlockSpec((None, CM_GROUPS, CM_CHUNK, 1), lambda i: (layer, 0, 0, 0)),
        ],
        out_specs=pl.BlockSpec((rows, CM_DIM), lambda i: (i, 0)),
        out_shape=jax.ShapeDtypeStruct((m, CM_DIM), BF16),
        compiler_params=_params("arbitrary"),
        name="cm_gate",
    )(zz, zz, ln_w, ln_b, w_s, b_s_col)


def _ffn_kernel(*refs, tail):
    x_ref, nw_ref, sh_ref, sc_ref, g_ref, w1_ref, w2_ref, tw_ref = refs[:8]
    if tail == "next_h":
        tsh_ref, tsc_ref, o_ref, hn_ref, h_ref = refs[8:]
    else:
        o_ref, h_ref = refs[8:]
    kk = pl.program_id(1)

    @pl.when(kk == 0)
    def _():
        y = _rms(x_ref[...], nw_ref[...])
        h_ref[...] = (y * (1.0 + sc_ref[...]) + sh_ref[...]).astype(BF16)
        o_ref[...] = jnp.zeros_like(o_ref)

    a = jnp.dot(h_ref[...], w1_ref[...].astype(BF16), preferred_element_type=F32)
    a = jnp.square(jnp.maximum(a, 0.0)).astype(BF16)
    o_ref[...] += jnp.dot(a, w2_ref[...].astype(BF16), preferred_element_type=F32)

    @pl.when(kk == pl.num_programs(1) - 1)
    def _():
        r = x_ref[...] + g_ref[...] * o_ref[...]
        if tail == "final_norm":
            r = _rms(r, tw_ref[...])
        else:
            hn_ref[...] = (_rms(r, tw_ref[...]) * (1.0 + tsc_ref[...]) + tsh_ref[...]).astype(BF16)
        o_ref[...] = r


def ffn(x, norm_w, layer, shift, scale, gate, w1, w2, *, tail, tail_w, tail_mod=None, row0=0, n_rows=None,
        tm=ROW_TILE, tk=FFN_K_TILE):
    d = x.shape[1]
    n_rows = x.shape[0] if n_rows is None else n_rows
    ff = w1.shape[2]
    t0 = row0 // tm
    mod_spec = pl.BlockSpec((None, 1, d), lambda i, k: (_group_of_row((t0 + i) * tm), 0, 0))
    row_spec = pl.BlockSpec((tm, d), lambda i, k: (i, 0), pipeline_mode=pl.Buffered(1))
    in_specs = [
        pl.BlockSpec((tm, d), lambda i, k: (t0 + i, 0), pipeline_mode=pl.Buffered(1)),
        pl.BlockSpec((None, 1, d), lambda i, k: (layer, 0, 0)),
        mod_spec,
        mod_spec,
        mod_spec,
        pl.BlockSpec((None, d, tk), lambda i, k: (layer, 0, k)),
        pl.BlockSpec((None, tk, d), lambda i, k: (layer, k, 0)),
        pl.BlockSpec((1, d), lambda i, k: (0, 0)),
    ]
    args = [x, norm_w.reshape(norm_w.shape[0], 1, d), shift, scale, gate, w1, w2, tail_w.reshape(1, d)]
    out_specs, out_shape = row_spec, jax.ShapeDtypeStruct((n_rows, d), F32)
    if tail == "next_h":
        in_specs += [mod_spec, mod_spec]
        args += list(tail_mod)
        out_specs = [row_spec, row_spec]
        out_shape = [out_shape, jax.ShapeDtypeStruct((n_rows, d), BF16)]
    return pl.pallas_call(
        functools.partial(_ffn_kernel, tail=tail),
        grid=(n_rows // tm, ff // tk),
        in_specs=in_specs,
        out_specs=out_specs,
        out_shape=out_shape,
        scratch_shapes=[pltpu.VMEM((tm, d), BF16)],
        compiler_params=_params("arbitrary", "arbitrary"),
        name="ffn_" + tail,
    )(*args)


def _grid_pos_embed(n_tokens):
    rows = n_tokens // GRID_W
    r = np.repeat(np.arange(rows), GRID_W).astype(np.float64)
    col = np.tile(np.arange(GRID_W), rows).astype(np.float64)
    quarter = D_MODEL // 4
    freq = 1.0 / (10000.0 ** (np.arange(quarter, dtype=np.float64) / quarter))
    ar = r[:, None] * freq[None, :]
    ac = col[:, None] * freq[None, :]
    return np.concatenate([np.sin(ar), np.cos(ar), np.sin(ac), np.cos(ac)], axis=-1)


def _deltanet_layer(x, h, j, gate, state_f, state_b, dn_w_in, dn_conv_w, dn_a_log, dn_dt_bias, dn_norm_w, dn_w_out):
    tm, tn = ROW_TILE, PROJ_COL_TILE

    def project(col0, n_cols, name, epilogue=_ep_plain, extras=(), out_dtype=F32, tn=tn, row_pieces=1):
        return matmul_ws(h, dn_w_in, j, n_cols, tm=tm, tn=tn, epilogue=epilogue, extras=extras,
                         out_dtype=out_dtype, col_block0=col0 // tn, row_pieces=row_pieces, name=name)

    def conv_w_for(col0):
        return (dn_conv_w, pl.BlockSpec((None, DN_CONV, tn), lambda jj, i: (j, 0, col0 // tn + jj)))

    qk = project(0, 2 * DN_K_DIM, "dn_in_qk", extras=(conv_w_for(0),), out_dtype=BF16, row_pieces=CONV_ROW_PIECES,
                 epilogue=functools.partial(_ep_conv_silu, normalize=True, n_q_tiles=DN_K_DIM // tn))
    v = project(2 * DN_K_DIM, DN_V_DIM, "dn_in_v", extras=(conv_w_for(2 * DN_K_DIM),), row_pieces=CONV_ROW_PIECES,
                epilogue=functools.partial(_ep_conv_silu, normalize=False, n_q_tiles=0))
    z = project(DN_QKV_DIM, DN_V_DIM, "dn_in_z")
    ab = project(DN_QKV_DIM + DN_V_DIM, DN_GATE_COLS, "dn_in_gates", tn=DN_GATE_COLS)

    zeros = jnp.zeros((DN_HEADS_V,), F32)
    a_log_row = jnp.concatenate([dn_a_log[j, 0], zeros, dn_a_log[j, 1], zeros])[None, :]
    dt_row = jnp.concatenate([dn_dt_bias[j, 0], zeros, dn_dt_bias[j, 1], zeros])[None, :]
    gates = dn_gates(ab, a_log_row, dt_row)
    m = gates.shape[0]
    g5 = gates.reshape(m // DN_CHUNK, DN_CHUNK, 4, DN_HEADS_K, 2)
    gates_row = g5.transpose(3, 0, 2, 4, 1).reshape(DN_HEADS_K, m // DN_CHUNK, 4, 2 * DN_CHUNK)
    gates_row = jnp.pad(gates_row, ((0, 0), (0, 0), (0, SUBLANES - 4), (0, 0)))
    norm_w = dn_norm_w.reshape(dn_norm_w.shape[0], 1, DN_DV)

    og_ctx, s_f, s_b = dn_core(qk, v, z, gates_row, norm_w, j, seq_len=SEQ, n_seq=BATCH, row0=0,
                               heads=4, group=1)
    og_lat = dn_core(qk, v, z, gates_row, norm_w, j, seq_len=DEC_SEQ, n_seq=DEC_BATCH, row0=N_CTX_TOK,
                     heads=2, group=2, init_states=(state_f, state_b))
    x = matmul_resid_gate([og_ctx, og_lat], dn_w_out, j, x, gate, tm=DN_OUT_ROW_TILE, tn=OUT_COL_TILE,
                          name="dn_out")
    return x, s_f, s_b


def _chunk_mlp_layer(x, h, j, gate, cm_w_in, cm_b_in, cm_ln_w, cm_ln_b, cm_w_s, cm_b_s, cm_w_out):
    n_in = 2 * CM_DIM
    tn = PROJ_COL_TILE
    bias = (cm_b_in.reshape(cm_b_in.shape[0], 1, n_in), pl.BlockSpec((None, 1, tn), lambda jj, i: (j, 0, jj)))
    zz = matmul_ws(h, cm_w_in, j, n_in, tm=ROW_TILE, tn=tn, epilogue=_ep_bias_gelu, extras=(bias,),
                   out_dtype=BF16, name="cm_in")
    n_b = cm_ln_w.shape[0]
    uv = cm_gate(zz, cm_ln_w.reshape(n_b, 1, CM_DIM), cm_ln_b.reshape(n_b, 1, CM_DIM), cm_w_s,
                 cm_b_s[..., None], j)
    return matmul_resid_gate(uv, cm_w_out, j, x, gate, tm=ROW_TILE, tn=OUT_COL_TILE, name="cm_out")


def kernel(x_prompt, x_sample, state_dn_fwd, state_dn_bwd, c, c_ctx, norm_mix_w, norm_mlp_w, w_ada, b_ada, dn_w_in, dn_conv_w, dn_A_log, dn_dt_bias, dn_norm_w, dn_w_out, cm_w_in, cm_b_in, cm_ln_w, cm_ln_b, cm_w_s, cm_b_s, cm_w_out, w_ff1, w_ff2, final_norm_w):
    cond = jnp.concatenate([c_ctx[None, :], c, jnp.zeros((N_COND - 1 - DEC_BATCH, D_MODEL), F32)], axis=0)
    mod = adaln_all(cond, w_ada, b_ada)
    mod = mod.reshape(DEPTH, N_COND, N_MOD, D_MODEL).transpose(0, 2, 1, 3)[:, :, :, None, :]
    mods = [[mod[i, t] for t in range(N_MOD)] for i in range(DEPTH)]

    x, h = embed_norm_modulate(x_prompt.reshape(N_CTX_TOK, D_MODEL), x_sample.reshape(N_LAT_TOK, D_MODEL),
                               jnp.asarray(_grid_pos_embed(DEC_SEQ), x_sample.dtype), norm_mix_w, 0,
                               mods[0][0], mods[0][1])
    new_fwd, new_bwd = [], []
    for i in range(DEPTH):
        j = i // N_MIXERS
        _, _, gate, shift2, scale2, gate2 = mods[i]
        if i % N_MIXERS == 0:
            x, s_f, s_b = _deltanet_layer(x, h, j, gate, state_dn_fwd, state_dn_bwd, dn_w_in, dn_conv_w,
                                          dn_A_log, dn_dt_bias, dn_norm_w, dn_w_out)
            new_fwd.append(s_f)
            new_bwd.append(s_b)
        else:
            x = _chunk_mlp_layer(x, h, j, gate, cm_w_in, cm_b_in, cm_ln_w, cm_ln_b, cm_w_s, cm_b_s, cm_w_out)
        mlp = functools.partial(ffn, x, norm_mlp_w, i, shift2, scale2, gate2, w_ff1, w_ff2)
        if i < DEPTH - 1:
            x, h = mlp(tail="next_h", tail_w=norm_mix_w[i + 1], tail_mod=mods[i + 1][:2])
        else:
            y_ctx = mlp(tail="final_norm", tail_w=final_norm_w, row0=0, n_rows=N_CTX_TOK)
            y_lat = mlp(tail="final_norm", tail_w=final_norm_w, row0=N_CTX_TOK, n_rows=N_LAT_TOK)

    y_prompt = y_ctx.reshape(BATCH, SEQ, D_MODEL)
    y_sample = y_lat.reshape(DEC_BATCH, DEC_SEQ, D_MODEL)
    return (y_prompt, y_sample, jnp.concatenate(new_fwd, axis=1), jnp.concatenate(new_bwd, axis=1))
```

```python
import functools
import math

import jax
import jax.numpy as jnp
import numpy as np
from jax import lax
from jax.experimental import pallas as pl
from jax.experimental.pallas import tpu as pltpu

F32 = jnp.float32
BF16 = jnp.bfloat16

D_MODEL = 2048
BATCH = 16
SEQ = 256
DEPTH = 2
DEC_BATCH = 2
DEC_SEQ = 1024
GRID_W = 64
N_MIXERS = 2
DN_DK = 128
DN_DV = 128
DN_HEADS_K = D_MODEL // DN_DK
DN_HEADS_V = 2 * DN_HEADS_K
DN_K_DIM = DN_HEADS_K * DN_DK
DN_V_DIM = DN_HEADS_V * DN_DV
DN_QKV_DIM = 2 * DN_K_DIM + DN_V_DIM
DN_GATE_COLS = 4 * DN_HEADS_V
DN_CONV = 5
DN_CHUNK = 64
CM_DIM = 2 * D_MODEL
CM_CHUNK = 128
CM_GROUPS = 16
CM_GDIM = CM_DIM // CM_GROUPS
FF_DIM = 4 * D_MODEL
N_MOD = 6
EPS = 1e-6

N_CTX_TOK = BATCH * SEQ
N_LAT_TOK = DEC_BATCH * DEC_SEQ
N_TOK = N_CTX_TOK + N_LAT_TOK
N_COND = 8
SUBLANES = 8
CONV_HALO = SUBLANES
MASKED_LOG_DECAY = -1e30
V7X_VMEM_BYTES = 64 * 1024 * 1024
VMEM_LIMIT_BYTES = V7X_VMEM_BYTES * 15 // 16

ROW_TILE = 1024
PROJ_COL_TILE = 1024
OUT_COL_TILE = 512
DN_OUT_ROW_TILE = 512
FFN_K_TILE = 512
CONV_ROW_PIECES = 1
ADALN_COL_TILE = 1024
EMBED_ROW_TILE = 512
GATES_CHUNKS_PER_STEP = 8
CM_GATE_CHUNKS_PER_STEP = 2


def _group_of_row(row0):
    return jnp.where(row0 < N_CTX_TOK, 0, 1 + (row0 - N_CTX_TOK) // DEC_SEQ)


def _params(*sem):
    return pltpu.CompilerParams(dimension_semantics=sem, vmem_limit_bytes=VMEM_LIMIT_BYTES)


def _rms(x, w):
    return x * lax.rsqrt(jnp.mean(x * x, axis=-1, keepdims=True) + EPS) * w


def _silu(x):
    half = 0.5 * x
    return half + half * jnp.tanh(half)


def _gelu_tanh(x):
    return 0.5 * x * (1.0 + jnp.tanh(math.sqrt(2.0 / math.pi) * (x + 0.044715 * (x * x * x))))


def _adaln_kernel(c_ref, w_ref, b_ref, o_ref):
    x = _silu(c_ref[...]).astype(BF16)
    acc = jnp.dot(x, w_ref[...].astype(BF16), preferred_element_type=F32)
    o_ref[...] = acc + b_ref[...]


def adaln_all(cond, w_ada, b_ada, tn=ADALN_COL_TILE):
    depth, d, n = w_ada.shape
    return pl.pallas_call(
        _adaln_kernel,
        grid=(depth, n // tn),
        in_specs=[
            pl.BlockSpec((N_COND, d), lambda l, j: (0, 0)),
            pl.BlockSpec((None, d, tn), lambda l, j: (l, 0, j)),
            pl.BlockSpec((None, 1, tn), lambda l, j: (l, 0, j)),
        ],
        out_specs=pl.BlockSpec((None, N_COND, tn), lambda l, j: (l, 0, j)),
        out_shape=jax.ShapeDtypeStruct((depth, N_COND, n), F32),
        compiler_params=_params("arbitrary", "arbitrary"),
        name="adaln",
    )(cond, w_ada, b_ada.reshape(depth, 1, n))


def _embed_kernel(xc_ref, xl_ref, pe_ref, w_ref, sh_ref, sc_ref, x_ref, h_ref, *, ctx_tiles):
    def emit(x):
        x_ref[...] = x
        h_ref[...] = (_rms(x, w_ref[...]) * (1.0 + sc_ref[...]) + sh_ref[...]).astype(BF16)

    i = pl.program_id(0)
    pl.when(i < ctx_tiles)(lambda: emit(xc_ref[...]))
    pl.when(i >= ctx_tiles)(lambda: emit(xl_ref[...] + pe_ref[...]))


def embed_norm_modulate(x_ctx, x_lat, pos_table, norm_w, layer, shift, scale, tm=EMBED_ROW_TILE):
    d = x_ctx.shape[1]
    ctx_tiles = x_ctx.shape[0] // tm
    lat_tiles = x_lat.shape[0] // tm
    pos_tiles = pos_table.shape[0] // tm
    m = x_ctx.shape[0] + x_lat.shape[0]
    mod_spec = pl.BlockSpec((None, 1, d), lambda i: (_group_of_row(i * tm), 0, 0))
    row_spec = pl.BlockSpec((tm, d), lambda i: (i, 0))
    return pl.pallas_call(
        functools.partial(_embed_kernel, ctx_tiles=ctx_tiles),
        grid=(ctx_tiles + lat_tiles,),
        in_specs=[
            pl.BlockSpec((tm, d), lambda i: (jnp.minimum(i, ctx_tiles - 1), 0)),
            pl.BlockSpec((tm, d), lambda i: (jnp.clip(i - ctx_tiles, 0, lat_tiles - 1), 0)),
            pl.BlockSpec((tm, d), lambda i: (jnp.maximum(i - ctx_tiles, 0) % pos_tiles, 0)),
            pl.BlockSpec((None, 1, d), lambda i: (layer, 0, 0)),
            mod_spec,
            mod_spec,
        ],
        out_specs=[row_spec, row_spec],
        out_shape=[jax.ShapeDtypeStruct((m, d), F32), jax.ShapeDtypeStruct((m, d), BF16)],
        compiler_params=_params("arbitrary"),
        name="embed_norm_modulate",
    )(x_ctx, x_lat, pos_table, norm_w.reshape(norm_w.shape[0], 1, d), shift, scale)


def _mm_kernel(*refs, part_tiles, n_extra, epilogue, row_pieces):
    n_parts = len(part_tiles)
    x_refs = refs[:n_parts]
    w_ref = refs[n_parts]
    extra = refs[n_parts + 1:n_parts + 1 + n_extra]
    o_ref = refs[n_parts + 1 + n_extra]
    wb_ref = refs[n_parts + 2 + n_extra]
    i = pl.program_id(1)

    @pl.when(i == 0)
    def _():
        wb_ref[...] = w_ref[...].astype(BF16)

    def run(x_ref):
        extra_vals = [r[...] for r in extra]
        rows = x_ref.shape[0] // row_pieces
        accs = []
        for s in range(row_pieces):
            accs.append(jnp.dot(x_ref[s * rows:(s + 1) * rows, :], wb_ref[...], preferred_element_type=F32))
            if s >= 1:
                epilogue(accs, s - 1, o_ref, *extra_vals)
        epilogue(accs, row_pieces - 1, o_ref, *extra_vals)

    if n_parts == 1:
        run(x_refs[0])
    else:
        first = 0
        for x_ref, tiles in zip(x_refs, part_tiles):
            pl.when(jnp.logical_and(i >= first, i < first + tiles))(functools.partial(run, x_ref))
            first += tiles


def matmul_ws(x, w, layer, n_out, *, tm, tn, epilogue, extras=(), out_dtype=F32, col_block0=0, row_pieces=1,
              name="matmul"):
    parts = list(x) if isinstance(x, (list, tuple)) else [x]
    k = parts[0].shape[1]
    part_tiles = [part.shape[0] // tm for part in parts]
    m = sum(part.shape[0] for part in parts)
    in_specs = []
    first = 0
    for tiles in part_tiles:
        in_specs.append(pl.BlockSpec(
            (tm, k), lambda j, i, first=first, tiles=tiles: (jnp.clip(i - first, 0, tiles - 1), 0)))
        first += tiles
    in_specs.append(pl.BlockSpec((None, k, tn), lambda j, i: (layer, 0, j + col_block0)))
    in_specs += [spec for _, spec in extras]
    return pl.pallas_call(
        functools.partial(_mm_kernel, part_tiles=tuple(part_tiles), n_extra=len(extras), epilogue=epilogue,
                          row_pieces=row_pieces),
        grid=(n_out // tn, m // tm),
        in_specs=in_specs,
        out_specs=pl.BlockSpec((tm, tn), lambda j, i: (i, j)),
        out_shape=jax.ShapeDtypeStruct((m, n_out), out_dtype),
        scratch_shapes=[pltpu.VMEM((k, tn), BF16)],
        compiler_params=_params("arbitrary", "arbitrary"),
        name=name,
    )(*parts, w, *[a for a, _ in extras])


def _piece_rows(accs, s):
    rows = accs[s].shape[0]
    return slice(s * rows, (s + 1) * rows)


def _ep_plain(accs, s, o_ref):
    o_ref[_piece_rows(accs, s), :] = accs[s].astype(o_ref.dtype)


def _ep_bias_gelu(accs, s, o_ref, b):
    o_ref[_piece_rows(accs, s), :] = _gelu_tanh(accs[s] + b).astype(o_ref.dtype)


def _ep_resid_gate(accs, s, o_ref, resid, gate):
    rows = _piece_rows(accs, s)
    o_ref[rows, :] = (resid[rows, :] + gate * accs[s]).astype(o_ref.dtype)


def _ep_conv_silu(accs, s, o_ref, w, *, normalize, n_q_tiles):
    p_rows = accs[s].shape[0]
    tile_rows, tn = o_ref.shape
    blk = min(p_rows, SEQ)
    assert tile_rows == DEC_SEQ and p_rows % blk == 0 and SEQ % blk == 0
    tile_is_context = pl.program_id(1) * tile_rows < N_CTX_TOK
    pad = DN_CONV // 2
    n_ext = blk + 2 * CONV_HALO
    scale = jnp.where(pl.program_id(0) < n_q_tiles, DN_DK ** -0.5, 1.0)

    def rows_at(g, n):
        return accs[g // p_rows][g % p_rows:g % p_rows + n]

    def halo(boundary, g):
        if boundary in (0, tile_rows):
            return jnp.zeros((CONV_HALO, tn), F32)
        if boundary % SEQ == 0:
            return jnp.where(tile_is_context, 0.0, rows_at(g, CONV_HALO))
        return rows_at(g, CONV_HALO)

    for b in range(p_rows // blk):
        g0 = s * p_rows + b * blk
        ext = jnp.concatenate([halo(g0, g0 - CONV_HALO), rows_at(g0, blk), halo(g0 + blk, g0 + blk)], axis=0)
        y = None
        for t in range(DN_CONV):
            shifted = ext if t == pad else pltpu.roll(ext, (pad - t) % n_ext, 0)
            term = shifted[CONV_HALO:CONV_HALO + blk] * w[t:t + 1, :]
            y = term if y is None else y + term
        y = _silu(y)
        rows = slice(g0, g0 + blk)
        if normalize:
            for hh in range(tn // DN_DK):
                lanes = slice(hh * DN_DK, (hh + 1) * DN_DK)
                sl = y[:, lanes]
                inv = lax.rsqrt(jnp.sum(sl * sl, axis=-1, keepdims=True) + EPS) * scale
                o_ref[rows, lanes] = (sl * inv).astype(o_ref.dtype)
        else:
            o_ref[rows, :] = y.astype(o_ref.dtype)


def matmul_resid_gate(x, w, layer, resid, gate, *, tm, tn, name):
    n_out = resid.shape[1]
    extras = (
        (resid, pl.BlockSpec((tm, tn), lambda j, i: (i, j))),
        (gate, pl.BlockSpec((None, 1, tn), lambda j, i: (_group_of_row(i * tm), 0, j))),
    )
    return matmul_ws(x, w, layer, n_out, tm=tm, tn=tn, epilogue=_ep_resid_gate, extras=extras, name=name)


def _split3(x):
    hi = x.astype(BF16)
    r1 = x - hi.astype(F32)
    mid = r1.astype(BF16)
    lo = (r1 - mid.astype(F32)).astype(BF16)
    return hi, mid, lo


def _gates_kernel(ab_ref, alog_ref, dtb_ref, o_ref):
    c = DN_CHUNK
    ii = lax.broadcasted_iota(jnp.int32, (c, c), 0)
    jj = lax.broadcasted_iota(jnp.int32, (c, c), 1)
    tril = (ii >= jj).astype(BF16)
    triu = (ii <= jj).astype(BF16)
    lane = lax.broadcasted_iota(jnp.int32, (c, ab_ref.shape[1]), 1)
    kind = lane // DN_HEADS_V
    for n in range(ab_ref.shape[0] // c):
        ab = ab_ref[n * c:(n + 1) * c, :]
        log_g = -jnp.exp(alog_ref[...]) * jax.nn.softplus(ab + dtb_ref[...])
        beta = jax.nn.sigmoid(ab)
        parts = _split3(log_g)
        cum_f = sum(jnp.dot(tril, part, preferred_element_type=F32) for part in parts)
        cum_b = sum(jnp.dot(triu, part, preferred_element_type=F32) for part in parts)
        o_ref[n * c:(n + 1) * c, :] = jnp.where(kind == 0, cum_f, jnp.where(kind == 2, cum_b, beta))


def dn_gates(ab, a_log_row, dt_bias_row, chunks_per_step=GATES_CHUNKS_PER_STEP):
    m, n = ab.shape
    rows = chunks_per_step * DN_CHUNK
    return pl.pallas_call(
        _gates_kernel,
        grid=(m // rows,),
        in_specs=[
            pl.BlockSpec((rows, n), lambda i: (i, 0)),
            pl.BlockSpec((1, n), lambda i: (0, 0)),
            pl.BlockSpec((1, n), lambda i: (0, 0)),
        ],
        out_specs=pl.BlockSpec((rows, n), lambda i: (i, 0)),
        out_shape=jax.ShapeDtypeStruct((m, n), F32),
        compiler_params=_params("arbitrary"),
        name="dn_gates",
    )(ab, a_log_row, dt_bias_row)


def _dot(a, b):
    return jnp.dot(a, b, preferred_element_type=F32)


def _inv_unit_triangular_many(a_list):
    n = a_list[0].shape[0]
    ii = lax.broadcasted_iota(jnp.int32, (n, n), 0)
    jj = lax.broadcasted_iota(jnp.int32, (n, n), 1)
    eye = jnp.where(ii == jj, 1.0, 0.0)
    xs = [eye - jnp.where((ii >> 1) == (jj >> 1), a, 0.0) for a in a_list]
    for level in range(1, int(math.log2(DN_CHUNK))):
        joins = jnp.logical_and((ii >> (level + 1)) == (jj >> (level + 1)), (ii >> level) != (jj >> level))
        ns = [jnp.where(joins, a, 0.0).astype(BF16) for a in a_list]
        xbs = [x.astype(BF16) for x in xs]
        ys = [_dot(xb, nn).astype(BF16) for xb, nn in zip(xbs, ns)]
        xs = [x - _dot(y, xb) for x, y, xb in zip(xs, ys, xbs)]
    return xs


def _dn_core_kernel(*refs, n_chunks, group, heads, has_init, has_state_out):
    q_ref, k_ref, v_ref, z_ref, gr_ref, nw_ref = refs[:6]
    pos = 6
    if has_init:
        s0f_ref, s0b_ref = refs[pos:pos + 2]
        pos += 2
    og_ref = refs[pos]
    pos += 1
    if has_state_out:
        sf_ref, sb_ref = refs[pos:pos + 2]
        pos += 2
    wm_ref, cm_ref, egl_ref, p_ref, u_ref, wv_ref, sh_ref, s_ref = refs[pos:pos + 8]

    c = DN_CHUNK
    pc = 2 * c
    dv = DN_DV
    if has_init:
        for hl in range(heads):
            s_ref[2 * hl] = jnp.concatenate([s0f_ref[2 * hl], s0f_ref[2 * hl + 1]], axis=1)
            s_ref[2 * hl + 1] = jnp.concatenate([s0b_ref[2 * hl], s0b_ref[2 * hl + 1]], axis=1)
    else:
        s_ref[...] = jnp.zeros_like(s_ref)

    ii = lax.broadcasted_iota(jnp.int32, (pc, pc), 0)
    jj = lax.broadcasted_iota(jnp.int32, (pc, pc), 1)
    chunk_shift = int(math.log2(c))
    same_head = (ii >> chunk_shift) == (jj >> chunk_shift)
    top = ii < c
    top_col = lax.broadcasted_iota(jnp.int32, (pc, 1), 0) < c
    nt_dims = (((1,), (1,)), ((), ()))

    def rows_of(chunk):
        return slice(chunk * c, (chunk + 1) * c)

    def gate_columns(units):
        return [gr_ref[hl, ch].T for hl, ch in units]

    def entry(hl, chunk, d):
        return (hl * n_chunks + chunk) * 2 + d

    def qk_lanes(hl):
        return slice(hl * DN_DK, (hl + 1) * DN_DK)

    def v_lanes(hl, hh):
        return slice((2 * hl + hh) * dv, (2 * hl + hh + 1) * dv)

    def phase_a(chunks):
        units = [(hl, ch) for hl in range(heads) for ch in chunks]
        chains = [(n, d) for n in range(len(units)) for d in (0, 1)]
        k2s, grams = [], []
        for hl, ch in units:
            kc = k_ref[rows_of(ch), qk_lanes(hl)]
            qc = q_ref[rows_of(ch), qk_lanes(hl)]
            k2 = jnp.concatenate([kc, kc], axis=0)
            k2s.append(k2)
            lhs = jnp.concatenate([kc, kc, qc, qc], axis=0)
            grams.append(lax.dot_general(lhs, k2, nt_dims, preferred_element_type=F32))
        colvs = gate_columns(units)
        a_mats, p_mats, gcols, bcols = [], [], [], []
        for n, d in chains:
            hl, ch = units[n]
            gcol, bcol = colvs[n][:, 2 * d:2 * d + 1], colvs[n][:, 2 * d + 1:2 * d + 2]
            grow = gr_ref[hl, ch][2 * d:2 * d + 1, :]
            incl = jnp.logical_and(same_head, (ii >= jj) if d == 0 else (ii <= jj))
            strict = jnp.logical_and(same_head, (ii > jj) if d == 0 else (ii < jj))
            decay = jnp.exp(jnp.where(incl, gcol - grow, MASKED_LOG_DECAY))
            a_mats.append(jnp.where(strict, bcol * grams[n][:pc] * decay, 0.0))
            p_mats.append(grams[n][pc:] * decay)
            gcols.append(gcol)
            bcols.append(bcol)
        t_invs = _inv_unit_triangular_many(a_mats)
        rhss, kfs = [], []
        for (n, d), gcol, bcol in zip(chains, gcols, bcols):
            hl, ch = units[n]
            kf = k2s[n].astype(F32)
            rows = rows_of(ch)
            vp = jnp.concatenate([v_ref[rows, v_lanes(hl, 0)], v_ref[rows, v_lanes(hl, 1)]], axis=0)
            rhss.append(jnp.concatenate([(bcol * jnp.exp(gcol)) * kf, bcol * vp], axis=1).astype(BF16))
            kfs.append(kf)
        uws = [_dot(t.astype(BF16), r) for t, r in zip(t_invs, rhss)]
        kdts, x2s, egls = [], [], []
        for (n, d), gcol, kf, uw in zip(chains, gcols, kfs, uws):
            r0 = c - 1 if d == 0 else 0
            gl0 = gcol[r0:r0 + 1]
            gl1 = gcol[c + r0:c + r0 + 1]
            gl = jnp.where(top_col, gl0, gl1)
            kdts.append((kf * jnp.exp(gl - gcol)).T.astype(BF16))
            u, wv = uw[:, :dv], uw[:, dv:]
            x2s.append(jnp.concatenate([jnp.where(top, u, 0.0), jnp.where(top, 0.0, u),
                                        jnp.where(top, wv, 0.0), jnp.where(top, 0.0, wv)], axis=1).astype(BF16))
            egls.append(jnp.concatenate([jnp.broadcast_to(jnp.exp(gl0), (1, dv)),
                                         jnp.broadcast_to(jnp.exp(gl1), (1, dv))], axis=1))
        wcs = [_dot(kdt, x2) for kdt, x2 in zip(kdts, x2s)]
        for (n, d), p_mat, uw, wc, egl in zip(chains, p_mats, uws, wcs, egls):
            e = entry(*units[n], d)
            wm_ref[e] = wc[:, :2 * dv].astype(BF16)
            cm_ref[e] = wc[:, 2 * dv:]
            egl_ref[e] = egl
            p_ref[e] = p_mat.astype(BF16)
            u_ref[e] = uw[:, :dv].astype(BF16)
            wv_ref[e] = uw[:, dv:]

    def phase_b(n, carry):
        lines = [(hl, d) for hl in range(heads) for d in (0, 1)]
        es = [entry(hl, n if d == 0 else n_chunks - 1 - n, d) for hl, d in lines]
        ss = [s_ref[2 * hl + d] for hl, d in lines]
        sbs = [s.astype(BF16) for s in ss]
        for e, sb in zip(es, sbs):
            sh_ref[e] = sb
        wms = [wm_ref[e] for e in es]
        wss = [jnp.concatenate([_dot(wm[:, :dv], sb[:, :dv]), _dot(wm[:, dv:], sb[:, dv:])], axis=1)
               for wm, sb in zip(wms, sbs)]
        for (hl, d), e, s, ws in zip(lines, es, ss, wss):
            s_ref[2 * hl + d] = egl_ref[e] * s - ws + cm_ref[e]
        return carry

    def phase_c(chunks):
        units = [(hl, ch) for hl in range(heads) for ch in chunks]
        chains = [(n, d) for n in range(len(units)) for d in (0, 1)]
        es = [entry(*units[n], d) for n, d in chains]
        rs = []
        for (n, d), e in zip(chains, es):
            hl, ch = units[n]
            lhs = jnp.concatenate([q_ref[rows_of(ch), qk_lanes(hl)], u_ref[e]], axis=0)
            rs.append(_dot(lhs, sh_ref[e]))
        v_news, q_ss = [], []
        for e, r in zip(es, rs):
            q_ss.append(jnp.concatenate([r[:c, :dv], r[:c, dv:]], axis=0))
            u_s = jnp.concatenate([r[c:2 * c, :dv], r[2 * c:, dv:]], axis=0)
            v_news.append((wv_ref[e] - u_s).astype(BF16))
        pvs = [_dot(p_ref[e], v_new) for e, v_new in zip(es, v_news)]
        colvs = gate_columns(units)
        outs = [jnp.exp(colvs[n][:, 2 * d:2 * d + 1]) * q_s + pv for (n, d), q_s, pv in zip(chains, q_ss, pvs)]
        for n, (hl, ch) in enumerate(units):
            o = outs[2 * n] + outs[2 * n + 1]
            rows = rows_of(ch)
            for hh in range(2):
                y = _rms(o[hh * c:(hh + 1) * c], nw_ref[...])
                og_ref[rows, v_lanes(hl, hh)] = (y * _silu(z_ref[rows, v_lanes(hl, hh)])).astype(og_ref.dtype)

    ready = set()
    for n in range(n_chunks):
        if n not in ready:
            batch = set(range(n, min(n + group, n_chunks)))
            batch |= set(range(max(n_chunks - n - group, 0), n_chunks - n))
            phase_a(sorted(batch - ready))
            ready |= batch
        phase_b(n, 0)
    phase_c(list(range(n_chunks)))

    if has_state_out:
        for hl in range(heads):
            for hh in range(2):
                sf_ref[2 * hl + hh] = s_ref[2 * hl][:, hh * dv:(hh + 1) * dv]
                sb_ref[2 * hl + hh] = s_ref[2 * hl + 1][:, hh * dv:(hh + 1) * dv]


def dn_core(qk, v, z, gates_row, norm_w, layer, *, seq_len, n_seq, row0, heads, group, init_states=None):
    n_chunks = seq_len // DN_CHUNK
    rb0 = row0 // seq_len
    qkw = heads * DN_DK
    pair = 2 * DN_DV
    vw = heads * pair
    pc = 2 * DN_CHUNK
    n_e = 2 * n_chunks * heads
    assert DN_HEADS_K % heads == 0 and n_chunks % (2 * group) == 0
    has_init = init_states is not None
    in_specs = [
        pl.BlockSpec((seq_len, qkw), lambda s, h: (rb0 + s, h)),
        pl.BlockSpec((seq_len, qkw), lambda s, h: (rb0 + s, DN_HEADS_K // heads + h)),
        pl.BlockSpec((seq_len, vw), lambda s, h: (rb0 + s, h)),
        pl.BlockSpec((seq_len, vw), lambda s, h: (rb0 + s, h)),
        pl.BlockSpec((heads, n_chunks, SUBLANES, 2 * DN_CHUNK), lambda s, h: (h, rb0 + s, 0, 0)),
        pl.BlockSpec((None, 1, DN_DV), lambda s, h: (layer, 0, 0)),
    ]
    args = [qk, qk, v, z, gates_row, norm_w]
    og_shape = jax.ShapeDtypeStruct((n_seq * seq_len, DN_V_DIM), BF16)
    og_spec = pl.BlockSpec((seq_len, vw), lambda s, h: (s, h))
    if has_init:
        st_spec = pl.BlockSpec((None, None, 2 * heads, DN_DK, DN_DV), lambda s, h: (s, layer, h, 0, 0))
        in_specs += [st_spec, st_spec]
        args += [init_states[0], init_states[1]]
        out_specs = og_spec
        out_shape = og_shape
    else:
        so_spec = pl.BlockSpec((None, None, 2 * heads, DN_DK, DN_DV), lambda s, h: (s, 0, h, 0, 0))
        so_shape = jax.ShapeDtypeStruct((n_seq, 1, DN_HEADS_V, DN_DK, DN_DV), F32)
        out_specs = [og_spec, so_spec, so_spec]
        out_shape = [og_shape, so_shape, so_shape]
    return pl.pallas_call(
        functools.partial(_dn_core_kernel, n_chunks=n_chunks, group=group, heads=heads, has_init=has_init,
                          has_state_out=not has_init),
        grid=(n_seq, DN_HEADS_K // heads),
        in_specs=in_specs,
        out_specs=out_specs,
        out_shape=out_shape,
        scratch_shapes=[
            pltpu.VMEM((n_e, DN_DK, pair), BF16),
            pltpu.VMEM((n_e, DN_DK, pair), F32),
            pltpu.VMEM((n_e, 1, pair), F32),
            pltpu.VMEM((n_e, pc, pc), BF16),
            pltpu.VMEM((n_e, pc, DN_DK), BF16),
            pltpu.VMEM((n_e, pc, DN_DV), F32),
            pltpu.VMEM((n_e, DN_DK, pair), BF16),
            pltpu.VMEM((2 * heads, DN_DK, pair), F32),
        ],
        compiler_params=_params("arbitrary", "arbitrary"),
        name="dn_core_lat" if has_init else "dn_core_ctx",
    )(*args)


def _cm_gate_kernel(u_ref, v_ref, lw_ref, lb_ref, ws_ref, bs_ref, o_ref):
    for n in range(u_ref.shape[0] // CM_CHUNK):
        rows = slice(n * CM_CHUNK, (n + 1) * CM_CHUNK)
        v = v_ref[rows, :].astype(F32)
        mu = jnp.mean(v, axis=-1, keepdims=True)
        vc = v - mu
        vn = vc * lax.rsqrt(jnp.mean(vc * vc, axis=-1, keepdims=True) + EPS) * lw_ref[...] + lb_ref[...]
        vn = vn.astype(BF16)
        for g in range(CM_GROUPS):
            lanes = slice(g * CM_GDIM, (g + 1) * CM_GDIM)
            sp = jnp.dot(ws_ref[g].astype(BF16), vn[:, lanes], preferred_element_type=F32) + bs_ref[g]
            o_ref[rows, lanes] = (u_ref[rows, lanes].astype(F32) * sp).astype(o_ref.dtype)


def cm_gate(zz, ln_w, ln_b, w_s, b_s_col, layer, chunks_per_step=CM_GATE_CHUNKS_PER_STEP):
    m = zz.shape[0]
    rows = chunks_per_step * CM_CHUNK
    row = pl.BlockSpec((None, 1, CM_DIM), lambda i: (layer, 0, 0))
    return pl.pallas_call(
        _cm_gate_kernel,
        grid=(m // rows,),
        in_specs=[
            pl.BlockSpec((rows, CM_DIM), lambda i: (i, 0)),
            pl.BlockSpec((rows, CM_DIM), lambda i: (i, 1)),
            row,
            row,
            pl.BlockSpec((None, CM_GROUPS, CM_CHUNK, CM_CHUNK), lambda i: (layer, 0, 0, 0)),
            pl.BlockSpec((None, CM_GROUPS, CM_CHUNK, 1), lambda i: (layer, 0, 0, 0)),
        ],
        out_specs=pl.BlockSpec((rows, CM_DIM), lambda i: (i, 0)),
        out_shape=jax.ShapeDtypeStruct((m, CM_DIM), BF16),
        compiler_params=_params("arbitrary"),
        name="cm_gate",
    )(zz, zz, ln_w, ln_b, w_s, b_s_col)


def _ffn_kernel(*refs, tail):
    x_ref, nw_ref, sh_ref, sc_ref, g_ref, w1_ref, w2_ref, tw_ref = refs[:8]
    if tail == "next_h":
        tsh_ref, tsc_ref, o_ref, hn_ref, h_ref = refs[8:]
    else:
        o_ref, h_ref = refs[8:]
    kk = pl.program_id(1)

    @pl.when(kk == 0)
    def _():
        y = _rms(x_ref[...], nw_ref[...])
        h_ref[...] = (y * (1.0 + sc_ref[...]) + sh_ref[...]).astype(BF16)
        o_ref[...] = jnp.zeros_like(o_ref)

    a = jnp.dot(h_ref[...], w1_ref[...].astype(BF16), preferred_element_type=F32)
    a = jnp.square(jnp.maximum(a, 0.0)).astype(BF16)
    o_ref[...] += jnp.dot(a, w2_ref[...].astype(BF16), preferred_element_type=F32)

    @pl.when(kk == pl.num_programs(1) - 1)
    def _():
        r = x_ref[...] + g_ref[...] * o_ref[...]
        if tail == "final_norm":
            r = _rms(r, tw_ref[...])
        else:
            hn_ref[...] = (_rms(r, tw_ref[...]) * (1.0 + tsc_ref[...]) + tsh_ref[...]).astype(BF16)
        o_ref[...] = r


def ffn(x, norm_w, layer, shift, scale, gate, w1, w2, *, tail, tail_w, tail_mod=None, row0=0, n_rows=None,
        tm=ROW_TILE, tk=FFN_K_TILE):
    d = x.shape[1]
    n_rows = x.shape[0] if n_rows is None else n_rows
    ff = w1.shape[2]
    t0 = row0 // tm
    mod_spec = pl.BlockSpec((None, 1, d), lambda i, k: (_group_of_row((t0 + i) * tm), 0, 0))
    row_spec = pl.BlockSpec((tm, d), lambda i, k: (i, 0), pipeline_mode=pl.Buffered(1))
    in_specs = [
        pl.BlockSpec((tm, d), lambda i, k: (t0 + i, 0)),
        pl.BlockSpec((None, 1, d), lambda i, k: (layer, 0, 0)),
        mod_spec,
        mod_spec,
        mod_spec,
        pl.BlockSpec((None, d, tk), lambda i, k: (layer, 0, k)),
        pl.BlockSpec((None, tk, d), lambda i, k: (layer, k, 0)),
        pl.BlockSpec((1, d), lambda i, k: (0, 0)),
    ]
    args = [x, norm_w.reshape(norm_w.shape[0], 1, d), shift, scale, gate, w1, w2, tail_w.reshape(1, d)]
    out_specs, out_shape = row_spec, jax.ShapeDtypeStruct((n_rows, d), F32)
    if tail == "next_h":
        in_specs += [mod_spec, mod_spec]
        args += list(tail_mod)
        out_specs = [row_spec, row_spec]
        out_shape = [out_shape, jax.ShapeDtypeStruct((n_rows, d), BF16)]
    return pl.pallas_call(
        functools.partial(_ffn_kernel, tail=tail),
        grid=(n_rows // tm, ff // tk),
        in_specs=in_specs,
        out_specs=out_specs,
        out_shape=out_shape,
        scratch_shapes=[pltpu.VMEM((tm, d), BF16)],
        compiler_params=_params("arbitrary", "arbitrary"),
        name="ffn_" + tail,
    )(*args)


def _grid_pos_embed(n_tokens):
    rows = n_tokens // GRID_W
    r = np.repeat(np.arange(rows), GRID_W).astype(np.float64)
    col = np.tile(np.arange(GRID_W), rows).astype(np.float64)
    quarter = D_MODEL // 4
    freq = 1.0 / (10000.0 ** (np.arange(quarter, dtype=np.float64) / quarter))
    ar = r[:, None] * freq[None, :]
    ac = col[:, None] * freq[None, :]
    return np.concatenate([np.sin(ar), np.cos(ar), np.sin(ac), np.cos(ac)], axis=-1)


def _deltanet_layer(x, h, j, gate, state_f, state_b, dn_w_in, dn_conv_w, dn_a_log, dn_dt_bias, dn_norm_w, dn_w_out):
    tm, tn = ROW_TILE, PROJ_COL_TILE

    def project(col0, n_cols, name, epilogue=_ep_plain, extras=(), out_dtype=F32, tn=tn, row_pieces=1):
        return matmul_ws(h, dn_w_in, j, n_cols, tm=tm, tn=tn, epilogue=epilogue, extras=extras,
                         out_dtype=out_dtype, col_block0=col0 // tn, row_pieces=row_pieces, name=name)

    def conv_w_for(col0):
        return (dn_conv_w, pl.BlockSpec((None, DN_CONV, tn), lambda jj, i: (j, 0, col0 // tn + jj)))

    qk = project(0, 2 * DN_K_DIM, "dn_in_qk", extras=(conv_w_for(0),), out_dtype=BF16, row_pieces=CONV_ROW_PIECES,
                 epilogue=functools.partial(_ep_conv_silu, normalize=True, n_q_tiles=DN_K_DIM // tn))
    v = project(2 * DN_K_DIM, DN_V_DIM, "dn_in_v", extras=(conv_w_for(2 * DN_K_DIM),), row_pieces=CONV_ROW_PIECES,
                epilogue=functools.partial(_ep_conv_silu, normalize=False, n_q_tiles=0))
    z = project(DN_QKV_DIM, DN_V_DIM, "dn_in_z")
    ab = project(DN_QKV_DIM + DN_V_DIM, DN_GATE_COLS, "dn_in_gates", tn=DN_GATE_COLS)

    zeros = jnp.zeros((DN_HEADS_V,), F32)
    a_log_row = jnp.concatenate([dn_a_log[j, 0], zeros, dn_a_log[j, 1], zeros])[None, :]
    dt_row = jnp.concatenate([dn_dt_bias[j, 0], zeros, dn_dt_bias[j, 1], zeros])[None, :]
    gates = dn_gates(ab, a_log_row, dt_row)
    m = gates.shape[0]
    g5 = gates.reshape(m // DN_CHUNK, DN_CHUNK, 4, DN_HEADS_K, 2)
    gates_row = g5.transpose(3, 0, 2, 4, 1).reshape(DN_HEADS_K, m // DN_CHUNK, 4, 2 * DN_CHUNK)
    gates_row = jnp.pad(gates_row, ((0, 0), (0, 0), (0, SUBLANES - 4), (0, 0)))
    norm_w = dn_norm_w.reshape(dn_norm_w.shape[0], 1, DN_DV)

    og_ctx, s_f, s_b = dn_core(qk, v, z, gates_row, norm_w, j, seq_len=SEQ, n_seq=BATCH, row0=0,
                               heads=4, group=1)
    og_lat = dn_core(qk, v, z, gates_row, norm_w, j, seq_len=DEC_SEQ, n_seq=DEC_BATCH, row0=N_CTX_TOK,
                     heads=2, group=2, init_states=(state_f, state_b))
    x = matmul_resid_gate([og_ctx, og_lat], dn_w_out, j, x, gate, tm=DN_OUT_ROW_TILE, tn=OUT_COL_TILE,
                          name="dn_out")
    return x, s_f, s_b


def _chunk_mlp_layer(x, h, j, gate, cm_w_in, cm_b_in, cm_ln_w, cm_ln_b, cm_w_s, cm_b_s, cm_w_out):
    n_in = 2 * CM_DIM
    tn = PROJ_COL_TILE
    bias = (cm_b_in.reshape(cm_b_in.shape[0], 1, n_in), pl.BlockSpec((None, 1, tn), lambda jj, i: (j, 0, jj)))
    zz = matmul_ws(h, cm_w_in, j, n_in, tm=ROW_TILE, tn=tn, epilogue=_ep_bias_gelu, extras=(bias,),
                   out_dtype=BF16, name="cm_in")
    n_b = cm_ln_w.shape[0]
    uv = cm_gate(zz, cm_ln_w.reshape(n_b, 1, CM_DIM), cm_ln_b.reshape(n_b, 1, CM_DIM), cm_w_s,
                 cm_b_s[..., None], j)
    return matmul_resid_gate(uv, cm_w_out, j, x, gate, tm=ROW_TILE, tn=OUT_COL_TILE, name="cm_out")


def kernel(x_prompt, x_sample, state_dn_fwd, state_dn_bwd, c, c_ctx, norm_mix_w, norm_mlp_w, w_ada, b_ada, dn_w_in, dn_conv_w, dn_A_log, dn_dt_bias, dn_norm_w, dn_w_out, cm_w_in, cm_b_in, cm_ln_w, cm_ln_b, cm_w_s, cm_b_s, cm_w_out, w_ff1, w_ff2, final_norm_w):
    cond = jnp.concatenate([c_ctx[None, :], c, jnp.zeros((N_COND - 1 - DEC_BATCH, D_MODEL), F32)], axis=0)
    mod = adaln_all(cond, w_ada, b_ada)
    mod = mod.reshape(DEPTH, N_COND, N_MOD, D_MODEL).transpose(0, 2, 1, 3)[:, :, :, None, :]
    mods = [[mod[i, t] for t in range(N_MOD)] for i in range(DEPTH)]

    x, h = embed_norm_modulate(x_prompt.reshape(N_CTX_TOK, D_MODEL), x_sample.reshape(N_LAT_TOK, D_MODEL),
                               jnp.asarray(_grid_pos_embed(DEC_SEQ), x_sample.dtype), norm_mix_w, 0,
                               mods[0][0], mods[0][1])
    new_fwd, new_bwd = [], []
    for i in range(DEPTH):
        j = i // N_MIXERS
        _, _, gate, shift2, scale2, gate2 = mods[i]
        if i % N_MIXERS == 0:
            x, s_f, s_b = _deltanet_layer(x, h, j, gate, state_dn_fwd, state_dn_bwd, dn_w_in, dn_conv_w,
                                          dn_A_log, dn_dt_bias, dn_norm_w, dn_w_out)
            new_fwd.append(s_f)
            new_bwd.append(s_b)
        else:
            x = _chunk_mlp_layer(x, h, j, gate, cm_w_in, cm_b_in, cm_ln_w, cm_ln_b, cm_w_s, cm_b_s, cm_w_out)
        mlp = functools.partial(ffn, x, norm_mlp_w, i, shift2, scale2, gate2, w_ff1, w_ff2)
        if i < DEPTH - 1:
            x, h = mlp(tail="next_h", tail_w=norm_mix_w[i + 1], tail_mod=mods[i + 1][:2])
        else:
            y_ctx = mlp(tail="final_norm", tail_w=final_norm_w, row0=0, n_rows=N_CTX_TOK)
            y_lat = mlp(tail="final_norm", tail_w=final_norm_w, row0=N_CTX_TOK, n_rows=N_LAT_TOK)

    y_prompt = y_ctx.reshape(BATCH, SEQ, D_MODEL)
    y_sample = y_lat.reshape(DEC_BATCH, DEC_SEQ, D_MODEL)
    return (y_prompt, y_sample, jnp.concatenate(new_fwd, axis=1), jnp.concatenate(new_bwd, axis=1))
```

```python
import functools
import math

import jax
import jax.numpy as jnp
import numpy as np
from jax import lax
from jax.experimental import pallas as pl
from jax.experimental.pallas import tpu as pltpu

F32 = jnp.float32
BF16 = jnp.bfloat16

D_MODEL = 2048
BATCH = 16
SEQ = 256
DEPTH = 2
DEC_BATCH = 2
DEC_SEQ = 1024
GRID_W = 64
N_MIXERS = 2
DN_DK = 128
DN_DV = 128
DN_HEADS_K = D_MODEL // DN_DK
DN_HEADS_V = 2 * DN_HEADS_K
DN_K_DIM = DN_HEADS_K * DN_DK
DN_V_DIM = DN_HEADS_V * DN_DV
DN_QKV_DIM = 2 * DN_K_DIM + DN_V_DIM
DN_GATE_COLS = 4 * DN_HEADS_V
DN_CONV = 5
DN_CHUNK = 64
CM_DIM = 2 * D_MODEL
CM_CHUNK = 128
CM_GROUPS = 16
CM_GDIM = CM_DIM // CM_GROUPS
FF_DIM = 4 * D_MODEL
N_MOD = 6
EPS = 1e-6

N_CTX_TOK = BATCH * SEQ
N_LAT_TOK = DEC_BATCH * DEC_SEQ
N_TOK = N_CTX_TOK + N_LAT_TOK
N_COND = 8
SUBLANES = 8
CONV_HALO = SUBLANES
MASKED_LOG_DECAY = -1e30
V7X_VMEM_BYTES = 64 * 1024 * 1024
VMEM_LIMIT_BYTES = V7X_VMEM_BYTES * 15 // 16

ROW_TILE = 1024
PROJ_COL_TILE = 1024
OUT_COL_TILE = 512
DN_OUT_ROW_TILE = 512
FFN_K_TILE = 512
CONV_ROW_PIECES = 1
CONV_NORM_ROW_PIECES = 4
ADALN_COL_TILE = 2048
EMBED_ROW_TILE = 512
GATES_CHUNKS_PER_STEP = 8
CM_GATE_CHUNKS_PER_STEP = 2


def _group_of_row(row0):
    return jnp.where(row0 < N_CTX_TOK, 0, 1 + (row0 - N_CTX_TOK) // DEC_SEQ)


def _params(*sem):
    return pltpu.CompilerParams(dimension_semantics=sem, vmem_limit_bytes=VMEM_LIMIT_BYTES)


def _rms(x, w):
    return x * lax.rsqrt(jnp.mean(x * x, axis=-1, keepdims=True) + EPS) * w


def _silu(x):
    half = 0.5 * x
    return half + half * jnp.tanh(half)


def _gelu_tanh(x):
    return 0.5 * x * (1.0 + jnp.tanh(math.sqrt(2.0 / math.pi) * (x + 0.044715 * (x * x * x))))


def _adaln_kernel(c_ref, w_ref, b_ref, o_ref):
    x = _silu(c_ref[...]).astype(BF16)
    acc = jnp.dot(x, w_ref[...].astype(BF16), preferred_element_type=F32)
    o_ref[...] = acc + b_ref[...]


def adaln_all(cond, w_ada, b_ada, tn=ADALN_COL_TILE):
    depth, d, n = w_ada.shape
    return pl.pallas_call(
        _adaln_kernel,
        grid=(depth, n // tn),
        in_specs=[
            pl.BlockSpec((N_COND, d), lambda l, j: (0, 0)),
            pl.BlockSpec((None, d, tn), lambda l, j: (l, 0, j)),
            pl.BlockSpec((None, 1, tn), lambda l, j: (l, 0, j)),
        ],
        out_specs=pl.BlockSpec((None, N_COND, tn), lambda l, j: (l, 0, j)),
        out_shape=jax.ShapeDtypeStruct((depth, N_COND, n), F32),
        compiler_params=_params("arbitrary", "arbitrary"),
        name="adaln",
    )(cond, w_ada, b_ada.reshape(depth, 1, n))


def _embed_kernel(xc_ref, xl_ref, pe_ref, w_ref, sh_ref, sc_ref, x_ref, h_ref, *, ctx_tiles):
    def emit(x):
        x_ref[...] = x
        h_ref[...] = (_rms(x, w_ref[...]) * (1.0 + sc_ref[...]) + sh_ref[...]).astype(BF16)

    i = pl.program_id(0)
    pl.when(i < ctx_tiles)(lambda: emit(xc_ref[...]))
    pl.when(i >= ctx_tiles)(lambda: emit(xl_ref[...] + pe_ref[...]))


def embed_norm_modulate(x_ctx, x_lat, pos_table, norm_w, layer, shift, scale, tm=EMBED_ROW_TILE):
    d = x_ctx.shape[1]
    ctx_tiles = x_ctx.shape[0] // tm
    lat_tiles = x_lat.shape[0] // tm
    pos_tiles = pos_table.shape[0] // tm
    m = x_ctx.shape[0] + x_lat.shape[0]
    mod_spec = pl.BlockSpec((None, 1, d), lambda i: (_group_of_row(i * tm), 0, 0))
    row_spec = pl.BlockSpec((tm, d), lambda i: (i, 0))
    return pl.pallas_call(
        functools.partial(_embed_kernel, ctx_tiles=ctx_tiles),
        grid=(ctx_tiles + lat_tiles,),
        in_specs=[
            pl.BlockSpec((tm, d), lambda i: (jnp.minimum(i, ctx_tiles - 1), 0)),
            pl.BlockSpec((tm, d), lambda i: (jnp.clip(i - ctx_tiles, 0, lat_tiles - 1), 0)),
            pl.BlockSpec((tm, d), lambda i: (jnp.maximum(i - ctx_tiles, 0) % pos_tiles, 0)),
            pl.BlockSpec((None, 1, d), lambda i: (layer, 0, 0)),
            mod_spec,
            mod_spec,
        ],
        out_specs=[row_spec, row_spec],
        out_shape=[jax.ShapeDtypeStruct((m, d), F32), jax.ShapeDtypeStruct((m, d), BF16)],
        compiler_params=_params("arbitrary"),
        name="embed_norm_modulate",
    )(x_ctx, x_lat, pos_table, norm_w.reshape(norm_w.shape[0], 1, d), shift, scale)


def _mm_kernel(*refs, part_tiles, n_extra, epilogue, row_pieces):
    n_parts = len(part_tiles)
    x_refs = refs[:n_parts]
    w_ref = refs[n_parts]
    extra = refs[n_parts + 1:n_parts + 1 + n_extra]
    o_ref = refs[n_parts + 1 + n_extra]
    wb_ref = refs[n_parts + 2 + n_extra]
    i = pl.program_id(1)

    @pl.when(i == 0)
    def _():
        wb_ref[...] = w_ref[...].astype(BF16)

    def run(x_ref):
        extra_vals = [r[...] for r in extra]
        rows = x_ref.shape[0] // row_pieces
        accs = []
        for s in range(row_pieces):
            accs.append(jnp.dot(x_ref[s * rows:(s + 1) * rows, :], wb_ref[...], preferred_element_type=F32))
            if s >= 1:
                epilogue(accs, s - 1, o_ref, *extra_vals)
        epilogue(accs, row_pieces - 1, o_ref, *extra_vals)

    if n_parts == 1:
        run(x_refs[0])
    else:
        first = 0
        for x_ref, tiles in zip(x_refs, part_tiles):
            pl.when(jnp.logical_and(i >= first, i < first + tiles))(functools.partial(run, x_ref))
            first += tiles


def matmul_ws(x, w, layer, n_out, *, tm, tn, epilogue, extras=(), out_dtype=F32, col_block0=0, row_pieces=1,
              name="matmul"):
    parts = list(x) if isinstance(x, (list, tuple)) else [x]
    k = parts[0].shape[1]
    part_tiles = [part.shape[0] // tm for part in parts]
    m = sum(part.shape[0] for part in parts)
    in_specs = []
    first = 0
    for tiles in part_tiles:
        in_specs.append(pl.BlockSpec(
            (tm, k), lambda j, i, first=first, tiles=tiles: (jnp.clip(i - first, 0, tiles - 1), 0)))
        first += tiles
    in_specs.append(pl.BlockSpec((None, k, tn), lambda j, i: (layer, 0, j + col_block0)))
    in_specs += [spec for _, spec in extras]
    return pl.pallas_call(
        functools.partial(_mm_kernel, part_tiles=tuple(part_tiles), n_extra=len(extras), epilogue=epilogue,
                          row_pieces=row_pieces),
        grid=(n_out // tn, m // tm),
        in_specs=in_specs,
        out_specs=pl.BlockSpec((tm, tn), lambda j, i: (i, j)),
        out_shape=jax.ShapeDtypeStruct((m, n_out), out_dtype),
        scratch_shapes=[pltpu.VMEM((k, tn), BF16)],
        compiler_params=_params("arbitrary", "arbitrary"),
        name=name,
    )(*parts, w, *[a for a, _ in extras])


def _piece_rows(accs, s):
    rows = accs[s].shape[0]
    return slice(s * rows, (s + 1) * rows)


def _ep_plain(accs, s, o_ref):
    o_ref[_piece_rows(accs, s), :] = accs[s].astype(o_ref.dtype)


def _ep_bias_gelu(accs, s, o_ref, b):
    o_ref[_piece_rows(accs, s), :] = _gelu_tanh(accs[s] + b).astype(o_ref.dtype)


def _ep_resid_gate(accs, s, o_ref, resid, gate):
    rows = _piece_rows(accs, s)
    o_ref[rows, :] = (resid[rows, :] + gate * accs[s]).astype(o_ref.dtype)


def _ep_conv_silu(accs, s, o_ref, w, *, normalize, n_q_tiles):
    p_rows = accs[s].shape[0]
    tile_rows, tn = o_ref.shape
    blk = min(p_rows, SEQ)
    assert tile_rows == DEC_SEQ and p_rows % blk == 0 and SEQ % blk == 0
    tile_is_context = pl.program_id(1) * tile_rows < N_CTX_TOK
    pad = DN_CONV // 2
    n_ext = blk + 2 * CONV_HALO
    scale = jnp.where(pl.program_id(0) < n_q_tiles, DN_DK ** -0.5, 1.0)

    def rows_at(g, n):
        return accs[g // p_rows][g % p_rows:g % p_rows + n]

    def halo(boundary, g):
        if boundary in (0, tile_rows):
            return jnp.zeros((CONV_HALO, tn), F32)
        if boundary % SEQ == 0:
            return jnp.where(tile_is_context, 0.0, rows_at(g, CONV_HALO))
        return rows_at(g, CONV_HALO)

    for b in range(p_rows // blk):
        g0 = s * p_rows + b * blk
        ext = jnp.concatenate([halo(g0, g0 - CONV_HALO), rows_at(g0, blk), halo(g0 + blk, g0 + blk)], axis=0)
        y = None
        for t in range(DN_CONV):
            shifted = ext if t == pad else pltpu.roll(ext, (pad - t) % n_ext, 0)
            term = shifted[CONV_HALO:CONV_HALO + blk] * w[t:t + 1, :]
            y = term if y is None else y + term
        y = _silu(y)
        rows = slice(g0, g0 + blk)
        if normalize:
            for hh in range(tn // DN_DK):
                lanes = slice(hh * DN_DK, (hh + 1) * DN_DK)
                sl = y[:, lanes]
                inv = lax.rsqrt(jnp.sum(sl * sl, axis=-1, keepdims=True) + EPS) * scale
                o_ref[rows, lanes] = (sl * inv).astype(o_ref.dtype)
        else:
            o_ref[rows, :] = y.astype(o_ref.dtype)


def matmul_resid_gate(x, w, layer, resid, gate, *, tm, tn, name):
    n_out = resid.shape[1]
    extras = (
        (resid, pl.BlockSpec((tm, tn), lambda j, i: (i, j))),
        (gate, pl.BlockSpec((None, 1, tn), lambda j, i: (_group_of_row(i * tm), 0, j))),
    )
    return matmul_ws(x, w, layer, n_out, tm=tm, tn=tn, epilogue=_ep_resid_gate, extras=extras, name=name)


def _split3(x):
    hi = x.astype(BF16)
    r1 = x - hi.astype(F32)
    mid = r1.astype(BF16)
    lo = (r1 - mid.astype(F32)).astype(BF16)
    return hi, mid, lo


def _gates_kernel(ab_ref, alog_ref, dtb_ref, o_ref):
    c = DN_CHUNK
    ii = lax.broadcasted_iota(jnp.int32, (c, c), 0)
    jj = lax.broadcasted_iota(jnp.int32, (c, c), 1)
    tril = (ii >= jj).astype(BF16)
    triu = (ii <= jj).astype(BF16)
    lane = lax.broadcasted_iota(jnp.int32, (c, ab_ref.shape[1]), 1)
    kind = lane // DN_HEADS_V
    for n in range(ab_ref.shape[0] // c):
        ab = ab_ref[n * c:(n + 1) * c, :]
        log_g = -jnp.exp(alog_ref[...]) * jax.nn.softplus(ab + dtb_ref[...])
        beta = jax.nn.sigmoid(ab)
        parts = _split3(log_g)
        cum_f = sum(jnp.dot(tril, part, preferred_element_type=F32) for part in parts)
        cum_b = sum(jnp.dot(triu, part, preferred_element_type=F32) for part in parts)
        o_ref[n * c:(n + 1) * c, :] = jnp.where(kind == 0, cum_f, jnp.where(kind == 2, cum_b, beta))


def dn_gates(ab, a_log_row, dt_bias_row, chunks_per_step=GATES_CHUNKS_PER_STEP):
    m, n = ab.shape
    rows = chunks_per_step * DN_CHUNK
    return pl.pallas_call(
        _gates_kernel,
        grid=(m // rows,),
        in_specs=[
            pl.BlockSpec((rows, n), lambda i: (i, 0)),
            pl.BlockSpec((1, n), lambda i: (0, 0)),
            pl.BlockSpec((1, n), lambda i: (0, 0)),
        ],
        out_specs=pl.BlockSpec((rows, n), lambda i: (i, 0)),
        out_shape=jax.ShapeDtypeStruct((m, n), F32),
        compiler_params=_params("arbitrary"),
        name="dn_gates",
    )(ab, a_log_row, dt_bias_row)


def _dot(a, b):
    return jnp.dot(a, b, preferred_element_type=F32)


def _inv_unit_triangular_many(a_list):
    n = a_list[0].shape[0]
    ii = lax.broadcasted_iota(jnp.int32, (n, n), 0)
    jj = lax.broadcasted_iota(jnp.int32, (n, n), 1)
    eye = jnp.where(ii == jj, 1.0, 0.0)
    xs = [eye - jnp.where((ii >> 1) == (jj >> 1), a, 0.0) for a in a_list]
    for level in range(1, int(math.log2(DN_CHUNK))):
        joins = jnp.logical_and((ii >> (level + 1)) == (jj >> (level + 1)), (ii >> level) != (jj >> level))
        ns = [jnp.where(joins, a, 0.0).astype(BF16) for a in a_list]
        xbs = [x.astype(BF16) for x in xs]
        ys = [_dot(xb, nn).astype(BF16) for xb, nn in zip(xbs, ns)]
        xs = [x - _dot(y, xb) for x, y, xb in zip(xs, ys, xbs)]
    return xs


def _dn_core_kernel(*refs, n_chunks, group, heads, has_init, has_state_out):
    q_ref, k_ref, v_ref, z_ref, gr_ref, nw_ref = refs[:6]
    pos = 6
    if has_init:
        s0f_ref, s0b_ref = refs[pos:pos + 2]
        pos += 2
    og_ref = refs[pos]
    pos += 1
    if has_state_out:
        sf_ref, sb_ref = refs[pos:pos + 2]
        pos += 2
    wm_ref, cm_ref, egl_ref, p_ref, u_ref, wv_ref, sh_ref, s_ref = refs[pos:pos + 8]

    c = DN_CHUNK
    pc = 2 * c
    dv = DN_DV
    if has_init:
        for hl in range(heads):
            s_ref[2 * hl] = jnp.concatenate([s0f_ref[2 * hl], s0f_ref[2 * hl + 1]], axis=1)
            s_ref[2 * hl + 1] = jnp.concatenate([s0b_ref[2 * hl], s0b_ref[2 * hl + 1]], axis=1)
    else:
        s_ref[...] = jnp.zeros_like(s_ref)

    ii = lax.broadcasted_iota(jnp.int32, (pc, pc), 0)
    jj = lax.broadcasted_iota(jnp.int32, (pc, pc), 1)
    chunk_shift = int(math.log2(c))
    same_head = (ii >> chunk_shift) == (jj >> chunk_shift)
    top = ii < c
    top_col = lax.broadcasted_iota(jnp.int32, (pc, 1), 0) < c
    nt_dims = (((1,), (1,)), ((), ()))

    def rows_of(chunk):
        return slice(chunk * c, (chunk + 1) * c)

    def gate_columns(units):
        return [gr_ref[hl, ch].T for hl, ch in units]

    def entry(hl, chunk, d):
        return (hl * n_chunks + chunk) * 2 + d

    def qk_lanes(hl):
        return slice(hl * DN_DK, (hl + 1) * DN_DK)

    def v_lanes(hl, hh):
        return slice((2 * hl + hh) * dv, (2 * hl + hh + 1) * dv)

    def phase_a(chunks):
        units = [(hl, ch) for hl in range(heads) for ch in chunks]
        chains = [(n, d) for n in range(len(units)) for d in (0, 1)]
        k2s, grams = [], []
        for hl, ch in units:
            kc = k_ref[rows_of(ch), qk_lanes(hl)]
            qc = q_ref[rows_of(ch), qk_lanes(hl)]
            k2 = jnp.concatenate([kc, kc], axis=0)
            k2s.append(k2)
            lhs = jnp.concatenate([kc, kc, qc, qc], axis=0)
            grams.append(lax.dot_general(lhs, k2, nt_dims, preferred_element_type=F32))
        colvs = gate_columns(units)
        a_mats, p_mats, gcols, bcols = [], [], [], []
        for n, d in chains:
            hl, ch = units[n]
            gcol, bcol = colvs[n][:, 2 * d:2 * d + 1], colvs[n][:, 2 * d + 1:2 * d + 2]
            grow = gr_ref[hl, ch][2 * d:2 * d + 1, :]
            incl = jnp.logical_and(same_head, (ii >= jj) if d == 0 else (ii <= jj))
            strict = jnp.logical_and(same_head, (ii > jj) if d == 0 else (ii < jj))
            decay = jnp.exp(jnp.where(incl, gcol - grow, MASKED_LOG_DECAY))
            a_mats.append(jnp.where(strict, bcol * grams[n][:pc] * decay, 0.0))
            p_mats.append(grams[n][pc:] * decay)
            gcols.append(gcol)
            bcols.append(bcol)
        t_invs = _inv_unit_triangular_many(a_mats)
        rhss, kfs = [], []
        for (n, d), gcol, bcol in zip(chains, gcols, bcols):
            hl, ch = units[n]
            kf = k2s[n].astype(F32)
            rows = rows_of(ch)
            vp = jnp.concatenate([v_ref[rows, v_lanes(hl, 0)], v_ref[rows, v_lanes(hl, 1)]], axis=0)
            rhss.append(jnp.concatenate([(bcol * jnp.exp(gcol)) * kf, bcol * vp], axis=1).astype(BF16))
            kfs.append(kf)
        uws = [_dot(t.astype(BF16), r) for t, r in zip(t_invs, rhss)]
        kdts, x2s, egls = [], [], []
        for (n, d), gcol, kf, uw in zip(chains, gcols, kfs, uws):
            r0 = c - 1 if d == 0 else 0
            gl0 = gcol[r0:r0 + 1]
            gl1 = gcol[c + r0:c + r0 + 1]
            gl = jnp.where(top_col, gl0, gl1)
            kdts.append((kf * jnp.exp(gl - gcol)).T.astype(BF16))
            u, wv = uw[:, :dv], uw[:, dv:]
            x2s.append(jnp.concatenate([jnp.where(top, u, 0.0), jnp.where(top, 0.0, u),
                                        jnp.where(top, wv, 0.0), jnp.where(top, 0.0, wv)], axis=1).astype(BF16))
            egls.append(jnp.concatenate([jnp.broadcast_to(jnp.exp(gl0), (1, dv)),
                                         jnp.broadcast_to(jnp.exp(gl1), (1, dv))], axis=1))
        wcs = [_dot(kdt, x2) for kdt, x2 in zip(kdts, x2s)]
        for (n, d), p_mat, uw, wc, egl in zip(chains, p_mats, uws, wcs, egls):
            e = entry(*units[n], d)
            wm_ref[e] = wc[:, :2 * dv].astype(BF16)
            cm_ref[e] = wc[:, 2 * dv:]
            egl_ref[e] = egl
            p_ref[e] = p_mat.astype(BF16)
            u_ref[e] = uw[:, :dv].astype(BF16)
            wv_ref[e] = uw[:, dv:]

    def phase_b(n, carry):
        lines = [(hl, d) for hl in range(heads) for d in (0, 1)]
        es = [entry(hl, n if d == 0 else n_chunks - 1 - n, d) for hl, d in lines]
        ss = [s_ref[2 * hl + d] for hl, d in lines]
        sbs = [s.astype(BF16) for s in ss]
        for e, sb in zip(es, sbs):
            sh_ref[e] = sb
        wms = [wm_ref[e] for e in es]
        wss = [jnp.concatenate([_dot(wm[:, :dv], sb[:, :dv]), _dot(wm[:, dv:], sb[:, dv:])], axis=1)
               for wm, sb in zip(wms, sbs)]
        for (hl, d), e, s, ws in zip(lines, es, ss, wss):
            s_ref[2 * hl + d] = egl_ref[e] * s - ws + cm_ref[e]
        return carry

    def phase_c(chunks):
        units = [(hl, ch) for hl in range(heads) for ch in chunks]
        chains = [(n, d) for n in range(len(units)) for d in (0, 1)]
        es = [entry(*units[n], d) for n, d in chains]
        rs = []
        for (n, d), e in zip(chains, es):
            hl, ch = units[n]
            lhs = jnp.concatenate([q_ref[rows_of(ch), qk_lanes(hl)], u_ref[e]], axis=0)
            rs.append(_dot(lhs, sh_ref[e]))
        v_news, q_ss = [], []
        for e, r in zip(es, rs):
            q_ss.append(jnp.concatenate([r[:c, :dv], r[:c, dv:]], axis=0))
            u_s = jnp.concatenate([r[c:2 * c, :dv], r[2 * c:, dv:]], axis=0)
            v_news.append((wv_ref[e] - u_s).astype(BF16))
        pvs = [_dot(p_ref[e], v_new) for e, v_new in zip(es, v_news)]
        colvs = gate_columns(units)
        outs = [jnp.exp(colvs[n][:, 2 * d:2 * d + 1]) * q_s + pv for (n, d), q_s, pv in zip(chains, q_ss, pvs)]
        for n, (hl, ch) in enumerate(units):
            o = outs[2 * n] + outs[2 * n + 1]
            rows = rows_of(ch)
            for hh in range(2):
                y = _rms(o[hh * c:(hh + 1) * c], nw_ref[...])
                og_ref[rows, v_lanes(hl, hh)] = (y * _silu(z_ref[rows, v_lanes(hl, hh)])).astype(og_ref.dtype)

    ready = set()
    for n in range(n_chunks):
        if n not in ready:
            batch = set(range(n, min(n + group, n_chunks)))
            batch |= set(range(max(n_chunks - n - group, 0), n_chunks - n))
            phase_a(sorted(batch - ready))
            ready |= batch
        phase_b(n, 0)
    phase_c(list(range(n_chunks)))

    if has_state_out:
        for hl in range(heads):
            for hh in range(2):
                sf_ref[2 * hl + hh] = s_ref[2 * hl][:, hh * dv:(hh + 1) * dv]
                sb_ref[2 * hl + hh] = s_ref[2 * hl + 1][:, hh * dv:(hh + 1) * dv]


def dn_core(qk, v, z, gates_row, norm_w, layer, *, seq_len, n_seq, row0, heads, group, init_states=None):
    n_chunks = seq_len // DN_CHUNK
    rb0 = row0 // seq_len
    qkw = heads * DN_DK
    pair = 2 * DN_DV
    vw = heads * pair
    pc = 2 * DN_CHUNK
    n_e = 2 * n_chunks * heads
    assert DN_HEADS_K % heads == 0 and n_chunks % (2 * group) == 0
    has_init = init_states is not None
    in_specs = [
        pl.BlockSpec((seq_len, qkw), lambda s, h: (rb0 + s, h)),
        pl.BlockSpec((seq_len, qkw), lambda s, h: (rb0 + s, DN_HEADS_K // heads + h)),
        pl.BlockSpec((seq_len, vw), lambda s, h: (rb0 + s, h)),
        pl.BlockSpec((seq_len, vw), lambda s, h: (rb0 + s, h)),
        pl.BlockSpec((heads, n_chunks, SUBLANES, 2 * DN_CHUNK), lambda s, h: (h, rb0 + s, 0, 0)),
        pl.BlockSpec((None, 1, DN_DV), lambda s, h: (layer, 0, 0)),
    ]
    args = [qk, qk, v, z, gates_row, norm_w]
    og_shape = jax.ShapeDtypeStruct((n_seq * seq_len, DN_V_DIM), BF16)
    og_spec = pl.BlockSpec((seq_len, vw), lambda s, h: (s, h))
    if has_init:
        st_spec = pl.BlockSpec((None, None, 2 * heads, DN_DK, DN_DV), lambda s, h: (s, layer, h, 0, 0))
        in_specs += [st_spec, st_spec]
        args += [init_states[0], init_states[1]]
        out_specs = og_spec
        out_shape = og_shape
    else:
        so_spec = pl.BlockSpec((None, None, 2 * heads, DN_DK, DN_DV), lambda s, h: (s, 0, h, 0, 0))
        so_shape = jax.ShapeDtypeStruct((n_seq, 1, DN_HEADS_V, DN_DK, DN_DV), F32)
        out_specs = [og_spec, so_spec, so_spec]
        out_shape = [og_shape, so_shape, so_shape]
    return pl.pallas_call(
        functools.partial(_dn_core_kernel, n_chunks=n_chunks, group=group, heads=heads, has_init=has_init,
                          has_state_out=not has_init),
        grid=(n_seq, DN_HEADS_K // heads),
        in_specs=in_specs,
        out_specs=out_specs,
        out_shape=out_shape,
        scratch_shapes=[
            pltpu.VMEM((n_e, DN_DK, pair), BF16),
            pltpu.VMEM((n_e, DN_DK, pair), F32),
            pltpu.VMEM((n_e, 1, pair), F32),
            pltpu.VMEM((n_e, pc, pc), BF16),
            pltpu.VMEM((n_e, pc, DN_DK), BF16),
            pltpu.VMEM((n_e, pc, DN_DV), F32),
            pltpu.VMEM((n_e, DN_DK, pair), BF16),
            pltpu.VMEM((2 * heads, DN_DK, pair), F32),
        ],
        compiler_params=_params("arbitrary", "arbitrary"),
        name="dn_core_lat" if has_init else "dn_core_ctx",
    )(*args)


def _cm_gate_kernel(u_ref, v_ref, lw_ref, lb_ref, ws_ref, bs_ref, o_ref):
    for n in range(u_ref.shape[0] // CM_CHUNK):
        rows = slice(n * CM_CHUNK, (n + 1) * CM_CHUNK)
        v = v_ref[rows, :].astype(F32)
        mu = jnp.mean(v, axis=-1, keepdims=True)
        vc = v - mu
        vn = vc * lax.rsqrt(jnp.mean(vc * vc, axis=-1, keepdims=True) + EPS) * lw_ref[...] + lb_ref[...]
        vn = vn.astype(BF16)
        for g in range(CM_GROUPS):
            lanes = slice(g * CM_GDIM, (g + 1) * CM_GDIM)
            sp = jnp.dot(ws_ref[g].astype(BF16), vn[:, lanes], preferred_element_type=F32) + bs_ref[g]
            o_ref[rows, lanes] = (u_ref[rows, lanes].astype(F32) * sp).astype(o_ref.dtype)


def cm_gate(zz, ln_w, ln_b, w_s, b_s_col, layer, chunks_per_step=CM_GATE_CHUNKS_PER_STEP):
    m = zz.shape[0]
    rows = chunks_per_step * CM_CHUNK
    row = pl.BlockSpec((None, 1, CM_DIM), lambda i: (layer, 0, 0))
    return pl.pallas_call(
        _cm_gate_kernel,
        grid=(m // rows,),
        in_specs=[
            pl.BlockSpec((rows, CM_DIM), lambda i: (i, 0)),
            pl.BlockSpec((rows, CM_DIM), lambda i: (i, 1)),
            row,
            row,
            pl.BlockSpec((None, CM_GROUPS, CM_CHUNK, CM_CHUNK), lambda i: (layer, 0, 0, 0)),
            pl.BlockSpec((None, CM_GROUPS, CM_CHUNK, 1), lambda i: (layer, 0, 0, 0)),
        ],
        out_specs=pl.BlockSpec((rows, CM_DIM), lambda i: (i, 0)),
        out_shape=jax.ShapeDtypeStruct((m, CM_DIM), BF16),
        compiler_params=_params("arbitrary"),
        name="cm_gate",
    )(zz, zz, ln_w, ln_b, w_s, b_s_col)


def _ffn_kernel(*refs, tail):
    x_ref, nw_ref, sh_ref, sc_ref, g_ref, w1_ref, w2_ref, tw_ref = refs[:8]
    if tail == "next_h":
        tsh_ref, tsc_ref, o_ref, hn_ref, h_ref = refs[8:]
    else:
        o_ref, h_ref = refs[8:]
    kk = pl.program_id(1)

    @pl.when(kk == 0)
    def _():
        y = _rms(x_ref[...], nw_ref[...])
        h_ref[...] = (y * (1.0 + sc_ref[...]) + sh_ref[...]).astype(BF16)
        o_ref[...] = jnp.zeros_like(o_ref)

    a = jnp.dot(h_ref[...], w1_ref[...].astype(BF16), preferred_element_type=F32)
    a = jnp.square(jnp.maximum(a, 0.0)).astype(BF16)
    o_ref[...] += jnp.dot(a, w2_ref[...].astype(BF16), preferred_element_type=F32)

    @pl.when(kk == pl.num_programs(1) - 1)
    def _():
        r = x_ref[...] + g_ref[...] * o_ref[...]
        if tail == "final_norm":
            r = _rms(r, tw_ref[...])
        else:
            hn_ref[...] = (_rms(r, tw_ref[...]) * (1.0 + tsc_ref[...]) + tsh_ref[...]).astype(BF16)
        o_ref[...] = r


def ffn(x, norm_w, layer, shift, scale, gate, w1, w2, *, tail, tail_w, tail_mod=None, row0=0, n_rows=None,
        tm=ROW_TILE, tk=FFN_K_TILE):
    d = x.shape[1]
    n_rows = x.shape[0] if n_rows is None else n_rows
    ff = w1.shape[2]
    t0 = row0 // tm
    mod_spec = pl.BlockSpec((None, 1, d), lambda i, k: (_group_of_row((t0 + i) * tm), 0, 0))
    row_spec = pl.BlockSpec((tm, d), lambda i, k: (i, 0), pipeline_mode=pl.Buffered(1))
    in_specs = [
        pl.BlockSpec((tm, d), lambda i, k: (t0 + i, 0)),
        pl.BlockSpec((None, 1, d), lambda i, k: (layer, 0, 0)),
        mod_spec,
        mod_spec,
        mod_spec,
        pl.BlockSpec((None, d, tk), lambda i, k: (layer, 0, k)),
        pl.BlockSpec((None, tk, d), lambda i, k: (layer, k, 0)),
        pl.BlockSpec((1, d), lambda i, k: (0, 0)),
    ]
    args = [x, norm_w.reshape(norm_w.shape[0], 1, d), shift, scale, gate, w1, w2, tail_w.reshape(1, d)]
    out_specs, out_shape = row_spec, jax.ShapeDtypeStruct((n_rows, d), F32)
    if tail == "next_h":
        in_specs += [mod_spec, mod_spec]
        args += list(tail_mod)
        out_specs = [row_spec, row_spec]
        out_shape = [out_shape, jax.ShapeDtypeStruct((n_rows, d), BF16)]
    return pl.pallas_call(
        functools.partial(_ffn_kernel, tail=tail),
        grid=(n_rows // tm, ff // tk),
        in_specs=in_specs,
        out_specs=out_specs,
        out_shape=out_shape,
        scratch_shapes=[pltpu.VMEM((tm, d), BF16)],
        compiler_params=_params("arbitrary", "arbitrary"),
        name="ffn_" + tail,
    )(*args)


def _grid_pos_embed(n_tokens):
    rows = n_tokens // GRID_W
    r = np.repeat(np.arange(rows), GRID_W).astype(np.float64)
    col = np.tile(np.arange(GRID_W), rows).astype(np.float64)
    quarter = D_MODEL // 4
    freq = 1.0 / (10000.0 ** (np.arange(quarter, dtype=np.float64) / quarter))
    ar = r[:, None] * freq[None, :]
    ac = col[:, None] * freq[None, :]
    return np.concatenate([np.sin(ar), np.cos(ar), np.sin(ac), np.cos(ac)], axis=-1)


def _deltanet_layer(x, h, j, gate, state_f, state_b, dn_w_in, dn_conv_w, dn_a_log, dn_dt_bias, dn_norm_w, dn_w_out):
    tm, tn = ROW_TILE, PROJ_COL_TILE

    def project(col0, n_cols, name, epilogue=_ep_plain, extras=(), out_dtype=F32, tn=tn, row_pieces=1):
        return matmul_ws(h, dn_w_in, j, n_cols, tm=tm, tn=tn, epilogue=epilogue, extras=extras,
                         out_dtype=out_dtype, col_block0=col0 // tn, row_pieces=row_pieces, name=name)

    def conv_w_for(col0):
        return (dn_conv_w, pl.BlockSpec((None, DN_CONV, tn), lambda jj, i: (j, 0, col0 // tn + jj)))

    qk = project(0, 2 * DN_K_DIM, "dn_in_qk", extras=(conv_w_for(0),), out_dtype=BF16,
                 row_pieces=CONV_NORM_ROW_PIECES,
                 epilogue=functools.partial(_ep_conv_silu, normalize=True, n_q_tiles=DN_K_DIM // tn))
    v = project(2 * DN_K_DIM, DN_V_DIM, "dn_in_v", extras=(conv_w_for(2 * DN_K_DIM),), row_pieces=CONV_ROW_PIECES,
                epilogue=functools.partial(_ep_conv_silu, normalize=False, n_q_tiles=0))
    z = project(DN_QKV_DIM, DN_V_DIM, "dn_in_z")
    ab = project(DN_QKV_DIM + DN_V_DIM, DN_GATE_COLS, "dn_in_gates", tn=DN_GATE_COLS)

    zeros = jnp.zeros((DN_HEADS_V,), F32)
    a_log_row = jnp.concatenate([dn_a_log[j, 0], zeros, dn_a_log[j, 1], zeros])[None, :]
    dt_row = jnp.concatenate([dn_dt_bias[j, 0], zeros, dn_dt_bias[j, 1], zeros])[None, :]
    gates = dn_gates(ab, a_log_row, dt_row)
    m = gates.shape[0]
    g5 = gates.reshape(m // DN_CHUNK, DN_CHUNK, 4, DN_HEADS_K, 2)
    gates_row = g5.transpose(3, 0, 2, 4, 1).reshape(DN_HEADS_K, m // DN_CHUNK, 4, 2 * DN_CHUNK)
    gates_row = jnp.pad(gates_row, ((0, 0), (0, 0), (0, SUBLANES - 4), (0, 0)))
    norm_w = dn_norm_w.reshape(dn_norm_w.shape[0], 1, DN_DV)

    og_ctx, s_f, s_b = dn_core(qk, v, z, gates_row, norm_w, j, seq_len=SEQ, n_seq=BATCH, row0=0,
                               heads=4, group=1)
    og_lat = dn_core(qk, v, z, gates_row, norm_w, j, seq_len=DEC_SEQ, n_seq=DEC_BATCH, row0=N_CTX_TOK,
                     heads=2, group=2, init_states=(state_f, state_b))
    x = matmul_resid_gate([og_ctx, og_lat], dn_w_out, j, x, gate, tm=DN_OUT_ROW_TILE, tn=OUT_COL_TILE,
                          name="dn_out")
    return x, s_f, s_b


def _chunk_mlp_layer(x, h, j, gate, cm_w_in, cm_b_in, cm_ln_w, cm_ln_b, cm_w_s, cm_b_s, cm_w_out):
    n_in = 2 * CM_DIM
    tn = PROJ_COL_TILE
    bias = (cm_b_in.reshape(cm_b_in.shape[0], 1, n_in), pl.BlockSpec((None, 1, tn), lambda jj, i: (j, 0, jj)))
    zz = matmul_ws(h, cm_w_in, j, n_in, tm=ROW_TILE, tn=tn, epilogue=_ep_bias_gelu, extras=(bias,),
                   out_dtype=BF16, name="cm_in")
    n_b = cm_ln_w.shape[0]
    uv = cm_gate(zz, cm_ln_w.reshape(n_b, 1, CM_DIM), cm_ln_b.reshape(n_b, 1, CM_DIM), cm_w_s,
                 cm_b_s[..., None], j)
    return matmul_resid_gate(uv, cm_w_out, j, x, gate, tm=ROW_TILE, tn=OUT_COL_TILE, name="cm_out")


def kernel(x_prompt, x_sample, state_dn_fwd, state_dn_bwd, c, c_ctx, norm_mix_w, norm_mlp_w, w_ada, b_ada, dn_w_in, dn_conv_w, dn_A_log, dn_dt_bias, dn_norm_w, dn_w_out, cm_w_in, cm_b_in, cm_ln_w, cm_ln_b, cm_w_s, cm_b_s, cm_w_out, w_ff1, w_ff2, final_norm_w):
    cond = jnp.concatenate([c_ctx[None, :], c, jnp.zeros((N_COND - 1 - DEC_BATCH, D_MODEL), F32)], axis=0)
    mod = adaln_all(cond, w_ada, b_ada)
    mod = mod.reshape(DEPTH, N_COND, N_MOD, D_MODEL).transpose(0, 2, 1, 3)[:, :, :, None, :]
    mods = [[mod[i, t] for t in range(N_MOD)] for i in range(DEPTH)]

    x, h = embed_norm_modulate(x_prompt.reshape(N_CTX_TOK, D_MODEL), x_sample.reshape(N_LAT_TOK, D_MODEL),
                               jnp.asarray(_grid_pos_embed(DEC_SEQ), x_sample.dtype), norm_mix_w, 0,
                               mods[0][0], mods[0][1])
    new_fwd, new_bwd = [], []
    for i in range(DEPTH):
        j = i // N_MIXERS
        _, _, gate, shift2, scale2, gate2 = mods[i]
        if i % N_MIXERS == 0:
            x, s_f, s_b = _deltanet_layer(x, h, j, gate, state_dn_fwd, state_dn_bwd, dn_w_in, dn_conv_w,
                                          dn_A_log, dn_dt_bias, dn_norm_w, dn_w_out)
            new_fwd.append(s_f)
            new_bwd.append(s_b)
        else:
            x = _chunk_mlp_layer(x, h, j, gate, cm_w_in, cm_b_in, cm_ln_w, cm_ln_b, cm_w_s, cm_b_s, cm_w_out)
        mlp = functools.partial(ffn, x, norm_mlp_w, i, shift2, scale2, gate2, w_ff1, w_ff2)
        if i < DEPTH - 1:
            x, h = mlp(tail="next_h", tail_w=norm_mix_w[i + 1], tail_mod=mods[i + 1][:2])
        else:
            y_ctx = mlp(tail="final_norm", tail_w=final_norm_w, row0=0, n_rows=N_CTX_TOK)
            y_lat = mlp(tail="final_norm", tail_w=final_norm_w, row0=N_CTX_TOK, n_rows=N_LAT_TOK)

    y_prompt = y_ctx.reshape(BATCH, SEQ, D_MODEL)
    y_sample = y_lat.reshape(DEC_BATCH, DEC_SEQ, D_MODEL)
    return (y_prompt, y_sample, jnp.concatenate(new_fwd, axis=1), jnp.concatenate(new_bwd, axis=1))
```
